```python
import jax, jax.numpy as jnp
from jax import lax
import numpy as np

D_MODEL = 1024
BATCH = 2
SEQ = 8192
DEPTH = 4
DEC_BATCH = 128
DEC_SEQ = 4
PAST_LEN = 8192
PAGE_SIZE = 128

N_MIXERS = 2
N_CONV_LAYERS = (DEPTH + N_MIXERS - 1) // N_MIXERS
N_ATTN_LAYERS = DEPTH // N_MIXERS
CONV_WIDTH = 31
CONV_CH = D_MODEL
N_HEADS = 16
N_KV_HEADS = 4
HEAD_DIM = D_MODEL // N_HEADS
GROUP = N_HEADS // N_KV_HEADS
WINDOW = 128
ATTN_BLOCK = WINDOW
N_GROUPS = 8
EXPERTS_PER_GROUP = 8
N_EXPERTS = N_GROUPS * EXPERTS_PER_GROUP
TOP_K = 2
D_EXPERT = D_MODEL // 2
EXPERT_BLOCK = 128
RMS_EPS = 1e-6
LN_EPS = 1e-5

kernel_name = "hybrid_conv_swa_hmoe_step"


def rms_norm(x, g):
    x32 = x.astype(jnp.float32)
    y = x32 * lax.rsqrt(jnp.mean(x32 * x32, axis=-1, keepdims=True) + RMS_EPS)
    return (y * g.astype(jnp.float32)).astype(x.dtype)


def layer_norm(x, g, b):
    x32 = x.astype(jnp.float32)
    mu = jnp.mean(x32, axis=-1, keepdims=True)
    var = jnp.mean(jnp.square(x32 - mu), axis=-1, keepdims=True)
    y = (x32 - mu) * lax.rsqrt(var + LN_EPS) * g.astype(jnp.float32) + b.astype(jnp.float32)
    return y.astype(x.dtype)


def conv_module(h, past, w_pw1, b_pw1, w_dw, b_dw, ln_g, ln_b, w_pw2, b_pw2):
    a, gate = jnp.split(h @ w_pw1 + b_pw1, 2, axis=-1)
    u = a * jax.nn.sigmoid(gate)
    ucat = jnp.concatenate([past.astype(u.dtype), u], axis=1)
    c = lax.conv_general_dilated(ucat, w_dw[:, None, :], window_strides=(1,), padding='VALID',
                                 dimension_numbers=('NWC', 'WIO', 'NWC'),
                                 feature_group_count=CONV_CH) + b_dw
    z = jax.nn.silu(layer_norm(c, ln_g, ln_b))
    return z @ w_pw2 + b_pw2, ucat[:, -(CONV_WIDTH - 1):]


def qkv_proj(h, w_qkv, b_qkv):
    qkv = h @ w_qkv + b_qkv
    q, k, v = jnp.split(qkv, [N_HEADS * HEAD_DIM, (N_HEADS + N_KV_HEADS) * HEAD_DIM], axis=-1)
    lead = h.shape[:-1]
    return (q.reshape(*lead, N_KV_HEADS, GROUP, HEAD_DIM),
            k.reshape(*lead, N_KV_HEADS, HEAD_DIM),
            v.reshape(*lead, N_KV_HEADS, HEAD_DIM))


def band_mask(n_q, n_k):
    i = jnp.arange(n_q)[:, None]
    c = jnp.arange(n_k)[None, :]
    return (c >= i) & (c <= i + WINDOW)


def sink_attention(q, k, v, mask, sinks):
    s = jnp.einsum('...qhgd,...khd->...hgqk', q, k).astype(jnp.float32) * (HEAD_DIM ** -0.5)
    s = jnp.where(mask, s, -jnp.inf)
    sink = jnp.broadcast_to(sinks.astype(jnp.float32).reshape(N_KV_HEADS, GROUP, 1, 1), s.shape[:-1] + (1,))
    p = jax.nn.softmax(jnp.concatenate([s, sink], axis=-1), axis=-1)[..., :-1]
    return jnp.einsum('...hgqk,...khd->...qhgd', p.astype(v.dtype), v)


def swa_prompt(h, w_qkv, b_qkv, sinks, w_o, b_o):
    B, S = h.shape[0], h.shape[1]
    nb = S // ATTN_BLOCK
    q, k, v = qkv_proj(h, w_qkv, b_qkv)
    qb = q.reshape(B, nb, ATTN_BLOCK, N_KV_HEADS, GROUP, HEAD_DIM)

    def with_prev(t):
        tb = t.reshape(B, nb, ATTN_BLOCK, N_KV_HEADS, HEAD_DIM)
        prev = jnp.concatenate([jnp.zeros_like(tb[:, :1]), tb[:, :-1]], axis=1)
        return jnp.concatenate([prev, tb], axis=2)

    kb, vb = with_prev(k), with_prev(v)
    first_ok = (jnp.arange(nb)[:, None, None] > 0) | (jnp.arange(2 * ATTN_BLOCK)[None, None, :] >= ATTN_BLOCK)
    mask = band_mask(ATTN_BLOCK, 2 * ATTN_BLOCK)[None] & first_ok
    o = sink_attention(qb, kb, vb, mask[:, None, None], sinks)
    y = o.reshape(B, S, N_HEADS * HEAD_DIM) @ w_o + b_o
    return y, k[:, -WINDOW:], v[:, -WINDOW:]


def swa_sample(h, past_k, past_v, w_qkv, b_qkv, sinks, w_o, b_o):
    N, L = h.shape[0], h.shape[1]
    q, k, v = qkv_proj(h, w_qkv, b_qkv)
    kcat = jnp.concatenate([past_k.astype(k.dtype), k], axis=1)
    vcat = jnp.concatenate([past_v.astype(v.dtype), v], axis=1)
    o = sink_attention(q, kcat, vcat, band_mask(L, WINDOW + L), sinks)
    y = o.reshape(N, L, N_HEADS * HEAD_DIM) @ w_o + b_o
    return y, kcat[:, -WINDOW:], vcat[:, -WINDOW:]


def routed_experts(t, eid, gate, w_gate, w_up, w_down):
    T = t.shape[0]
    n = T * TOP_K
    n_blocks = -(-(n + N_EXPERTS * (EXPERT_BLOCK - 1)) // EXPERT_BLOCK)
    n_rows = n_blocks * EXPERT_BLOCK
    flat_e = eid.reshape(n)
    flat_t = jnp.repeat(jnp.arange(T, dtype=jnp.int32), TOP_K)
    flat_g = gate.reshape(n)
    order = jnp.argsort(flat_e)
    se = flat_e[order]
    counts = jax.ops.segment_sum(jnp.ones((n,), jnp.int32), flat_e, num_segments=N_EXPERTS)
    padded = (counts + EXPERT_BLOCK - 1) // EXPERT_BLOCK * EXPERT_BLOCK
    pad_end = jnp.cumsum(padded)
    pad_start = pad_end - padded
    start = jnp.cumsum(counts) - counts
    dest = pad_start[se] + jnp.arange(n, dtype=jnp.int32) - start[se]
    row_tok = jnp.full((n_rows,), T, jnp.int32).at[dest].set(flat_t[order])
    row_gate = jnp.zeros((n_rows,), t.dtype).at[dest].set(flat_g[order].astype(t.dtype))
    block_e = jnp.minimum(jnp.searchsorted(pad_end, jnp.arange(n_blocks) * EXPERT_BLOCK, side='right'),
                          N_EXPERTS - 1).astype(jnp.int32)
    t_pad = jnp.concatenate([t, jnp.zeros((1, t.shape[1]), t.dtype)], axis=0)
    xb = t_pad[row_tok].reshape(n_blocks, EXPERT_BLOCK, t.shape[1])

    def expert_block(args):
        xblk, e = args
        return (jax.nn.silu(xblk @ w_gate[e]) * (xblk @ w_up[e])) @ w_down[e]

    yb = lax.map(expert_block, (xb, block_e)).reshape(n_rows, t.shape[1])
    out = jnp.zeros((T + 1, t.shape[1]), t.dtype).at[row_tok].add(yb * row_gate[:, None])
    return out[:T]


def hier_moe(h, w_rg, b_rg, w_re, b_re, w_gate, w_up, w_down):
    lead = h.shape[:-1]
    t = h.reshape(-1, D_MODEL)
    T = t.shape[0]
    rows = jnp.arange(T)
    t32 = t.astype(jnp.float32)
    pg = jax.nn.softmax(t32 @ w_rg.astype(jnp.float32) + b_rg.astype(jnp.float32), axis=-1)
    gsel = jnp.argmax(pg, axis=-1).astype(jnp.int32)
    gprob = pg[rows, gsel][:, None]
    le = (t32 @ w_re.astype(jnp.float32) + b_re.astype(jnp.float32)).reshape(T, N_GROUPS, EXPERTS_PER_GROUP)
    pe = jax.nn.softmax(le[rows, gsel], axis=-1)
    top_p, top_i = lax.top_k(pe, TOP_K)
    gates = gprob * top_p / jnp.sum(top_p, axis=-1, keepdims=True)
    eid = gsel[:, None] * EXPERTS_PER_GROUP + top_i.astype(jnp.int32)
    return routed_experts(t, eid, gates, w_gate, w_up, w_down).reshape(*lead, D_MODEL)


def setup_inputs(seed: int = 0) -> dict:
    key = jax.random.key(seed)
    ks = iter(jax.random.split(key, 40))

    def nrm(shape, scale):
        return jax.random.normal(next(ks), shape, jnp.float32) * scale

    def gain(shape):
        return 1.0 + nrm(shape, 0.05)

    qkv_out = (N_HEADS + 2 * N_KV_HEADS) * HEAD_DIM
    return {
        "x_prompt": nrm((BATCH, SEQ, D_MODEL), 1.0),
        "x_sample": nrm((DEC_BATCH, DEC_SEQ, D_MODEL), 1.0),
        "cache_k": nrm((N_ATTN_LAYERS, DEC_BATCH, WINDOW, N_KV_HEADS, HEAD_DIM), 1.0),
        "cache_v": nrm((N_ATTN_LAYERS, DEC_BATCH, WINDOW, N_KV_HEADS, HEAD_DIM), 1.0),
        "state_conv": nrm((N_CONV_LAYERS, DEC_BATCH, CONV_WIDTH - 1, CONV_CH), 1.0),
        "norm_mix": gain((DEPTH, D_MODEL)),
        "norm_ffn": gain((DEPTH, D_MODEL)),
        "norm_final": gain((D_MODEL,)),
        "conv_w_pw1": nrm((N_CONV_LAYERS, D_MODEL, 2 * CONV_CH), D_MODEL ** -0.5),
        "conv_b_pw1": nrm((N_CONV_LAYERS, 2 * CONV_CH), 0.02),
        "conv_w_dw": nrm((N_CONV_LAYERS, CONV_WIDTH, CONV_CH), CONV_WIDTH ** -0.5),
        "conv_b_dw": nrm((N_CONV_LAYERS, CONV_CH), 0.02),
        "conv_ln_g": gain((N_CONV_LAYERS, CONV_CH)),
        "conv_ln_b": nrm((N_CONV_LAYERS, CONV_CH), 0.02),
        "conv_w_pw2": nrm((N_CONV_LAYERS, CONV_CH, D_MODEL), CONV_CH ** -0.5),
        "conv_b_pw2": nrm((N_CONV_LAYERS, D_MODEL), 0.02),
        "attn_w_qkv": nrm((N_ATTN_LAYERS, D_MODEL, qkv_out), D_MODEL ** -0.5),
        "attn_b_qkv": nrm((N_ATTN_LAYERS, qkv_out), 0.02),
        "attn_sinks": nrm((N_ATTN_LAYERS, N_HEADS), 0.5),
        "attn_w_o": nrm((N_ATTN_LAYERS, N_HEADS * HEAD_DIM, D_MODEL), (N_HEADS * HEAD_DIM) ** -0.5),
        "attn_b_o": nrm((N_ATTN_LAYERS, D_MODEL), 0.02),
        "moe_w_router_group": nrm((DEPTH, D_MODEL, N_GROUPS), D_MODEL ** -0.5),
        "moe_b_router_group": nrm((DEPTH, N_GROUPS), 0.01),
        "moe_w_router_expert": nrm((DEPTH, D_MODEL, N_EXPERTS), D_MODEL ** -0.5),
        "moe_b_router_expert": nrm((DEPTH, N_EXPERTS), 0.01),
        "moe_w_gate": nrm((DEPTH, N_EXPERTS, D_MODEL, D_EXPERT), D_MODEL ** -0.5),
        "moe_w_up": nrm((DEPTH, N_EXPERTS, D_MODEL, D_EXPERT), D_MODEL ** -0.5),
        "moe_w_down": nrm((DEPTH, N_EXPERTS, D_EXPERT, D_MODEL), D_EXPERT ** -0.5),
    }


def reference(x_prompt, x_sample, cache_k, cache_v, state_conv, norm_mix, norm_ffn, norm_final,
              conv_w_pw1, conv_b_pw1, conv_w_dw, conv_b_dw, conv_ln_g, conv_ln_b, conv_w_pw2, conv_b_pw2,
              attn_w_qkv, attn_b_qkv, attn_sinks, attn_w_o, attn_b_o,
              moe_w_router_group, moe_b_router_group, moe_w_router_expert, moe_b_router_expert,
              moe_w_gate, moe_w_up, moe_w_down):
    xp, xs = x_prompt, x_sample
    kp_list, vp_list, cp_list, ks_list, vs_list, cs_list = [], [], [], [], [], []
    for layer in range(DEPTH):
        j = layer // N_MIXERS
        hp = rms_norm(xp, norm_mix[layer])
        hs = rms_norm(xs, norm_mix[layer])
        if layer % N_MIXERS == 0:
            cparams = (conv_w_pw1[j], conv_b_pw1[j], conv_w_dw[j], conv_b_dw[j],
                       conv_ln_g[j], conv_ln_b[j], conv_w_pw2[j], conv_b_pw2[j])
            past0 = jnp.zeros((hp.shape[0], CONV_WIDTH - 1, CONV_CH), hp.dtype)
            yp, sp = conv_module(hp, past0, *cparams)
            ys, ss = conv_module(hs, state_conv[j], *cparams)
            cp_list.append(sp)
            cs_list.append(ss)
        else:
            aparams = (attn_w_qkv[j], attn_b_qkv[j], attn_sinks[j], attn_w_o[j], attn_b_o[j])
            yp, kp, vp = swa_prompt(hp, *aparams)
            ys, ksn, vsn = swa_sample(hs, cache_k[j], cache_v[j], *aparams)
            kp_list.append(kp)
            vp_list.append(vp)
            ks_list.append(ksn)
            vs_list.append(vsn)
        xp = xp + yp
        xs = xs + ys
        mparams = (moe_w_router_group[layer], moe_b_router_group[layer], moe_w_router_expert[layer],
                   moe_b_router_expert[layer], moe_w_gate[layer], moe_w_up[layer], moe_w_down[layer])
        xp = xp + hier_moe(rms_norm(xp, norm_ffn[layer]), *mparams)
        xs = xs + hier_moe(rms_norm(xs, norm_ffn[layer]), *mparams)
    y_prompt = rms_norm(xp, norm_final)
    y_sample = rms_norm(xs, norm_final)
    new_k_prompt = jnp.stack(kp_list)
    new_v_prompt = jnp.stack(vp_list)
    new_conv_prompt = jnp.stack(cp_list)
    new_k_sample = jnp.stack(ks_list)
    new_v_sample = jnp.stack(vs_list)
    new_conv_sample = jnp.stack(cs_list)
    return (y_prompt, y_sample, new_k_prompt, new_v_prompt, new_conv_prompt, new_k_sample, new_v_sample, new_conv_sample)
```

```python
import functools

import jax
import jax.numpy as jnp
from jax import lax
from jax.experimental import pallas as pl
from jax.experimental.pallas import tpu as pltpu

F32 = jnp.float32
BF16 = jnp.bfloat16

D_MODEL = 1024
CONV_WIDTH = 31
N_HEADS = 16
N_KV_HEADS = 4
HEAD_DIM = D_MODEL // N_HEADS
GROUP = N_HEADS // N_KV_HEADS
KV_DIM = N_KV_HEADS * HEAD_DIM
WINDOW = 128
N_GROUPS = 8
EXPERTS_PER_GROUP = 8
N_EXPERTS = N_GROUPS * EXPERTS_PER_GROUP
TOP_K = 2
D_EXPERT = D_MODEL // 2
EXPERT_BLOCK = 128
RMS_EPS = 1e-6
LN_EPS = 1e-5

LANES = 128
SUBLANES = 8
ROUTER_LANE0 = N_GROUPS
CONV_HALO = 32
CONV_ROWS = 32
VMEM_LIMIT = 56 * 1024 * 1024


def _params(sem):
    return pltpu.CompilerParams(dimension_semantics=sem, vmem_limit_bytes=VMEM_LIMIT)


def _rms(x, g):
    return x * lax.rsqrt(jnp.mean(x * x, axis=-1, keepdims=True) + RMS_EPS) * g


def _token_tile(t_total, t_prompt):
    for tm in (512, 256, 128):
        if t_total % tm == 0 and t_prompt % tm == 0:
            return tm
    raise ValueError("token counts must be multiples of 128")


def _mixer_in_kernel(*refs, has_moe, is_conv):
    if has_moe:
        x_ref, m0_ref, m1_ref, g_ref, w_ref, b_ref, xo_ref, *outs = refs
        x = x_ref[...] + m0_ref[...] + m1_ref[...]
        xo_ref[...] = x
    else:
        x_ref, g_ref, w_ref, b_ref, *outs = refs
        x = x_ref[...]
    h = _rms(x, g_ref[...]).astype(BF16)
    y = jnp.dot(h, w_ref[...], preferred_element_type=F32) + b_ref[...]
    if is_conv:
        (u_ref,) = outs
        u_ref[...] = y[:, :D_MODEL] * jax.nn.sigmoid(y[:, D_MODEL:])
    else:
        q_ref, k_ref, v_ref = outs
        q_ref[...] = (y[:, :D_MODEL] * (HEAD_DIM ** -0.5)).astype(BF16)
        k_ref[...] = y[:, D_MODEL:D_MODEL + KV_DIM]
        v_ref[...] = y[:, D_MODEL + KV_DIM:]


def _mixer_in(x, moe, g, w, b, *, is_conv, tm):
    t = x.shape[0]
    nt = t // tm
    n_out = w.shape[1]
    row = lambda i: (i, 0)
    const = lambda i: (0, 0)
    in_specs = [pl.BlockSpec((tm, D_MODEL), row)]
    args = [x]
    out_shape, out_specs = [], []
    if moe is not None:
        in_specs += [pl.BlockSpec((tm, D_MODEL), row), pl.BlockSpec((tm, D_MODEL), lambda i: (i + nt, 0))]
        args += [moe, moe]
        out_shape.append(jax.ShapeDtypeStruct((t, D_MODEL), F32))
        out_specs.append(pl.BlockSpec((tm, D_MODEL), row))
    in_specs += [pl.BlockSpec((1, D_MODEL), const), pl.BlockSpec((D_MODEL, n_out), const),
                 pl.BlockSpec((1, n_out), const)]
    args += [g.reshape(1, D_MODEL), w.astype(BF16), b.reshape(1, n_out)]
    if is_conv:
        out_shape.append(jax.ShapeDtypeStruct((t, D_MODEL), F32))
        out_specs.append(pl.BlockSpec((tm, D_MODEL), row))
    else:
        out_shape += [jax.ShapeDtypeStruct((t, D_MODEL), BF16), jax.ShapeDtypeStruct((t, KV_DIM), F32),
                      jax.ShapeDtypeStruct((t, KV_DIM), F32)]
        out_specs += [pl.BlockSpec((tm, D_MODEL), row), pl.BlockSpec((tm, KV_DIM), row),
                      pl.BlockSpec((tm, KV_DIM), row)]
    outs = pl.pallas_call(
        functools.partial(_mixer_in_kernel, has_moe=moe is not None, is_conv=is_conv),
        grid=(nt,), in_specs=in_specs, out_specs=out_specs, out_shape=out_shape,
        compiler_params=_params(("parallel",)),
        name="mixer_in_conv" if is_conv else "mixer_in_attn",
    )(*args)
    if moe is None:
        outs = [x] + list(outs)
    return outs


def _ln_silu(c, g, b):
    mu = jnp.mean(c, axis=-1, keepdims=True)
    d = c - mu
    var = jnp.mean(d * d, axis=-1, keepdims=True)
    return jax.nn.silu(d * lax.rsqrt(var + LN_EPS) * g + b)


def _conv_prompt_kernel(u_ref, w_ref, bdw_ref, lg_ref, lb_ref, z_ref, ubuf, shifted, *, ts):
    s = pl.program_id(1)
    nbuf = ts + CONV_HALO

    @pl.when(s == 0)
    def _():
        ubuf[0:CONV_HALO, :] = jnp.zeros((CONV_HALO, D_MODEL), F32)

    ubuf[CONV_HALO:nbuf, :] = u_ref[...]
    ubuf[nbuf:nbuf + SUBLANES, :] = jnp.zeros((SUBLANES, D_MODEL), F32)

    def shift_chunk(c, carry):
        r0 = pl.multiple_of(c * CONV_ROWS, CONV_ROWS)
        win = ubuf[pl.ds(r0, CONV_ROWS + SUBLANES), :]
        for b in range(1, SUBLANES):
            rolled = pltpu.roll(win, CONV_ROWS + SUBLANES - b, axis=0)
            shifted[b - 1, pl.ds(r0, CONV_ROWS), :] = rolled[:CONV_ROWS]
        return carry

    lax.fori_loop(0, nbuf // CONV_ROWS, shift_chunk, 0)
    first = CONV_HALO - (CONV_WIDTH - 1)

    def chunk(c, carry):
        base = pl.multiple_of(c * CONV_ROWS, CONV_ROWS)
        acc = jnp.zeros((CONV_ROWS, D_MODEL), F32)
        for k in range(CONV_WIDTH):
            a, b = divmod(first + k, SUBLANES)
            rows = pl.ds(base + a * SUBLANES, CONV_ROWS)
            tap = ubuf[rows, :] if b == 0 else shifted[b - 1, rows, :]
            acc = acc + w_ref[k:k + 1, :] * tap
        z = _ln_silu(acc + bdw_ref[...], lg_ref[...], lb_ref[...])
        z_ref[pl.ds(base, CONV_ROWS), :] = z.astype(BF16)
        return carry

    lax.fori_loop(0, ts // CONV_ROWS, chunk, 0)
    ubuf[0:CONV_HALO, :] = ubuf[ts:nbuf, :]


def _conv_prompt(u, w_dw, b_dw, ln_g, ln_b, *, batch, seq):
    ts = 512 if seq % 512 == 0 else seq
    ns = seq // ts
    const = lambda b, s: (0, 0)
    return pl.pallas_call(
        functools.partial(_conv_prompt_kernel, ts=ts),
        grid=(batch, ns),
        in_specs=[pl.BlockSpec((ts, D_MODEL), lambda b, s: (b * ns + s, 0)),
                  pl.BlockSpec((CONV_WIDTH, D_MODEL), const),
                  pl.BlockSpec((1, D_MODEL), const), pl.BlockSpec((1, D_MODEL), const),
                  pl.BlockSpec((1, D_MODEL), const)],
        out_specs=pl.BlockSpec((ts, D_MODEL), lambda b, s: (b * ns + s, 0)),
        out_shape=jax.ShapeDtypeStruct((batch * seq, D_MODEL), BF16),
        scratch_shapes=[pltpu.VMEM((ts + CONV_HALO + SUBLANES, D_MODEL), F32),
                        pltpu.VMEM((SUBLANES - 1, ts + CONV_HALO, D_MODEL), F32)],
        compiler_params=_params(("arbitrary", "arbitrary")),
        name="conv_prompt",
    )(u, w_dw, b_dw.reshape(1, D_MODEL), ln_g.reshape(1, D_MODEL), ln_b.reshape(1, D_MODEL))


def _conv_sample_kernel(u_ref, w_ref, bdw_ref, lg_ref, lb_ref, z_ref, *, dec_seq):
    for t in range(dec_seq):
        acc = jnp.zeros(u_ref.shape[1:], F32)
        for k in range(CONV_WIDTH):
            acc = acc + w_ref[k:k + 1, :] * u_ref[t + k]
        z_ref[t] = _ln_silu(acc + bdw_ref[...], lg_ref[...], lb_ref[...]).astype(BF16)


def _conv_sample(ucat_t, w_dw, b_dw, ln_g, ln_b, *, dec_seq):
    rows, n, _ = ucat_t.shape
    nb = 32 if n % 32 == 0 else n
    const = lambda i: (0, 0)
    return pl.pallas_call(
        functools.partial(_conv_sample_kernel, dec_seq=dec_seq),
        grid=(n // nb,),
        in_specs=[pl.BlockSpec((rows, nb, D_MODEL), lambda i: (0, i, 0)),
                  pl.BlockSpec((CONV_WIDTH, D_MODEL), const),
                  pl.BlockSpec((1, D_MODEL), const), pl.BlockSpec((1, D_MODEL), const),
                  pl.BlockSpec((1, D_MODEL), const)],
        out_specs=pl.BlockSpec((dec_seq, nb, D_MODEL), lambda i: (0, i, 0)),
        out_shape=jax.ShapeDtypeStruct((dec_seq, n, D_MODEL), BF16),
        compiler_params=_params(("parallel",)),
        name="conv_sample",
    )(ucat_t, w_dw, b_dw.reshape(1, D_MODEL), ln_g.reshape(1, D_MODEL), ln_b.reshape(1, D_MODEL))


def _softmax_with_sink(s, allowed, sink):
    s = jnp.where(allowed, s, -jnp.inf)
    m = jnp.maximum(jnp.max(s, axis=-1, keepdims=True), sink)
    e = jnp.exp(s - m)
    return e / (jnp.sum(e, axis=-1, keepdims=True) + jnp.exp(sink - m))


def _attn_prompt_kernel(sink_ref, q_ref, kp_ref, kc_ref, vp_ref, vc_ref, o_ref):
    i = pl.program_id(1)
    blk = q_ref.shape[0]
    qi = lax.broadcasted_iota(jnp.int32, (blk, 2 * blk), 0)
    kc = lax.broadcasted_iota(jnp.int32, (blk, 2 * blk), 1)
    allowed = (kc >= qi) & (kc <= qi + WINDOW) & ((i > 0) | (kc >= blk))
    for hk in range(N_KV_HEADS):
        cols = slice(hk * HEAD_DIM, (hk + 1) * HEAD_DIM)
        kcat = jnp.concatenate([kp_ref[:, cols], kc_ref[:, cols]], axis=0).astype(BF16)
        vcat = jnp.concatenate([vp_ref[:, cols], vc_ref[:, cols]], axis=0).astype(BF16)
        for pair in range(GROUP // 2):
            outs = []
            for g in (2 * pair, 2 * pair + 1):
                h = hk * GROUP + g
                q = q_ref[:, h * HEAD_DIM:(h + 1) * HEAD_DIM]
                s = lax.dot_general(q, kcat, (((1,), (1,)), ((), ())), preferred_element_type=F32)
                p = _softmax_with_sink(s, allowed, sink_ref[h])
                outs.append(jnp.dot(p.astype(BF16), vcat, preferred_element_type=F32))
            lo = (hk * GROUP + 2 * pair) * HEAD_DIM
            o_ref[:, lo:lo + 2 * HEAD_DIM] = jnp.concatenate(outs, axis=-1).astype(BF16)


def _attn_prompt(q, k, v, sinks, *, batch, seq):
    blk = WINDOW
    nb = seq // blk
    cur = lambda b, i, *_: (b * nb + i, 0)
    prev = lambda b, i, *_: (b * nb + jnp.maximum(i - 1, 0), 0)
    return pl.pallas_call(
        _attn_prompt_kernel,
        grid_spec=pltpu.PrefetchScalarGridSpec(
            num_scalar_prefetch=1, grid=(batch, nb),
            in_specs=[pl.BlockSpec((blk, D_MODEL), cur),
                      pl.BlockSpec((blk, KV_DIM), prev), pl.BlockSpec((blk, KV_DIM), cur),
                      pl.BlockSpec((blk, KV_DIM), prev), pl.BlockSpec((blk, KV_DIM), cur)],
            out_specs=pl.BlockSpec((blk, D_MODEL), cur)),
        out_shape=jax.ShapeDtypeStruct((batch * seq, D_MODEL), BF16),
        compiler_params=_params(("parallel", "parallel")),
        name="attn_prompt",
    )(sinks, q, k, k, v, v)


def _attn_sample_kernel(q_ref, k_ref, v_ref, sink_ref, o_ref, *, dec_seq):
    ns, kvh, nq, hd = q_ref.shape
    nk = k_ref.shape[2]
    q = q_ref[...].reshape(ns * kvh, nq, hd)
    k = k_ref[...].reshape(ns * kvh, nk, hd).astype(BF16)
    v = v_ref[...].reshape(ns * kvh, nk, hd).astype(BF16)
    s = jnp.einsum('bqd,bkd->bqk', q, k, preferred_element_type=F32).reshape(ns, kvh, nq, nk)
    t = lax.broadcasted_iota(jnp.int32, (nq, nk), 0) % dec_seq
    c = lax.broadcasted_iota(jnp.int32, (nq, nk), 1)
    allowed = (c >= t) & (c <= t + WINDOW)
    p = _softmax_with_sink(s, allowed[None, None], sink_ref[...][None])
    p = p.reshape(ns * kvh, nq, nk).astype(BF16)
    o = jnp.einsum('bqk,bkd->bqd', p, v, preferred_element_type=F32)
    o_ref[...] = o.reshape(ns, kvh, nq, hd)


def _attn_sample(q, kcat, vcat, sinks, *, dec_seq):
    n, kvh, nq, hd = q.shape
    nk = kcat.shape[2]
    ns = 8 if n % 8 == 0 else n
    sink_rows = jnp.repeat(sinks.reshape(N_KV_HEADS, GROUP), dec_seq, axis=1)[..., None]
    blk4 = lambda i: (i, 0, 0, 0)
    return pl.pallas_call(
        functools.partial(_attn_sample_kernel, dec_seq=dec_seq),
        grid=(n // ns,),
        in_specs=[pl.BlockSpec((ns, kvh, nq, hd), blk4), pl.BlockSpec((ns, kvh, nk, hd), blk4),
                  pl.BlockSpec((ns, kvh, nk, hd), blk4), pl.BlockSpec((kvh, nq, 1), lambda i: (0, 0, 0))],
        out_specs=pl.BlockSpec((ns, kvh, nq, hd), blk4),
        out_shape=jax.ShapeDtypeStruct((n, kvh, nq, hd), F32),
        compiler_params=_params(("parallel",)),
        name="attn_sample",
    )(q, kcat, vcat, sink_rows)


def _proj_router_kernel(x_ref, z_ref, w_ref, b_ref, g_ref, wr_ref, br_ref,
                        x1_ref, h_ref, meta_ref, cnt_ref, carry_ref):
    i = pl.program_id(0)
    tm = x_ref.shape[0]

    @pl.when(i == 0)
    def _():
        carry_ref[...] = jnp.zeros_like(carry_ref)

    x1 = x_ref[...] + (jnp.dot(z_ref[...], w_ref[...], preferred_element_type=F32) + b_ref[...])
    x1_ref[...] = x1
    h = _rms(x1, g_ref[...])
    h_ref[...] = h
    logits = jnp.dot(h, wr_ref[...], preferred_element_type=F32, precision=lax.Precision.HIGHEST) + br_ref[...]

    lane = lax.broadcasted_iota(jnp.int32, (tm, LANES), 1).astype(F32)
    first_lane = lambda hit: jnp.min(jnp.where(hit, lane, float(LANES)), axis=-1, keepdims=True)
    lg = jnp.where(lane < N_GROUPS, logits, -jnp.inf)
    mg = jnp.max(lg, axis=-1, keepdims=True)
    gsel = first_lane(lg == mg)
    gprob = 1.0 / jnp.sum(jnp.exp(lg - mg), axis=-1, keepdims=True)
    lo = ROUTER_LANE0 + gsel * EXPERTS_PER_GROUP
    in_group = (lane >= lo) & (lane < lo + EXPERTS_PER_GROUP)
    le = jnp.where(in_group, logits, -jnp.inf)
    ee = jnp.exp(le - jnp.max(le, axis=-1, keepdims=True))
    pe = jnp.where(in_group, ee / jnp.sum(ee, axis=-1, keepdims=True), -1.0)
    p1 = jnp.max(pe, axis=-1, keepdims=True)
    i1 = first_lane(pe == p1)
    pe2 = jnp.where(lane == i1, -1.0, pe)
    p2 = jnp.max(pe2, axis=-1, keepdims=True)
    i2 = first_lane(pe2 == p2)
    denom = p1 + p2
    g1 = gprob * p1 / denom
    g2 = gprob * p2 / denom
    sel1 = lane == i1
    sel2 = lane == i2
    onehot = jnp.where(sel1 | sel2, 1.0, 0.0)
    r = lax.broadcasted_iota(jnp.int32, (tm, tm), 0)
    c = lax.broadcasted_iota(jnp.int32, (tm, tm), 1)
    before = jnp.where(c < r, 1.0, 0.0).astype(BF16)
    rank = jnp.dot(before, onehot.astype(BF16), preferred_element_type=F32) + carry_ref[...]
    r1 = jnp.sum(jnp.where(sel1, rank, 0.0), axis=-1, keepdims=True)
    r2 = jnp.sum(jnp.where(sel2, rank, 0.0), axis=-1, keepdims=True)
    carry = carry_ref[...] + jnp.sum(onehot, axis=0, keepdims=True)
    carry_ref[...] = carry
    cnt_ref[...] = carry
    e1 = i1 - ROUTER_LANE0
    e2 = i2 - ROUTER_LANE0
    meta = jnp.zeros((tm, LANES), F32)
    for col, val in enumerate((e1, e2, r1, r2, g1, g2)):
        meta = jnp.where(lane == col, val, meta)
    meta_ref[...] = meta


def _proj_router(x, z, w, b, g, w_rg, b_rg, w_re, b_re):
    t = x.shape[0]
    tm = 256 if t % 256 == 0 else 128
    pad = LANES - N_GROUPS - N_EXPERTS
    wr = jnp.concatenate([w_rg, w_re, jnp.zeros((D_MODEL, pad), F32)], axis=1)
    br = jnp.concatenate([b_rg, b_re, jnp.zeros((pad,), F32)]).reshape(1, LANES)
    row = lambda i: (i, 0)
    const = lambda i: (0, 0)
    return pl.pallas_call(
        _proj_router_kernel,
        grid=(t // tm,),
        in_specs=[pl.BlockSpec((tm, D_MODEL), row), pl.BlockSpec((tm, D_MODEL), row),
                  pl.BlockSpec((D_MODEL, D_MODEL), const), pl.BlockSpec((1, D_MODEL), const),
                  pl.BlockSpec((1, D_MODEL), const), pl.BlockSpec((D_MODEL, LANES), const),
                  pl.BlockSpec((1, LANES), const)],
        out_specs=[pl.BlockSpec((tm, D_MODEL), row), pl.BlockSpec((tm, D_MODEL), row),
                   pl.BlockSpec((tm, LANES), row), pl.BlockSpec((1, LANES), const)],
        out_shape=[jax.ShapeDtypeStruct((t, D_MODEL), F32), jax.ShapeDtypeStruct((t, D_MODEL), F32),
                   jax.ShapeDtypeStruct((t, LANES), F32), jax.ShapeDtypeStruct((1, LANES), F32)],
        scratch_shapes=[pltpu.VMEM((1, LANES), F32)],
        compiler_params=_params(("arbitrary",)),
        name="proj_router",
    )(x, z, w.astype(BF16), b.reshape(1, D_MODEL), g.reshape(1, D_MODEL), wr, br)


def _experts_kernel(be_ref, src_ref, dst_ref, nused_ref,
                    h_hbm, gate_ref, wg_ref, wu_ref, wd_ref, out_hbm,
                    xbuf, ybuf, wg_bf, wu_bf, wd_bf, gsem, ssem):
    i = pl.program_id(0)
    n_blocks = pl.num_programs(0)
    nused = nused_ref[0]
    rows = EXPERT_BLOCK

    def gather_copy(blk, r, slot):
        tok = src_ref[blk * rows + r]
        return pltpu.make_async_copy(h_hbm.at[pl.ds(tok, 1), :], xbuf.at[slot, pl.ds(r, 1), :], gsem.at[slot])

    def scatter_copy(blk, r):
        dst = dst_ref[blk * rows + r]
        return pltpu.make_async_copy(ybuf.at[pl.ds(r, 1), :], out_hbm.at[pl.ds(dst, 1), :], ssem.at[0])

    def start_gather(blk, slot):
        for r in range(rows):
            gather_copy(blk, r, slot).start()

    def wait_scatter():
        for r in range(rows):
            pltpu.make_async_copy(ybuf.at[pl.ds(0, 1), :], out_hbm.at[pl.ds(0, 1), :], ssem.at[0]).wait()

    @pl.when((i == 0) & (nused > 0))
    def _():
        start_gather(0, 0)

    @pl.when(i + 1 < nused)
    def _():
        start_gather(i + 1, (i + 1) % 2)

    @pl.when(i < nused)
    def _():
        slot = i % 2
        new_expert = (i == 0) | (be_ref[i] != be_ref[jnp.maximum(i - 1, 0)])

        @pl.when(new_expert)
        def _():
            wg_bf[...] = wg_ref[0].astype(BF16)
            wu_bf[...] = wu_ref[0].astype(BF16)
            wd_bf[...] = wd_ref[0].astype(BF16)

        for r in range(rows):
            pltpu.make_async_copy(h_hbm.at[pl.ds(0, 1), :], xbuf.at[slot, pl.ds(0, 1), :], gsem.at[slot]).wait()
        x = xbuf[slot].astype(BF16)
        a = jnp.dot(x, wg_bf[...], preferred_element_type=F32)
        u = jnp.dot(x, wu_bf[...], preferred_element_type=F32)
        act = (jax.nn.silu(a) * u).astype(BF16)
        y = jnp.dot(act, wd_bf[...], preferred_element_type=F32) * gate_ref[...]

        @pl.when(i > 0)
        def _():
            wait_scatter()

        ybuf[...] = y

    @pl.when((i >= nused) & (i > 0))
    def _():
        wait_scatter()

    for r in range(rows):
        scatter_copy(i, r).start()

    @pl.when(i == n_blocks - 1)
    def _():
        wait_scatter()


def _experts(h, block_e, row_src, row_dst, row_gate, nused, w_gate, w_up, w_down, layer, out_rows):
    n_blocks = block_e.shape[0]
    wsel = lambda i, be, *_: (layer, be[i], 0, 0)
    return pl.pallas_call(
        _experts_kernel,
        grid_spec=pltpu.PrefetchScalarGridSpec(
            num_scalar_prefetch=4, grid=(n_blocks,),
            in_specs=[pl.BlockSpec(memory_space=pl.ANY),
                      pl.BlockSpec((EXPERT_BLOCK, 1), lambda i, *_: (i, 0)),
                      pl.BlockSpec((None, 1, D_MODEL, D_EXPERT), wsel),
                      pl.BlockSpec((None, 1, D_MODEL, D_EXPERT), wsel),
                      pl.BlockSpec((None, 1, D_EXPERT, D_MODEL), wsel)],
            out_specs=pl.BlockSpec(memory_space=pl.ANY),
            scratch_shapes=[pltpu.VMEM((2, EXPERT_BLOCK, D_MODEL), F32),
                            pltpu.VMEM((EXPERT_BLOCK, D_MODEL), F32),
                            pltpu.VMEM((D_MODEL, D_EXPERT), BF16),
                            pltpu.VMEM((D_MODEL, D_EXPERT), BF16),
                            pltpu.VMEM((D_EXPERT, D_MODEL), BF16),
                            pltpu.SemaphoreType.DMA((2,)),
                            pltpu.SemaphoreType.DMA((1,))]),
        out_shape=jax.ShapeDtypeStruct((out_rows, D_MODEL), F32),
        compiler_params=_params(("arbitrary",)),
        name="experts",
    )(block_e, row_src, row_dst, nused, h, row_gate, w_gate, w_up, w_down)


def _dispatch_plan(meta, counts, t):
    n = t * TOP_K
    n_blocks = -(-(n + N_EXPERTS * (EXPERT_BLOCK - 1)) // EXPERT_BLOCK)
    n_rows = n_blocks * EXPERT_BLOCK
    eid = meta[:, 0:TOP_K].astype(jnp.int32)
    rank = meta[:, TOP_K:2 * TOP_K].astype(jnp.int32)
    gate = meta[:, 2 * TOP_K:3 * TOP_K]
    cnt = counts[0, ROUTER_LANE0:ROUTER_LANE0 + N_EXPERTS].astype(jnp.int32)
    padded = (cnt + EXPERT_BLOCK - 1) // EXPERT_BLOCK * EXPERT_BLOCK
    pad_end = jnp.cumsum(padded)
    pad_start = pad_end - padded
    dest = (pad_start[eid] + rank).reshape(n)
    tok = jnp.repeat(jnp.arange(t, dtype=jnp.int32), TOP_K)
    slot = jnp.tile(jnp.arange(TOP_K, dtype=jnp.int32), t)
    row_src = jnp.zeros((n_rows,), jnp.int32).at[dest].set(tok)
    is_pad = jnp.ones((n_rows,), jnp.int32).at[dest].set(0)
    row_dst = (n - 1 + jnp.cumsum(is_pad)).at[dest].set(slot * t + tok)
    row_gate = jnp.zeros((n_rows,), F32).at[dest].set(gate.reshape(n)).reshape(n_rows, 1)
    block_e = jnp.minimum(jnp.searchsorted(pad_end, jnp.arange(n_blocks, dtype=jnp.int32) * EXPERT_BLOCK, side='right'),
                          N_EXPERTS - 1).astype(jnp.int32)
    nused = (pad_end[-1] // EXPERT_BLOCK).astype(jnp.int32).reshape(1)
    return block_e, row_src, row_dst, row_gate, nused, n_rows


def _final_kernel(x_ref, m0_ref, m1_ref, g_ref, y_ref):
    y_ref[...] = _rms(x_ref[...] + m0_ref[...] + m1_ref[...], g_ref[...])


def _final_norm(x, moe, g, *, tm):
    t = x.shape[0]
    nt = t // tm
    row = lambda i: (i, 0)
    return pl.pallas_call(
        _final_kernel, grid=(nt,),
        in_specs=[pl.BlockSpec((tm, D_MODEL), row), pl.BlockSpec((tm, D_MODEL), row),
                  pl.BlockSpec((tm, D_MODEL), lambda i: (i + nt, 0)), pl.BlockSpec((1, D_MODEL), lambda i: (0, 0))],
        out_specs=pl.BlockSpec((tm, D_MODEL), row),
        out_shape=jax.ShapeDtypeStruct((t, D_MODEL), F32),
        compiler_params=_params(("parallel",)),
        name="final_norm",
    )(x, moe, moe, g.reshape(1, D_MODEL))


def kernel(x_prompt, x_sample, cache_k, cache_v, state_conv, norm_mix, norm_ffn, norm_final, conv_w_pw1, conv_b_pw1, conv_w_dw, conv_b_dw, conv_ln_g, conv_ln_b, conv_w_pw2, conv_b_pw2, attn_w_qkv, attn_b_qkv, attn_sinks, attn_w_o, attn_b_o, moe_w_router_group, moe_b_router_group, moe_w_router_expert, moe_b_router_expert, moe_w_gate, moe_w_up, moe_w_down):
    batch, seq, _ = x_prompt.shape
    n_dec, dec_seq, _ = x_sample.shape
    depth = norm_mix.shape[0]
    tp, ts = batch * seq, n_dec * dec_seq
    t = tp + ts
    tm = _token_tile(t, tp)

    x = jnp.concatenate([x_prompt.reshape(tp, D_MODEL), x_sample.reshape(ts, D_MODEL)], axis=0)
    moe = None
    kp_list, vp_list, cp_list, ks_list, vs_list, cs_list = [], [], [], [], [], []
    for layer in range(depth):
        j = layer // 2
        if layer % 2 == 0:
            x, u = _mixer_in(x, moe, norm_mix[layer], conv_w_pw1[j], conv_b_pw1[j], is_conv=True, tm=tm)
            u_p = u[:tp].reshape(batch, seq, D_MODEL)
            ucat_s = jnp.concatenate([state_conv[j], u[tp:].reshape(n_dec, dec_seq, D_MODEL)], axis=1)
            cp_list.append(u_p[:, seq - (CONV_WIDTH - 1):])
            cs_list.append(ucat_s[:, dec_seq:])
            z_p = _conv_prompt(u, conv_w_dw[j], conv_b_dw[j], conv_ln_g[j], conv_ln_b[j], batch=batch, seq=seq)
            z_s = _conv_sample(jnp.swapaxes(ucat_s, 0, 1), conv_w_dw[j], conv_b_dw[j], conv_ln_g[j], conv_ln_b[j],
                               dec_seq=dec_seq)
            z = jnp.concatenate([z_p, jnp.swapaxes(z_s, 0, 1).reshape(ts, D_MODEL)], axis=0)
            w_out, b_out = conv_w_pw2[j], conv_b_pw2[j]
        else:
            x, q, k, v = _mixer_in(x, moe, norm_mix[layer], attn_w_qkv[j], attn_b_qkv[j], is_conv=False, tm=tm)
            k_p = k[:tp].reshape(batch, seq, N_KV_HEADS, HEAD_DIM)
            v_p = v[:tp].reshape(batch, seq, N_KV_HEADS, HEAD_DIM)
            kp_list.append(k_p[:, seq - WINDOW:])
            vp_list.append(v_p[:, seq - WINDOW:])
            kcat = jnp.concatenate([cache_k[j], k[tp:].reshape(n_dec, dec_seq, N_KV_HEADS, HEAD_DIM)], axis=1)
            vcat = jnp.concatenate([cache_v[j], v[tp:].reshape(n_dec, dec_seq, N_KV_HEADS, HEAD_DIM)], axis=1)
            ks_list.append(kcat[:, dec_seq:])
            vs_list.append(vcat[:, dec_seq:])
            o_p = _attn_prompt(q, k, v, attn_sinks[j], batch=batch, seq=seq)
            q_s = q[tp:].reshape(n_dec, dec_seq, N_KV_HEADS, GROUP, HEAD_DIM).transpose(0, 2, 3, 1, 4)
            q_s = q_s.reshape(n_dec, N_KV_HEADS, GROUP * dec_seq, HEAD_DIM)
            o_s = _attn_sample(q_s, kcat.transpose(0, 2, 1, 3), vcat.transpose(0, 2, 1, 3), attn_sinks[j],
                               dec_seq=dec_seq)
            o_s = o_s.reshape(n_dec, N_KV_HEADS, GROUP, dec_seq, HEAD_DIM).transpose(0, 3, 1, 2, 4)
            z = jnp.concatenate([o_p, o_s.reshape(ts, D_MODEL).astype(BF16)], axis=0)
            w_out, b_out = attn_w_o[j], attn_b_o[j]
        x, h, meta, counts = _proj_router(x, z, w_out, b_out, norm_ffn[layer],
                                          moe_w_router_group[layer], moe_b_router_group[layer],
                                          moe_w_router_expert[layer], moe_b_router_expert[layer])
        block_e, row_src, row_dst, row_gate, nused, out_rows = _dispatch_plan(meta, counts, t)
        moe = _experts(h, block_e, row_src, row_dst, row_gate, nused, moe_w_gate, moe_w_up, moe_w_down,
                       layer, out_rows)
    y = _final_norm(x, moe, norm_final, tm=tm)
    return (y[:tp].reshape(batch, seq, D_MODEL), y[tp:].reshape(n_dec, dec_seq, D_MODEL),
            jnp.stack(kp_list), jnp.stack(vp_list), jnp.stack(cp_list),
            jnp.stack(ks_list), jnp.stack(vs_list), jnp.stack(cs_list))
```

```python
import functools

import jax
import jax.numpy as jnp
from jax import lax
from jax.experimental import pallas as pl
from jax.experimental.pallas import tpu as pltpu

F32 = jnp.float32
BF16 = jnp.bfloat16

D_MODEL = 1024
CONV_WIDTH = 31
N_HEADS = 16
N_KV_HEADS = 4
HEAD_DIM = D_MODEL // N_HEADS
GROUP = N_HEADS // N_KV_HEADS
KV_DIM = N_KV_HEADS * HEAD_DIM
WINDOW = 128
N_GROUPS = 8
EXPERTS_PER_GROUP = 8
N_EXPERTS = N_GROUPS * EXPERTS_PER_GROUP
TOP_K = 2
D_EXPERT = D_MODEL // 2
EXPERT_BLOCK = 128
RMS_EPS = 1e-6
LN_EPS = 1e-5

LANES = 128
SUBLANES = 8
ROUTER_LANE0 = N_GROUPS
META_GATE_LANE = 2 * TOP_K
CONV_HALO = 32
CONV_ROWS = 32
WAIT_CHUNK = 128
ISSUE_UNROLL = 8
ISSUE_LAG = 2
COMBINE_TILE = 512
VMEM_LIMIT = 56 * 1024 * 1024


def _params(sem):
    return pltpu.CompilerParams(dimension_semantics=sem, vmem_limit_bytes=VMEM_LIMIT)


def _rms(x, g):
    return x * lax.rsqrt(jnp.mean(x * x, axis=-1, keepdims=True) + RMS_EPS) * g


def _token_tile(*counts, largest=512):
    for tm in (512, 256, 128):
        if tm <= largest and all(c % tm == 0 for c in counts):
            return tm
    raise ValueError("token counts must be multiples of 128")


def _row_copy(src_hbm, src_row, dst, dst_row, sem):
    return pltpu.make_async_copy(src_hbm.at[pl.ds(src_row, 1), :], dst.at[pl.ds(dst_row, 1), :], sem)


def _wait_row_copies(src_hbm, dst, sem, n_rows):
    def chunk(c, carry):
        for _ in range(WAIT_CHUNK):
            _row_copy(src_hbm, 0, dst, 0, sem).wait()
        return carry

    lax.fori_loop(0, n_rows // WAIT_CHUNK, chunk, 0)


def _issue_rows_rolling(n_tokens, start_token, wait_chunk):
    chunk_tokens = WAIT_CHUNK // TOP_K

    def chunk(c, carry):
        def body(j, inner):
            start_token(c * chunk_tokens + j)
            return inner

        lax.fori_loop(0, chunk_tokens, body, 0, unroll=ISSUE_UNROLL)

        @pl.when(c >= ISSUE_LAG)
        def _():
            wait_chunk()

        return carry

    lax.fori_loop(0, n_tokens // chunk_tokens, chunk, 0)


def _moe_combine(dest_ref, x_ref, meta_ref, ys_hbm, mbuf, sem):
    i = pl.program_id(0)
    tm = x_ref.shape[0]

    def issue(tile, slot):
        def start_token(j):
            tok = tile * tm + j
            for k in range(TOP_K):
                _row_copy(ys_hbm, dest_ref[tok * TOP_K + k], mbuf.at[slot, k], j, sem.at[slot]).start()

        _issue_rows_rolling(tm, start_token,
                            lambda: _wait_row_copies(ys_hbm, mbuf.at[slot, 0], sem.at[slot], WAIT_CHUNK))

    @pl.when(i == 0)
    def _():
        issue(0, 0)

    slot = i % 2
    _wait_row_copies(ys_hbm, mbuf.at[slot, 0], sem.at[slot], ISSUE_LAG * WAIT_CHUNK)

    @pl.when(i + 1 < pl.num_programs(0))
    def _():
        issue(i + 1, (i + 1) % 2)

    moe = meta_ref[:, META_GATE_LANE:META_GATE_LANE + 1] * mbuf[slot, 0]
    for k in range(1, TOP_K):
        moe = moe + meta_ref[:, META_GATE_LANE + k:META_GATE_LANE + k + 1] * mbuf[slot, k]
    return x_ref[...] + moe


def _combine_specs(tm):
    in_specs = [pl.BlockSpec((tm, D_MODEL), lambda i, *_: (i, 0)),
                pl.BlockSpec((tm, LANES), lambda i, *_: (i, 0)),
                pl.BlockSpec(memory_space=pl.ANY)]
    scratch = [pltpu.VMEM((2, TOP_K, tm, D_MODEL), F32), pltpu.SemaphoreType.DMA((2,))]
    return in_specs, scratch


def _mixer_proj(x, g_ref, w_ref, b_ref, outs, is_conv):
    h = _rms(x, g_ref[...]).astype(BF16)
    y = jnp.dot(h, w_ref[...], preferred_element_type=F32) + b_ref[...]
    if is_conv:
        (u_ref,) = outs
        u_ref[...] = y[:, :D_MODEL] * jax.nn.sigmoid(y[:, D_MODEL:])
    else:
        q_ref, k_ref, v_ref = outs
        q_ref[...] = (y[:, :D_MODEL] * (HEAD_DIM ** -0.5)).astype(BF16)
        k_ref[...] = y[:, D_MODEL:D_MODEL + KV_DIM]
        v_ref[...] = y[:, D_MODEL + KV_DIM:]


def _mixer_in_first_kernel(x_ref, g_ref, w_ref, b_ref, *outs, is_conv):
    _mixer_proj(x_ref[...], g_ref, w_ref, b_ref, outs, is_conv)


def _mixer_in_kernel(dest_ref, x_ref, meta_ref, ys_hbm, g_ref, w_ref, b_ref, xo_ref, *rest, is_conv):
    *outs, mbuf, sem = rest
    x = _moe_combine(dest_ref, x_ref, meta_ref, ys_hbm, mbuf, sem)
    xo_ref[...] = x
    _mixer_proj(x, g_ref, w_ref, b_ref, outs, is_conv)


def _mixer_in(x, moe, g, w, b, *, is_conv, tm):
    t = x.shape[0]
    n_out = w.shape[1]
    row = lambda i, *_: (i, 0)
    const = lambda i, *_: (0, 0)
    w_specs = [pl.BlockSpec((1, D_MODEL), const), pl.BlockSpec((D_MODEL, n_out), const),
               pl.BlockSpec((1, n_out), const)]
    w_args = [g.reshape(1, D_MODEL), w.astype(BF16), b.reshape(1, n_out)]
    if is_conv:
        out_shape = [jax.ShapeDtypeStruct((t, D_MODEL), F32)]
        out_specs = [pl.BlockSpec((tm, D_MODEL), row)]
    else:
        out_shape = [jax.ShapeDtypeStruct((t, D_MODEL), BF16), jax.ShapeDtypeStruct((t, KV_DIM), F32),
                     jax.ShapeDtypeStruct((t, KV_DIM), F32)]
        out_specs = [pl.BlockSpec((tm, D_MODEL), row), pl.BlockSpec((tm, KV_DIM), row),
                     pl.BlockSpec((tm, KV_DIM), row)]
    name = "mixer_in_conv" if is_conv else "mixer_in_attn"
    if moe is None:
        outs = pl.pallas_call(
            functools.partial(_mixer_in_first_kernel, is_conv=is_conv),
            grid=(t // tm,), in_specs=[pl.BlockSpec((tm, D_MODEL), row)] + w_specs,
            out_specs=out_specs, out_shape=out_shape,
            compiler_params=_params(("parallel",)), name=name + "_first",
        )(x, *w_args)
        return [x] + list(outs)
    dest, meta, ys = moe
    c_specs, c_scratch = _combine_specs(tm)
    return pl.pallas_call(
        functools.partial(_mixer_in_kernel, is_conv=is_conv),
        grid_spec=pltpu.PrefetchScalarGridSpec(
            num_scalar_prefetch=1, grid=(t // tm,),
            in_specs=c_specs + w_specs,
            out_specs=[pl.BlockSpec((tm, D_MODEL), row)] + out_specs,
            scratch_shapes=c_scratch),
        out_shape=[jax.ShapeDtypeStruct((t, D_MODEL), F32)] + out_shape,
        compiler_params=_params(("arbitrary",)), name=name,
    )(dest, x, meta, ys, *w_args)


def _ln_silu(c, g, b):
    mu = jnp.mean(c, axis=-1, keepdims=True)
    d = c - mu
    var = jnp.mean(d * d, axis=-1, keepdims=True)
    return jax.nn.silu(d * lax.rsqrt(var + LN_EPS) * g + b)


def _conv_prompt_kernel(u_ref, w_ref, bdw_ref, lg_ref, lb_ref, z_ref, ubuf, shifted, *, ts):
    s = pl.program_id(1)
    nbuf = ts + CONV_HALO

    @pl.when(s == 0)
    def _():
        ubuf[0:CONV_HALO, :] = jnp.zeros((CONV_HALO, D_MODEL), F32)

    ubuf[CONV_HALO:nbuf, :] = u_ref[...]
    ubuf[nbuf:nbuf + SUBLANES, :] = jnp.zeros((SUBLANES, D_MODEL), F32)

    def shift_chunk(c, carry):
        r0 = pl.multiple_of(c * CONV_ROWS, CONV_ROWS)
        win = ubuf[pl.ds(r0, CONV_ROWS + SUBLANES), :]
        for b in range(1, SUBLANES):
            rolled = pltpu.roll(win, CONV_ROWS + SUBLANES - b, axis=0)
            shifted[b - 1, pl.ds(r0, CONV_ROWS), :] = rolled[:CONV_ROWS]
        return carry

    lax.fori_loop(0, nbuf // CONV_ROWS, shift_chunk, 0)
    first = CONV_HALO - (CONV_WIDTH - 1)

    def chunk(c, carry):
        base = pl.multiple_of(c * CONV_ROWS, CONV_ROWS)
        acc = jnp.zeros((CONV_ROWS, D_MODEL), F32)
        for k in range(CONV_WIDTH):
            a, b = divmod(first + k, SUBLANES)
            rows = pl.ds(base + a * SUBLANES, CONV_ROWS)
            tap = ubuf[rows, :] if b == 0 else shifted[b - 1, rows, :]
            acc = acc + w_ref[k:k + 1, :] * tap
        z = _ln_silu(acc + bdw_ref[...], lg_ref[...], lb_ref[...])
        z_ref[pl.ds(base, CONV_ROWS), :] = z.astype(BF16)
        return carry

    lax.fori_loop(0, ts // CONV_ROWS, chunk, 0)
    ubuf[0:CONV_HALO, :] = ubuf[ts:nbuf, :]


def _conv_prompt(u, w_dw, b_dw, ln_g, ln_b, *, batch, seq):
    ts = 512 if seq % 512 == 0 else seq
    ns = seq // ts
    const = lambda b, s: (0, 0)
    return pl.pallas_call(
        functools.partial(_conv_prompt_kernel, ts=ts),
        grid=(batch, ns),
        in_specs=[pl.BlockSpec((ts, D_MODEL), lambda b, s: (b * ns + s, 0)),
                  pl.BlockSpec((CONV_WIDTH, D_MODEL), const),
                  pl.BlockSpec((1, D_MODEL), const), pl.BlockSpec((1, D_MODEL), const),
                  pl.BlockSpec((1, D_MODEL), const)],
        out_specs=pl.BlockSpec((ts, D_MODEL), lambda b, s: (b * ns + s, 0)),
        out_shape=jax.ShapeDtypeStruct((batch * seq, D_MODEL), BF16),
        scratch_shapes=[pltpu.VMEM((ts + CONV_HALO + SUBLANES, D_MODEL), F32),
                        pltpu.VMEM((SUBLANES - 1, ts + CONV_HALO, D_MODEL), F32)],
        compiler_params=_params(("arbitrary", "arbitrary")),
        name="conv_prompt",
    )(u, w_dw, b_dw.reshape(1, D_MODEL), ln_g.reshape(1, D_MODEL), ln_b.reshape(1, D_MODEL))


def _conv_sample_kernel(u_ref, w_ref, bdw_ref, lg_ref, lb_ref, z_ref, *, dec_seq):
    for t in range(dec_seq):
        acc = jnp.zeros(u_ref.shape[1:], F32)
        for k in range(CONV_WIDTH):
            acc = acc + w_ref[k:k + 1, :] * u_ref[t + k]
        z_ref[t] = _ln_silu(acc + bdw_ref[...], lg_ref[...], lb_ref[...]).astype(BF16)


def _conv_sample(ucat_t, w_dw, b_dw, ln_g, ln_b, *, dec_seq):
    rows, n, _ = ucat_t.shape
    nb = 32 if n % 32 == 0 else n
    const = lambda i: (0, 0)
    return pl.pallas_call(
        functools.partial(_conv_sample_kernel, dec_seq=dec_seq),
        grid=(n // nb,),
        in_specs=[pl.BlockSpec((rows, nb, D_MODEL), lambda i: (0, i, 0)),
                  pl.BlockSpec((CONV_WIDTH, D_MODEL), const),
                  pl.BlockSpec((1, D_MODEL), const), pl.BlockSpec((1, D_MODEL), const),
                  pl.BlockSpec((1, D_MODEL), const)],
        out_specs=pl.BlockSpec((dec_seq, nb, D_MODEL), lambda i: (0, i, 0)),
        out_shape=jax.ShapeDtypeStruct((dec_seq, n, D_MODEL), BF16),
        compiler_params=_params(("parallel",)),
        name="conv_sample",
    )(ucat_t, w_dw, b_dw.reshape(1, D_MODEL), ln_g.reshape(1, D_MODEL), ln_b.reshape(1, D_MODEL))


def _softmax_with_sink(s, allowed, sink):
    s = jnp.where(allowed, s, -jnp.inf)
    m = jnp.maximum(jnp.max(s, axis=-1, keepdims=True), sink)
    e = jnp.exp(s - m)
    return e / (jnp.sum(e, axis=-1, keepdims=True) + jnp.exp(sink - m))


def _attn_prompt_kernel(sink_ref, q_ref, kp_ref, kc_ref, vp_ref, vc_ref, o_ref):
    i = pl.program_id(1)
    blk = q_ref.shape[0]
    qi = lax.broadcasted_iota(jnp.int32, (blk, 2 * blk), 0)
    kc = lax.broadcasted_iota(jnp.int32, (blk, 2 * blk), 1)
    allowed = (kc >= qi) & (kc <= qi + WINDOW) & ((i > 0) | (kc >= blk))
    for hk in range(N_KV_HEADS):
        cols = slice(hk * HEAD_DIM, (hk + 1) * HEAD_DIM)
        kcat = jnp.concatenate([kp_ref[:, cols], kc_ref[:, cols]], axis=0).astype(BF16)
        vcat = jnp.concatenate([vp_ref[:, cols], vc_ref[:, cols]], axis=0).astype(BF16)
        for pair in range(GROUP // 2):
            outs = []
            for g in (2 * pair, 2 * pair + 1):
                h = hk * GROUP + g
                q = q_ref[:, h * HEAD_DIM:(h + 1) * HEAD_DIM]
                s = lax.dot_general(q, kcat, (((1,), (1,)), ((), ())), preferred_element_type=F32)
                p = _softmax_with_sink(s, allowed, sink_ref[h])
                outs.append(jnp.dot(p.astype(BF16), vcat, preferred_element_type=F32))
            lo = (hk * GROUP + 2 * pair) * HEAD_DIM
            o_ref[:, lo:lo + 2 * HEAD_DIM] = jnp.concatenate(outs, axis=-1).astype(BF16)


def _attn_prompt(q, k, v, sinks, *, batch, seq):
    blk = WINDOW
    nb = seq // blk
    cur = lambda b, i, *_: (b * nb + i, 0)
    prev = lambda b, i, *_: (b * nb + jnp.maximum(i - 1, 0), 0)
    return pl.pallas_call(
        _attn_prompt_kernel,
        grid_spec=pltpu.PrefetchScalarGridSpec(
            num_scalar_prefetch=1, grid=(batch, nb),
            in_specs=[pl.BlockSpec((blk, D_MODEL), cur),
                      pl.BlockSpec((blk, KV_DIM), prev), pl.BlockSpec((blk, KV_DIM), cur),
                      pl.BlockSpec((blk, KV_DIM), prev), pl.BlockSpec((blk, KV_DIM), cur)],
            out_specs=pl.BlockSpec((blk, D_MODEL), cur)),
        out_shape=jax.ShapeDtypeStruct((batch * seq, D_MODEL), BF16),
        compiler_params=_params(("parallel", "parallel")),
        name="attn_prompt",
    )(sinks, q, k, k, v, v)


def _attn_sample_kernel(q_ref, k_ref, v_ref, sink_ref, o_ref, *, dec_seq):
    ns, kvh, nq, hd = q_ref.shape
    nk = k_ref.shape[2]
    q = q_ref[...].reshape(ns * kvh, nq, hd)
    k = k_ref[...].reshape(ns * kvh, nk, hd).astype(BF16)
    v = v_ref[...].reshape(ns * kvh, nk, hd).astype(BF16)
    s = jnp.einsum('bqd,bkd->bqk', q, k, preferred_element_type=F32).reshape(ns, kvh, nq, nk)
    t = lax.broadcasted_iota(jnp.int32, (nq, nk), 0) % dec_seq
    c = lax.broadcasted_iota(jnp.int32, (nq, nk), 1)
    allowed = (c >= t) & (c <= t + WINDOW)
    p = _softmax_with_sink(s, allowed[None, None], sink_ref[...][None])
    p = p.reshape(ns * kvh, nq, nk).astype(BF16)
    o = jnp.einsum('bqk,bkd->bqd', p, v, preferred_element_type=F32)
    o_ref[...] = o.reshape(ns, kvh, nq, hd).astype(BF16)


def _attn_sample(q, kcat, vcat, sinks, *, dec_seq):
    n, kvh, nq, hd = q.shape
    nk = kcat.shape[2]
    ns = 8 if n % 8 == 0 else n
    sink_rows = jnp.repeat(sinks.reshape(N_KV_HEADS, GROUP), dec_seq, axis=1)[..., None]
    blk4 = lambda i: (i, 0, 0, 0)
    return pl.pallas_call(
        functools.partial(_attn_sample_kernel, dec_seq=dec_seq),
        grid=(n // ns,),
        in_specs=[pl.BlockSpec((ns, kvh, nq, hd), blk4), pl.BlockSpec((ns, kvh, nk, hd), blk4),
                  pl.BlockSpec((ns, kvh, nk, hd), blk4), pl.BlockSpec((kvh, nq, 1), lambda i: (0, 0, 0))],
        out_specs=pl.BlockSpec((ns, kvh, nq, hd), blk4),
        out_shape=jax.ShapeDtypeStruct((n, kvh, nq, hd), BF16),
        compiler_params=_params(("parallel",)),
        name="attn_sample",
    )(q, kcat, vcat, sink_rows)


def _proj_router_kernel(x_ref, zp_ref, zs_ref, w_ref, b_ref, g_ref, wrh_ref, wrl_ref, br_ref,
                        x1_ref, h_ref, meta_ref, cnt_ref, carry_ref, *, prompt_tiles):
    i = pl.program_id(0)
    tm = x_ref.shape[0]

    @pl.when(i == 0)
    def _():
        carry_ref[...] = jnp.zeros_like(carry_ref)

    z = jnp.where(i < prompt_tiles, zp_ref[...], zs_ref[...])
    x1 = x_ref[...] + (jnp.dot(z, w_ref[...], preferred_element_type=F32) + b_ref[...])
    x1_ref[...] = x1
    h = _rms(x1, g_ref[...])
    h_ref[...] = h
    h_hi = h.astype(BF16)
    h_lo = (h - h_hi.astype(F32)).astype(BF16)
    logits = (jnp.dot(h_hi, wrh_ref[...], preferred_element_type=F32)
              + (jnp.dot(h_hi, wrl_ref[...], preferred_element_type=F32)
                 + jnp.dot(h_lo, wrh_ref[...], preferred_element_type=F32))) + br_ref[...]

    lane = lax.broadcasted_iota(jnp.int32, (tm, LANES), 1).astype(F32)
    first_lane = lambda hit: jnp.min(jnp.where(hit, lane, float(LANES)), axis=-1, keepdims=True)
    lg = jnp.where(lane < N_GROUPS, logits, -jnp.inf)
    mg = jnp.max(lg, axis=-1, keepdims=True)
    gsel = first_lane(lg == mg)
    gprob = 1.0 / jnp.sum(jnp.exp(lg - mg), axis=-1, keepdims=True)
    lo = ROUTER_LANE0 + gsel * EXPERTS_PER_GROUP
    in_group = (lane >= lo) & (lane < lo + EXPERTS_PER_GROUP)
    le = jnp.where(in_group, logits, -jnp.inf)
    ee = jnp.exp(le - jnp.max(le, axis=-1, keepdims=True))
    pe = jnp.where(in_group, ee / jnp.sum(ee, axis=-1, keepdims=True), -1.0)
    p1 = jnp.max(pe, axis=-1, keepdims=True)
    i1 = first_lane(pe == p1)
    pe2 = jnp.where(lane == i1, -1.0, pe)
    p2 = jnp.max(pe2, axis=-1, keepdims=True)
    i2 = first_lane(pe2 == p2)
    denom = p1 + p2
    g1 = gprob * p1 / denom
    g2 = gprob * p2 / denom
    sel1 = lane == i1
    sel2 = lane == i2
    onehot = jnp.where(sel1 | sel2, 1.0, 0.0)
    r = lax.broadcasted_iota(jnp.int32, (tm, tm), 0)
    c = lax.broadcasted_iota(jnp.int32, (tm, tm), 1)
    before = jnp.where(c < r, 1.0, 0.0).astype(BF16)
    rank = jnp.dot(before, onehot.astype(BF16), preferred_element_type=F32) + carry_ref[...]
    r1 = jnp.sum(jnp.where(sel1, rank, 0.0), axis=-1, keepdims=True)
    r2 = jnp.sum(jnp.where(sel2, rank, 0.0), axis=-1, keepdims=True)
    carry = carry_ref[...] + jnp.sum(onehot, axis=0, keepdims=True)
    carry_ref[...] = carry
    cnt_ref[...] = carry
    e1 = i1 - ROUTER_LANE0
    e2 = i2 - ROUTER_LANE0
    meta = jnp.zeros((tm, LANES), F32)
    for col, val in enumerate((e1, e2, r1, r2, g1, g2)):
        meta = jnp.where(lane == col, val, meta)
    meta_ref[...] = meta


def _proj_router(x, z_p, z_s, w, b, g, w_rg, b_rg, w_re, b_re):
    t = x.shape[0]
    tp, ts = z_p.shape[0], z_s.shape[0]
    tm = _token_tile(tp, ts)
    npt = tp // tm
    pad = LANES - N_GROUPS - N_EXPERTS
    wr = jnp.concatenate([w_rg, w_re, jnp.zeros((D_MODEL, pad), F32)], axis=1)
    wr_hi = wr.astype(BF16)
    wr_lo = (wr - wr_hi.astype(F32)).astype(BF16)
    br = jnp.concatenate([b_rg, b_re, jnp.zeros((pad,), F32)]).reshape(1, LANES)
    row = lambda i: (i, 0)
    const = lambda i: (0, 0)
    return pl.pallas_call(
        functools.partial(_proj_router_kernel, prompt_tiles=npt),
        grid=(t // tm,),
        in_specs=[pl.BlockSpec((tm, D_MODEL), row),
                  pl.BlockSpec((tm, D_MODEL), lambda i: (jnp.minimum(i, npt - 1), 0)),
                  pl.BlockSpec((tm, D_MODEL), lambda i: (jnp.maximum(i - npt, 0), 0)),
                  pl.BlockSpec((D_MODEL, D_MODEL), const), pl.BlockSpec((1, D_MODEL), const),
                  pl.BlockSpec((1, D_MODEL), const), pl.BlockSpec((D_MODEL, LANES), const),
                  pl.BlockSpec((D_MODEL, LANES), const), pl.BlockSpec((1, LANES), const)],
        out_specs=[pl.BlockSpec((tm, D_MODEL), row), pl.BlockSpec((tm, D_MODEL), row),
                   pl.BlockSpec((tm, LANES), row), pl.BlockSpec((1, LANES), const)],
        out_shape=[jax.ShapeDtypeStruct((t, D_MODEL), F32), jax.ShapeDtypeStruct((t, D_MODEL), F32),
                   jax.ShapeDtypeStruct((t, LANES), F32), jax.ShapeDtypeStruct((1, LANES), F32)],
        scratch_shapes=[pltpu.VMEM((1, LANES), F32)],
        compiler_params=_params(("arbitrary",)),
        name="proj_router",
    )(x, z_p, z_s, w.astype(BF16), b.reshape(1, D_MODEL), g.reshape(1, D_MODEL), wr_hi, wr_lo, br)


def _dispatch_plan(meta, counts, t):
    n = t * TOP_K
    n_blocks = -(-(n + N_EXPERTS * (EXPERT_BLOCK - 1)) // EXPERT_BLOCK)
    eid = meta[:, 0:TOP_K].astype(jnp.int32)
    rank = meta[:, TOP_K:2 * TOP_K].astype(jnp.int32)
    cnt = counts[0, ROUTER_LANE0:ROUTER_LANE0 + N_EXPERTS].astype(jnp.int32)
    padded = (cnt + EXPERT_BLOCK - 1) // EXPERT_BLOCK * EXPERT_BLOCK
    pad_end = jnp.cumsum(padded)
    pad_start = pad_end - padded
    experts = jnp.arange(N_EXPERTS, dtype=jnp.int32)
    start_of = jnp.sum(jnp.where(eid[..., None] == experts, pad_start, 0), axis=-1)
    dest = (start_of + rank).reshape(n)
    block_row0 = jnp.arange(n_blocks, dtype=jnp.int32) * EXPERT_BLOCK
    block_e = jnp.minimum(jnp.sum((pad_end[None, :] <= block_row0[:, None]).astype(jnp.int32), axis=1),
                          N_EXPERTS - 1)
    nused = (pad_end[-1] // EXPERT_BLOCK).reshape(1)
    zstart = jnp.where(padded > cnt, pad_end - EXPERT_BLOCK, -1)
    return dest, block_e, nused, zstart, n_blocks


def _dispatch_kernel(dest_ref, zstart_ref, nused_ref, h_ref, xs_hbm, sbuf, zbuf, zsem, dsem, *, n_blocks):
    i = pl.program_id(0)
    tm = h_ref.shape[0]

    @pl.when(i == 0)
    def _():
        zbuf[...] = jnp.zeros_like(zbuf)
        nused = nused_ref[0]

        def zero_block(row0):
            return pltpu.make_async_copy(zbuf, xs_hbm.at[pl.ds(row0, EXPERT_BLOCK), :], zsem.at[0])

        def per_partial_block(fn):
            def body(e, carry):
                @pl.when(zstart_ref[e] >= 0)
                def _():
                    fn(pl.multiple_of(zstart_ref[e], EXPERT_BLOCK))
                return carry

            lax.fori_loop(0, N_EXPERTS, body, 0)

        def per_unused_block(fn):
            def body(j, carry):
                fn(pl.multiple_of(j * EXPERT_BLOCK, EXPERT_BLOCK))
                return carry

            lax.fori_loop(nused, n_blocks, body, 0)

        per_partial_block(lambda row0: zero_block(row0).start())
        per_unused_block(lambda row0: zero_block(row0).start())
        per_partial_block(lambda row0: zero_block(row0).wait())
        per_unused_block(lambda row0: zero_block(row0).wait())

    slot = i % 2
    sbuf[slot] = h_ref[...]

    @pl.when(i > 0)
    def _():
        _wait_row_copies(sbuf.at[1 - slot], xs_hbm, dsem.at[1 - slot], ISSUE_LAG * WAIT_CHUNK)

    def start_token(j):
        tok = i * tm + j
        for k in range(TOP_K):
            _row_copy(sbuf.at[slot], j, xs_hbm, dest_ref[tok * TOP_K + k], dsem.at[slot]).start()

    _issue_rows_rolling(tm, start_token,
                        lambda: _wait_row_copies(sbuf.at[slot], xs_hbm, dsem.at[slot], WAIT_CHUNK))

    @pl.when(i == pl.num_programs(0) - 1)
    def _():
        _wait_row_copies(sbuf.at[slot], xs_hbm, dsem.at[slot], ISSUE_LAG * WAIT_CHUNK)


def _dispatch(h, dest, zstart, nused, n_blocks):
    t = h.shape[0]
    tm = _token_tile(t, largest=COMBINE_TILE)
    return pl.pallas_call(
        functools.partial(_dispatch_kernel, n_blocks=n_blocks),
        grid_spec=pltpu.PrefetchScalarGridSpec(
            num_scalar_prefetch=3, grid=(t // tm,),
            in_specs=[pl.BlockSpec((tm, D_MODEL), lambda i, *_: (i, 0))],
            out_specs=pl.BlockSpec(memory_space=pl.ANY),
            scratch_shapes=[pltpu.VMEM((2, tm, D_MODEL), F32),
                            pltpu.VMEM((EXPERT_BLOCK, D_MODEL), F32),
                            pltpu.SemaphoreType.DMA((1,)), pltpu.SemaphoreType.DMA((2,))]),
        out_shape=jax.ShapeDtypeStruct((n_blocks * EXPERT_BLOCK, D_MODEL), F32),
        compiler_params=_params(("arbitrary",)),
        name="dispatch",
    )(dest, zstart, nused, h)


def _experts_kernel(be_ref, nused_ref, xs_ref, wg_ref, wu_ref, wd_ref, ys_ref, wg_bf, wu_bf, wd_bf):
    i = pl.program_id(0)
    nused = nused_ref[0]

    @pl.when(i < nused)
    def _():
        new_expert = (i == 0) | (be_ref[i] != be_ref[jnp.maximum(i - 1, 0)])

        @pl.when(new_expert)
        def _():
            wg_bf[...] = wg_ref[0].astype(BF16)
            wu_bf[...] = wu_ref[0].astype(BF16)
            wd_bf[...] = wd_ref[0].astype(BF16)

        x = xs_ref[...].astype(BF16)
        a = jnp.dot(x, wg_bf[...], preferred_element_type=F32)
        u = jnp.dot(x, wu_bf[...], preferred_element_type=F32)
        act = (jax.nn.silu(a) * u).astype(BF16)
        ys_ref[...] = jnp.dot(act, wd_bf[...], preferred_element_type=F32)

    @pl.when(i >= nused)
    def _():
        ys_ref[...] = jnp.zeros_like(ys_ref)


def _experts(xs, block_e, nused, w_gate, w_up, w_down, layer):
    n_blocks = block_e.shape[0]
    wsel = lambda i, be, nu: (layer, be[i], 0, 0)
    return pl.pallas_call(
        _experts_kernel,
        grid_spec=pltpu.PrefetchScalarGridSpec(
            num_scalar_prefetch=2, grid=(n_blocks,),
            in_specs=[pl.BlockSpec((EXPERT_BLOCK, D_MODEL), lambda i, be, nu: (jnp.minimum(i, nu[0] - 1), 0)),
                      pl.BlockSpec((None, 1, D_MODEL, D_EXPERT), wsel),
                      pl.BlockSpec((None, 1, D_MODEL, D_EXPERT), wsel),
                      pl.BlockSpec((None, 1, D_EXPERT, D_MODEL), wsel)],
            out_specs=pl.BlockSpec((EXPERT_BLOCK, D_MODEL), lambda i, be, nu: (i, 0)),
            scratch_shapes=[pltpu.VMEM((D_MODEL, D_EXPERT), BF16),
                            pltpu.VMEM((D_MODEL, D_EXPERT), BF16),
                            pltpu.VMEM((D_EXPERT, D_MODEL), BF16)]),
        out_shape=jax.ShapeDtypeStruct((n_blocks * EXPERT_BLOCK, D_MODEL), F32),
        compiler_params=_params(("arbitrary",)),
        name="experts",
    )(block_e, nused, xs, w_gate, w_up, w_down)


def _final_kernel(dest_ref, x_ref, meta_ref, ys_hbm, g_ref, y_ref, mbuf, sem):
    y_ref[...] = _rms(_moe_combine(dest_ref, x_ref, meta_ref, ys_hbm, mbuf, sem), g_ref[...])


def _final_norm(x, moe, g, *, tm):
    t = x.shape[0]
    dest, meta, ys = moe
    c_specs, c_scratch = _combine_specs(tm)
    return pl.pallas_call(
        _final_kernel,
        grid_spec=pltpu.PrefetchScalarGridSpec(
            num_scalar_prefetch=1, grid=(t // tm,),
            in_specs=c_specs + [pl.BlockSpec((1, D_MODEL), lambda i, *_: (0, 0))],
            out_specs=pl.BlockSpec((tm, D_MODEL), lambda i, *_: (i, 0)),
            scratch_shapes=c_scratch),
        out_shape=jax.ShapeDtypeStruct((t, D_MODEL), F32),
        compiler_params=_params(("arbitrary",)),
        name="final_norm",
    )(dest, x, meta, ys, g.reshape(1, D_MODEL))


def kernel(x_prompt, x_sample, cache_k, cache_v, state_conv, norm_mix, norm_ffn, norm_final, conv_w_pw1, conv_b_pw1, conv_w_dw, conv_b_dw, conv_ln_g, conv_ln_b, conv_w_pw2, conv_b_pw2, attn_w_qkv, attn_b_qkv, attn_sinks, attn_w_o, attn_b_o, moe_w_router_group, moe_b_router_group, moe_w_router_expert, moe_b_router_expert, moe_w_gate, moe_w_up, moe_w_down):
    batch, seq, _ = x_prompt.shape
    n_dec, dec_seq, _ = x_sample.shape
    depth = norm_mix.shape[0]
    tp, ts = batch * seq, n_dec * dec_seq
    t = tp + ts
    tm_first = _token_tile(t)
    tm = _token_tile(t, largest=COMBINE_TILE)

    x = jnp.concatenate([x_prompt.reshape(tp, D_MODEL), x_sample.reshape(ts, D_MODEL)], axis=0)
    moe = None
    kp_list, vp_list, cp_list, ks_list, vs_list, cs_list = [], [], [], [], [], []
    for layer in range(depth):
        j = layer // 2
        if layer % 2 == 0:
            x, u = _mixer_in(x, moe, norm_mix[layer], conv_w_pw1[j], conv_b_pw1[j], is_conv=True,
                             tm=tm_first if moe is None else tm)
            u_p = u[:tp].reshape(batch, seq, D_MODEL)
            ucat_s = jnp.concatenate([state_conv[j], u[tp:].reshape(n_dec, dec_seq, D_MODEL)], axis=1)
            cp_list.append(u_p[:, seq - (CONV_WIDTH - 1):])
            cs_list.append(ucat_s[:, dec_seq:])
            z_p = _conv_prompt(u, conv_w_dw[j], conv_b_dw[j], conv_ln_g[j], conv_ln_b[j], batch=batch, seq=seq)
            z_s = _conv_sample(jnp.swapaxes(ucat_s, 0, 1), conv_w_dw[j], conv_b_dw[j], conv_ln_g[j], conv_ln_b[j],
                               dec_seq=dec_seq)
            z_s = jnp.swapaxes(z_s, 0, 1).reshape(ts, D_MODEL)
            w_out, b_out = conv_w_pw2[j], conv_b_pw2[j]
        else:
            x, q, k, v = _mixer_in(x, moe, norm_mix[layer], attn_w_qkv[j], attn_b_qkv[j], is_conv=False, tm=tm)
            k_p = k[:tp].reshape(batch, seq, N_KV_HEADS, HEAD_DIM)
            v_p = v[:tp].reshape(batch, seq, N_KV_HEADS, HEAD_DIM)
            kp_list.append(k_p[:, seq - WINDOW:])
            vp_list.append(v_p[:, seq - WINDOW:])
            kcat = jnp.concatenate([cache_k[j], k[tp:].reshape(n_dec, dec_seq, N_KV_HEADS, HEAD_DIM)], axis=1)
            vcat = jnp.concatenate([cache_v[j], v[tp:].reshape(n_dec, dec_seq, N_KV_HEADS, HEAD_DIM)], axis=1)
            ks_list.append(kcat[:, dec_seq:])
            vs_list.append(vcat[:, dec_seq:])
            z_p = _attn_prompt(q, k, v, attn_sinks[j], batch=batch, seq=seq)
            q_s = q[tp:].reshape(n_dec, dec_seq, N_KV_HEADS, GROUP, HEAD_DIM).transpose(0, 2, 3, 1, 4)
            q_s = q_s.reshape(n_dec, N_KV_HEADS, GROUP * dec_seq, HEAD_DIM)
            o_s = _attn_sample(q_s, kcat.transpose(0, 2, 1, 3), vcat.transpose(0, 2, 1, 3), attn_sinks[j],
                               dec_seq=dec_seq)
            o_s = o_s.reshape(n_dec, N_KV_HEADS, GROUP, dec_seq, HEAD_DIM).transpose(0, 3, 1, 2, 4)
            z_s = o_s.reshape(ts, D_MODEL)
            w_out, b_out = attn_w_o[j], attn_b_o[j]
        x, h, meta, counts = _proj_router(x, z_p, z_s, w_out, b_out, norm_ffn[layer],
                                          moe_w_router_group[layer], moe_b_router_group[layer],
                                          moe_w_router_expert[layer], moe_b_router_expert[layer])
        dest, block_e, nused, zstart, n_blocks = _dispatch_plan(meta, counts, t)
        xs = _dispatch(h, dest, zstart, nused, n_blocks)
        ys = _experts(xs, block_e, nused, moe_w_gate, moe_w_up, moe_w_down, layer)
        moe = (dest, meta, ys)
    y = _final_norm(x, moe, norm_final, tm=tm)
    return (y[:tp].reshape(batch, seq, D_MODEL), y[tp:].reshape(n_dec, dec_seq, D_MODEL),
            jnp.stack(kp_list), jnp.stack(vp_list), jnp.stack(cp_list),
            jnp.stack(ks_list), jnp.stack(vs_list), jnp.stack(cs_list))
```

```python
import functools

import jax
import jax.numpy as jnp
from jax import lax
from jax.experimental import pallas as pl
from jax.experimental.pallas import tpu as pltpu

F32 = jnp.float32
BF16 = jnp.bfloat16

D_MODEL = 1024
CONV_WIDTH = 31
N_HEADS = 16
N_KV_HEADS = 4
HEAD_DIM = D_MODEL // N_HEADS
GROUP = N_HEADS // N_KV_HEADS
KV_DIM = N_KV_HEADS * HEAD_DIM
WINDOW = 128
N_GROUPS = 8
EXPERTS_PER_GROUP = 8
N_EXPERTS = N_GROUPS * EXPERTS_PER_GROUP
TOP_K = 2
D_EXPERT = D_MODEL // 2
EXPERT_BLOCK = 256
RMS_EPS = 1e-6
LN_EPS = 1e-5

LANES = 128
SUBLANES = 8
ROUTER_LANE0 = N_GROUPS
META_GATE_LANE = 2 * TOP_K
CONV_HALO = 32
CONV_ROWS = 32
WAIT_CHUNK = 128
ISSUE_UNROLL = 8
ISSUE_LAG = 2
COMBINE_TILE = 512
VMEM_LIMIT = 56 * 1024 * 1024


def _params(sem):
    return pltpu.CompilerParams(dimension_semantics=sem, vmem_limit_bytes=VMEM_LIMIT)


def _rms(x, g):
    return x * lax.rsqrt(jnp.mean(x * x, axis=-1, keepdims=True) + RMS_EPS) * g


def _token_tile(*counts, largest=512):
    for tm in (512, 256, 128):
        if tm <= largest and all(c % tm == 0 for c in counts):
            return tm
    raise ValueError("token counts must be multiples of 128")


def _row_copy(src_hbm, src_row, dst, dst_row, sem):
    return pltpu.make_async_copy(src_hbm.at[pl.ds(src_row, 1), :], dst.at[pl.ds(dst_row, 1), :], sem)


def _wait_row_copies(src_hbm, dst, sem, n_rows):
    def chunk(c, carry):
        for _ in range(WAIT_CHUNK):
            _row_copy(src_hbm, 0, dst, 0, sem).wait()
        return carry

    lax.fori_loop(0, n_rows // WAIT_CHUNK, chunk, 0)


def _issue_rows_rolling(n_tokens, start_token, wait_chunk):
    chunk_tokens = WAIT_CHUNK // TOP_K

    def chunk(c, carry):
        def body(j, inner):
            start_token(c * chunk_tokens + j)
            return inner

        lax.fori_loop(0, chunk_tokens, body, 0, unroll=ISSUE_UNROLL)

        @pl.when(c >= ISSUE_LAG)
        def _():
            wait_chunk()

        return carry

    lax.fori_loop(0, n_tokens // chunk_tokens, chunk, 0)


def _moe_combine(dest_ref, x_ref, meta_ref, ys_hbm, mbuf, sem):
    i = pl.program_id(0)
    tm = x_ref.shape[0]

    def issue(tile, slot):
        def start_token(j):
            tok = tile * tm + j
            for k in range(TOP_K):
                _row_copy(ys_hbm, dest_ref[tok * TOP_K + k], mbuf.at[slot, k], j, sem.at[slot]).start()

        _issue_rows_rolling(tm, start_token,
                            lambda: _wait_row_copies(ys_hbm, mbuf.at[slot, 0], sem.at[slot], WAIT_CHUNK))

    @pl.when(i == 0)
    def _():
        issue(0, 0)

    slot = i % 2
    _wait_row_copies(ys_hbm, mbuf.at[slot, 0], sem.at[slot], ISSUE_LAG * WAIT_CHUNK)

    @pl.when(i + 1 < pl.num_programs(0))
    def _():
        issue(i + 1, (i + 1) % 2)

    moe = meta_ref[:, META_GATE_LANE:META_GATE_LANE + 1] * mbuf[slot, 0]
    for k in range(1, TOP_K):
        moe = moe + meta_ref[:, META_GATE_LANE + k:META_GATE_LANE + k + 1] * mbuf[slot, k]
    return x_ref[...] + moe


def _combine_specs(tm):
    in_specs = [pl.BlockSpec((tm, D_MODEL), lambda i, *_: (i, 0)),
                pl.BlockSpec((tm, LANES), lambda i, *_: (i, 0)),
                pl.BlockSpec(memory_space=pl.ANY)]
    scratch = [pltpu.VMEM((2, TOP_K, tm, D_MODEL), F32), pltpu.SemaphoreType.DMA((2,))]
    return in_specs, scratch


def _mixer_proj(x, g_ref, w_ref, b_ref, outs, is_conv):
    h = _rms(x, g_ref[...]).astype(BF16)
    y = jnp.dot(h, w_ref[...], preferred_element_type=F32) + b_ref[...]
    if is_conv:
        (u_ref,) = outs
        u_ref[...] = y[:, :D_MODEL] * jax.nn.sigmoid(y[:, D_MODEL:])
    else:
        q_ref, k_ref, v_ref = outs
        q_ref[...] = (y[:, :D_MODEL] * (HEAD_DIM ** -0.5)).astype(BF16)
        k_ref[...] = y[:, D_MODEL:D_MODEL + KV_DIM]
        v_ref[...] = y[:, D_MODEL + KV_DIM:]


def _mixer_in_first_kernel(xp_ref, xs_ref, g_ref, w_ref, b_ref, xo_ref, *outs, is_conv, prompt_tiles):
    x = jnp.where(pl.program_id(0) < prompt_tiles, xp_ref[...], xs_ref[...])
    xo_ref[...] = x
    _mixer_proj(x, g_ref, w_ref, b_ref, outs, is_conv)


def _mixer_in_kernel(dest_ref, x_ref, meta_ref, ys_hbm, g_ref, w_ref, b_ref, xo_ref, *rest, is_conv):
    *outs, mbuf, sem = rest
    x = _moe_combine(dest_ref, x_ref, meta_ref, ys_hbm, mbuf, sem)
    xo_ref[...] = x
    _mixer_proj(x, g_ref, w_ref, b_ref, outs, is_conv)


def _mixer_in(x, moe, g, w, b, *, is_conv, tm):
    t = x.shape[0] if moe is not None else x[0].shape[0] + x[1].shape[0]
    n_out = w.shape[1]
    row = lambda i, *_: (i, 0)
    const = lambda i, *_: (0, 0)
    w_specs = [pl.BlockSpec((1, D_MODEL), const), pl.BlockSpec((D_MODEL, n_out), const),
               pl.BlockSpec((1, n_out), const)]
    w_args = [g.reshape(1, D_MODEL), w.astype(BF16), b.reshape(1, n_out)]
    if is_conv:
        out_shape = [jax.ShapeDtypeStruct((t, D_MODEL), F32)]
        out_specs = [pl.BlockSpec((tm, D_MODEL), row)]
    else:
        out_shape = [jax.ShapeDtypeStruct((t, D_MODEL), BF16), jax.ShapeDtypeStruct((t, KV_DIM), F32),
                     jax.ShapeDtypeStruct((t, KV_DIM), F32)]
        out_specs = [pl.BlockSpec((tm, D_MODEL), row), pl.BlockSpec((tm, KV_DIM), row),
                     pl.BlockSpec((tm, KV_DIM), row)]
    name = "mixer_in_conv" if is_conv else "mixer_in_attn"
    if moe is None:
        x_p, x_s = x
        npt = x_p.shape[0] // tm
        return pl.pallas_call(
            functools.partial(_mixer_in_first_kernel, is_conv=is_conv, prompt_tiles=npt),
            grid=(t // tm,),
            in_specs=[pl.BlockSpec((tm, D_MODEL), lambda i: (jnp.minimum(i, npt - 1), 0)),
                      pl.BlockSpec((tm, D_MODEL), lambda i: (jnp.maximum(i - npt, 0), 0))] + w_specs,
            out_specs=[pl.BlockSpec((tm, D_MODEL), row)] + out_specs,
            out_shape=[jax.ShapeDtypeStruct((t, D_MODEL), F32)] + out_shape,
            compiler_params=_params(("parallel",)), name=name + "_first",
        )(x_p, x_s, *w_args)
    dest, meta, ys = moe
    c_specs, c_scratch = _combine_specs(tm)
    return pl.pallas_call(
        functools.partial(_mixer_in_kernel, is_conv=is_conv),
        grid_spec=pltpu.PrefetchScalarGridSpec(
            num_scalar_prefetch=1, grid=(t // tm,),
            in_specs=c_specs + w_specs,
            out_specs=[pl.BlockSpec((tm, D_MODEL), row)] + out_specs,
            scratch_shapes=c_scratch),
        out_shape=[jax.ShapeDtypeStruct((t, D_MODEL), F32)] + out_shape,
        compiler_params=_params(("arbitrary",)), name=name,
    )(dest, x, meta, ys, *w_args)


def _ln_silu(c, g, b):
    mu = jnp.mean(c, axis=-1, keepdims=True)
    d = c - mu
    var = jnp.mean(d * d, axis=-1, keepdims=True)
    return jax.nn.silu(d * lax.rsqrt(var + LN_EPS) * g + b)


def _conv_prompt_kernel(u_ref, w_ref, bdw_ref, lg_ref, lb_ref, z_ref, ubuf, shifted, *, ts):
    s = pl.program_id(1)
    nbuf = ts + CONV_HALO

    @pl.when(s == 0)
    def _():
        ubuf[0:CONV_HALO, :] = jnp.zeros((CONV_HALO, D_MODEL), F32)

    ubuf[CONV_HALO:nbuf, :] = u_ref[...]
    ubuf[nbuf:nbuf + SUBLANES, :] = jnp.zeros((SUBLANES, D_MODEL), F32)

    def shift_chunk(c, carry):
        r0 = pl.multiple_of(c * CONV_ROWS, CONV_ROWS)
        win = ubuf[pl.ds(r0, CONV_ROWS + SUBLANES), :]
        for b in range(1, SUBLANES):
            rolled = pltpu.roll(win, CONV_ROWS + SUBLANES - b, axis=0)
            shifted[b - 1, pl.ds(r0, CONV_ROWS), :] = rolled[:CONV_ROWS]
        return carry

    lax.fori_loop(0, nbuf // CONV_ROWS, shift_chunk, 0)
    first = CONV_HALO - (CONV_WIDTH - 1)

    def chunk(c, carry):
        base = pl.multiple_of(c * CONV_ROWS, CONV_ROWS)
        groups = CONV_ROWS // SUBLANES
        accs = [jnp.zeros((SUBLANES, D_MODEL), F32) for _ in range(groups)]
        for k in range(CONV_WIDTH):
            a, b = divmod(first + k, SUBLANES)
            wk = w_ref[k]
            for g in range(groups):
                rows = pl.ds(base + (a + g) * SUBLANES, SUBLANES)
                tap = ubuf[rows, :] if b == 0 else shifted[b - 1, rows, :]
                accs[g] = accs[g] + wk * tap
        acc = jnp.concatenate(accs, axis=0)
        z = _ln_silu(acc + bdw_ref[...], lg_ref[...], lb_ref[...])
        z_ref[pl.ds(base, CONV_ROWS), :] = z.astype(BF16)
        return carry

    lax.fori_loop(0, ts // CONV_ROWS, chunk, 0)
    ubuf[0:CONV_HALO, :] = ubuf[ts:nbuf, :]


def _conv_prompt(u, w_dw, b_dw, ln_g, ln_b, *, batch, seq):
    ts = 512 if seq % 512 == 0 else seq
    ns = seq // ts
    const = lambda b, s: (0, 0)
    return pl.pallas_call(
        functools.partial(_conv_prompt_kernel, ts=ts),
        grid=(batch, ns),
        in_specs=[pl.BlockSpec((ts, D_MODEL), lambda b, s: (b * ns + s, 0)),
                  pl.BlockSpec((CONV_WIDTH, SUBLANES, D_MODEL), lambda b, s: (0, 0, 0)),
                  pl.BlockSpec((1, D_MODEL), const), pl.BlockSpec((1, D_MODEL), const),
                  pl.BlockSpec((1, D_MODEL), const)],
        out_specs=pl.BlockSpec((ts, D_MODEL), lambda b, s: (b * ns + s, 0)),
        out_shape=jax.ShapeDtypeStruct((batch * seq, D_MODEL), BF16),
        scratch_shapes=[pltpu.VMEM((ts + CONV_HALO + SUBLANES, D_MODEL), F32),
                        pltpu.VMEM((SUBLANES - 1, ts + CONV_HALO, D_MODEL), F32)],
        compiler_params=_params(("arbitrary", "arbitrary")),
        name="conv_prompt",
    )(u, jnp.broadcast_to(w_dw[:, None, :], (CONV_WIDTH, SUBLANES, D_MODEL)), b_dw.reshape(1, D_MODEL), ln_g.reshape(1, D_MODEL), ln_b.reshape(1, D_MODEL))


def _conv_sample_kernel(u_ref, w_ref, bdw_ref, lg_ref, lb_ref, z_ref, *, dec_seq):
    for t in range(dec_seq):
        acc = jnp.zeros(u_ref.shape[1:], F32)
        for k in range(CONV_WIDTH):
            acc = acc + w_ref[k:k + 1, :] * u_ref[t + k]
        z_ref[t] = _ln_silu(acc + bdw_ref[...], lg_ref[...], lb_ref[...]).astype(BF16)


def _conv_sample(ucat_t, w_dw, b_dw, ln_g, ln_b, *, dec_seq):
    rows, n, _ = ucat_t.shape
    nb = 32 if n % 32 == 0 else n
    const = lambda i: (0, 0)
    return pl.pallas_call(
        functools.partial(_conv_sample_kernel, dec_seq=dec_seq),
        grid=(n // nb,),
        in_specs=[pl.BlockSpec((rows, nb, D_MODEL), lambda i: (0, i, 0)),
                  pl.BlockSpec((CONV_WIDTH, D_MODEL), const),
                  pl.BlockSpec((1, D_MODEL), const), pl.BlockSpec((1, D_MODEL), const),
                  pl.BlockSpec((1, D_MODEL), const)],
        out_specs=pl.BlockSpec((dec_seq, nb, D_MODEL), lambda i: (0, i, 0)),
        out_shape=jax.ShapeDtypeStruct((dec_seq, n, D_MODEL), BF16),
        compiler_params=_params(("parallel",)),
        name="conv_sample",
    )(ucat_t, w_dw, b_dw.reshape(1, D_MODEL), ln_g.reshape(1, D_MODEL), ln_b.reshape(1, D_MODEL))


def _softmax_with_sink(s, allowed, sink):
    s = jnp.where(allowed, s, -jnp.inf)
    m = jnp.maximum(jnp.max(s, axis=-1, keepdims=True), sink)
    e = jnp.exp(s - m)
    return e / (jnp.sum(e, axis=-1, keepdims=True) + jnp.exp(sink - m))


def _attn_prompt_kernel(sink_ref, q_ref, kp_ref, kc_ref, vp_ref, vc_ref, o_ref):
    i = pl.program_id(1)
    blk = q_ref.shape[0]
    qi = lax.broadcasted_iota(jnp.int32, (blk, 2 * blk), 0)
    kc = lax.broadcasted_iota(jnp.int32, (blk, 2 * blk), 1)
    allowed = (kc >= qi) & (kc <= qi + WINDOW) & ((i > 0) | (kc >= blk))
    for hk in range(N_KV_HEADS):
        cols = slice(hk * HEAD_DIM, (hk + 1) * HEAD_DIM)
        kcat = jnp.concatenate([kp_ref[:, cols], kc_ref[:, cols]], axis=0).astype(BF16)
        vcat = jnp.concatenate([vp_ref[:, cols], vc_ref[:, cols]], axis=0).astype(BF16)
        for pair in range(GROUP // 2):
            outs = []
            for g in (2 * pair, 2 * pair + 1):
                h = hk * GROUP + g
                q = q_ref[:, h * HEAD_DIM:(h + 1) * HEAD_DIM]
                s = lax.dot_general(q, kcat, (((1,), (1,)), ((), ())), preferred_element_type=F32)
                p = _softmax_with_sink(s, allowed, sink_ref[h])
                outs.append(jnp.dot(p.astype(BF16), vcat, preferred_element_type=F32))
            lo = (hk * GROUP + 2 * pair) * HEAD_DIM
            o_ref[:, lo:lo + 2 * HEAD_DIM] = jnp.concatenate(outs, axis=-1).astype(BF16)


def _attn_prompt(q, k, v, sinks, *, batch, seq):
    blk = WINDOW
    nb = seq // blk
    cur = lambda b, i, *_: (b * nb + i, 0)
    prev = lambda b, i, *_: (b * nb + jnp.maximum(i - 1, 0), 0)
    return pl.pallas_call(
        _attn_prompt_kernel,
        grid_spec=pltpu.PrefetchScalarGridSpec(
            num_scalar_prefetch=1, grid=(batch, nb),
            in_specs=[pl.BlockSpec((blk, D_MODEL), cur),
                      pl.BlockSpec((blk, KV_DIM), prev), pl.BlockSpec((blk, KV_DIM), cur),
                      pl.BlockSpec((blk, KV_DIM), prev), pl.BlockSpec((blk, KV_DIM), cur)],
            out_specs=pl.BlockSpec((blk, D_MODEL), cur)),
        out_shape=jax.ShapeDtypeStruct((batch * seq, D_MODEL), BF16),
        compiler_params=_params(("parallel", "parallel")),
        name="attn_prompt",
    )(sinks, q, k, k, v, v)


def _attn_sample_kernel(q_ref, k_ref, v_ref, sink_ref, o_ref, *, dec_seq):
    ns, kvh, nq, hd = q_ref.shape
    nk = k_ref.shape[2]
    q = q_ref[...].reshape(ns * kvh, nq, hd)
    k = k_ref[...].reshape(ns * kvh, nk, hd).astype(BF16)
    v = v_ref[...].reshape(ns * kvh, nk, hd).astype(BF16)
    s = jnp.einsum('bqd,bkd->bqk', q, k, preferred_element_type=F32).reshape(ns, kvh, nq, nk)
    t = lax.broadcasted_iota(jnp.int32, (nq, nk), 0) % dec_seq
    c = lax.broadcasted_iota(jnp.int32, (nq, nk), 1)
    allowed = (c >= t) & (c <= t + WINDOW)
    p = _softmax_with_sink(s, allowed[None, None], sink_ref[...][None])
    p = p.reshape(ns * kvh, nq, nk).astype(BF16)
    o = jnp.einsum('bqk,bkd->bqd', p, v, preferred_element_type=F32)
    o_ref[...] = o.reshape(ns, kvh, nq, hd).astype(BF16)


def _attn_sample(q, kcat, vcat, sinks, *, dec_seq):
    n, kvh, nq, hd = q.shape
    nk = kcat.shape[2]
    ns = 8 if n % 8 == 0 else n
    sink_rows = jnp.repeat(sinks.reshape(N_KV_HEADS, GROUP), dec_seq, axis=1)[..., None]
    blk4 = lambda i: (i, 0, 0, 0)
    return pl.pallas_call(
        functools.partial(_attn_sample_kernel, dec_seq=dec_seq),
        grid=(n // ns,),
        in_specs=[pl.BlockSpec((ns, kvh, nq, hd), blk4), pl.BlockSpec((ns, kvh, nk, hd), blk4),
                  pl.BlockSpec((ns, kvh, nk, hd), blk4), pl.BlockSpec((kvh, nq, 1), lambda i: (0, 0, 0))],
        out_specs=pl.BlockSpec((ns, kvh, nq, hd), blk4),
        out_shape=jax.ShapeDtypeStruct((n, kvh, nq, hd), BF16),
        compiler_params=_params(("parallel",)),
        name="attn_sample",
    )(q, kcat, vcat, sink_rows)


def _proj_router_kernel(x_ref, zp_ref, zs_ref, w_ref, b_ref, g_ref, wrh_ref, wrl_ref, br_ref,
                        x1_ref, h_ref, meta_ref, cnt_ref, carry_ref, *, prompt_tiles):
    i = pl.program_id(0)
    tm = x_ref.shape[0]

    @pl.when(i == 0)
    def _():
        carry_ref[...] = jnp.zeros_like(carry_ref)

    z = jnp.where(i < prompt_tiles, zp_ref[...], zs_ref[...])
    x1 = x_ref[...] + (jnp.dot(z, w_ref[...], preferred_element_type=F32) + b_ref[...])
    x1_ref[...] = x1
    h = _rms(x1, g_ref[...])
    h_ref[...] = h
    h_hi = h.astype(BF16)
    h_lo = (h - h_hi.astype(F32)).astype(BF16)
    logits = (jnp.dot(h_hi, wrh_ref[...], preferred_element_type=F32)
              + (jnp.dot(h_hi, wrl_ref[...], preferred_element_type=F32)
                 + jnp.dot(h_lo, wrh_ref[...], preferred_element_type=F32))) + br_ref[...]

    lane = lax.broadcasted_iota(jnp.int32, (tm, LANES), 1).astype(F32)
    first_lane = lambda hit: jnp.min(jnp.where(hit, lane, float(LANES)), axis=-1, keepdims=True)
    lg = jnp.where(lane < N_GROUPS, logits, -jnp.inf)
    mg = jnp.max(lg, axis=-1, keepdims=True)
    gsel = first_lane(lg == mg)
    gprob = 1.0 / jnp.sum(jnp.exp(lg - mg), axis=-1, keepdims=True)
    lo = ROUTER_LANE0 + gsel * EXPERTS_PER_GROUP
    in_group = (lane >= lo) & (lane < lo + EXPERTS_PER_GROUP)
    le = jnp.where(in_group, logits, -jnp.inf)
    ee = jnp.exp(le - jnp.max(le, axis=-1, keepdims=True))
    pe = jnp.where(in_group, ee / jnp.sum(ee, axis=-1, keepdims=True), -1.0)
    p1 = jnp.max(pe, axis=-1, keepdims=True)
    i1 = first_lane(pe == p1)
    pe2 = jnp.where(lane == i1, -1.0, pe)
    p2 = jnp.max(pe2, axis=-1, keepdims=True)
    i2 = first_lane(pe2 == p2)
    denom = p1 + p2
    g1 = gprob * p1 / denom
    g2 = gprob * p2 / denom
    sel1 = lane == i1
    sel2 = lane == i2
    onehot = jnp.where(sel1 | sel2, 1.0, 0.0)
    r = lax.broadcasted_iota(jnp.int32, (tm, tm), 0)
    c = lax.broadcasted_iota(jnp.int32, (tm, tm), 1)
    before = jnp.where(c < r, 1.0, 0.0).astype(BF16)
    rank = jnp.dot(before, onehot.astype(BF16), preferred_element_type=F32) + carry_ref[...]
    r1 = jnp.sum(jnp.where(sel1, rank, 0.0), axis=-1, keepdims=True)
    r2 = jnp.sum(jnp.where(sel2, rank, 0.0), axis=-1, keepdims=True)
    carry = carry_ref[...] + jnp.sum(onehot, axis=0, keepdims=True)
    carry_ref[...] = carry
    cnt_ref[...] = carry
    e1 = i1 - ROUTER_LANE0
    e2 = i2 - ROUTER_LANE0
    meta = jnp.zeros((tm, LANES), F32)
    for col, val in enumerate((e1, e2, r1, r2, g1, g2)):
        meta = jnp.where(lane == col, val, meta)
    meta_ref[...] = meta


def _proj_router(x, z_p, z_s, w, b, g, w_rg, b_rg, w_re, b_re):
    t = x.shape[0]
    tp, ts = z_p.shape[0], z_s.shape[0]
    tm = _token_tile(tp, ts)
    npt = tp // tm
    pad = LANES - N_GROUPS - N_EXPERTS
    wr = jnp.concatenate([w_rg, w_re, jnp.zeros((D_MODEL, pad), F32)], axis=1)
    wr_hi = wr.astype(BF16)
    wr_lo = (wr - wr_hi.astype(F32)).astype(BF16)
    br = jnp.concatenate([b_rg, b_re, jnp.zeros((pad,), F32)]).reshape(1, LANES)
    row = lambda i: (i, 0)
    const = lambda i: (0, 0)
    return pl.pallas_call(
        functools.partial(_proj_router_kernel, prompt_tiles=npt),
        grid=(t // tm,),
        in_specs=[pl.BlockSpec((tm, D_MODEL), row),
                  pl.BlockSpec((tm, D_MODEL), lambda i: (jnp.minimum(i, npt - 1), 0)),
                  pl.BlockSpec((tm, D_MODEL), lambda i: (jnp.maximum(i - npt, 0), 0)),
                  pl.BlockSpec((D_MODEL, D_MODEL), const), pl.BlockSpec((1, D_MODEL), const),
                  pl.BlockSpec((1, D_MODEL), const), pl.BlockSpec((D_MODEL, LANES), const),
                  pl.BlockSpec((D_MODEL, LANES), const), pl.BlockSpec((1, LANES), const)],
        out_specs=[pl.BlockSpec((tm, D_MODEL), row), pl.BlockSpec((tm, D_MODEL), row),
                   pl.BlockSpec((tm, LANES), row), pl.BlockSpec((1, LANES), const)],
        out_shape=[jax.ShapeDtypeStruct((t, D_MODEL), F32), jax.ShapeDtypeStruct((t, D_MODEL), F32),
                   jax.ShapeDtypeStruct((t, LANES), F32), jax.ShapeDtypeStruct((1, LANES), F32)],
        scratch_shapes=[pltpu.VMEM((1, LANES), F32)],
        compiler_params=_params(("arbitrary",)),
        name="proj_router",
    )(x, z_p, z_s, w.astype(BF16), b.reshape(1, D_MODEL), g.reshape(1, D_MODEL), wr_hi, wr_lo, br)


def _dispatch_plan(meta, counts, t):
    n = t * TOP_K
    n_blocks = -(-(n + N_EXPERTS * (EXPERT_BLOCK - 1)) // EXPERT_BLOCK)
    eid = meta[:, 0:TOP_K].astype(jnp.int32)
    rank = meta[:, TOP_K:2 * TOP_K].astype(jnp.int32)
    cnt = counts[0, ROUTER_LANE0:ROUTER_LANE0 + N_EXPERTS].astype(jnp.int32)
    padded = (cnt + EXPERT_BLOCK - 1) // EXPERT_BLOCK * EXPERT_BLOCK
    pad_end = jnp.cumsum(padded)
    pad_start = pad_end - padded
    experts = jnp.arange(N_EXPERTS, dtype=jnp.int32)
    start_of = jnp.sum(jnp.where(eid[..., None] == experts, pad_start, 0), axis=-1)
    dest = (start_of + rank).reshape(n)
    block_row0 = jnp.arange(n_blocks, dtype=jnp.int32) * EXPERT_BLOCK
    block_e = jnp.minimum(jnp.sum((pad_end[None, :] <= block_row0[:, None]).astype(jnp.int32), axis=1),
                          N_EXPERTS - 1)
    nused = (pad_end[-1] // EXPERT_BLOCK).reshape(1)
    zstart = jnp.where(padded > cnt, pad_end - EXPERT_BLOCK, -1)
    return dest, block_e, nused, zstart, n_blocks


def _dispatch_kernel(dest_ref, zstart_ref, nused_ref, h_ref, xs_hbm, sbuf, zbuf, zsem, dsem, *, n_blocks):
    i = pl.program_id(0)
    tm = h_ref.shape[0]

    @pl.when(i == 0)
    def _():
        zbuf[...] = jnp.zeros_like(zbuf)
        nused = nused_ref[0]

        def zero_block(row0):
            return pltpu.make_async_copy(zbuf, xs_hbm.at[pl.ds(row0, EXPERT_BLOCK), :], zsem.at[0])

        def per_partial_block(fn):
            def body(e, carry):
                @pl.when(zstart_ref[e] >= 0)
                def _():
                    fn(pl.multiple_of(zstart_ref[e], EXPERT_BLOCK))
                return carry

            lax.fori_loop(0, N_EXPERTS, body, 0)

        def per_unused_block(fn):
            def body(j, carry):
                fn(pl.multiple_of(j * EXPERT_BLOCK, EXPERT_BLOCK))
                return carry

            lax.fori_loop(nused, n_blocks, body, 0)

        per_partial_block(lambda row0: zero_block(row0).start())
        per_unused_block(lambda row0: zero_block(row0).start())
        per_partial_block(lambda row0: zero_block(row0).wait())
        per_unused_block(lambda row0: zero_block(row0).wait())

    slot = i % 2
    sbuf[slot] = h_ref[...]

    @pl.when(i > 0)
    def _():
        _wait_row_copies(sbuf.at[1 - slot], xs_hbm, dsem.at[1 - slot], ISSUE_LAG * WAIT_CHUNK)

    def start_token(j):
        tok = i * tm + j
        for k in range(TOP_K):
            _row_copy(sbuf.at[slot], j, xs_hbm, dest_ref[tok * TOP_K + k], dsem.at[slot]).start()

    _issue_rows_rolling(tm, start_token,
                        lambda: _wait_row_copies(sbuf.at[slot], xs_hbm, dsem.at[slot], WAIT_CHUNK))

    @pl.when(i == pl.num_programs(0) - 1)
    def _():
        _wait_row_copies(sbuf.at[slot], xs_hbm, dsem.at[slot], ISSUE_LAG * WAIT_CHUNK)


def _dispatch(h, dest, zstart, nused, n_blocks):
    t = h.shape[0]
    tm = _token_tile(t, largest=COMBINE_TILE)
    return pl.pallas_call(
        functools.partial(_dispatch_kernel, n_blocks=n_blocks),
        grid_spec=pltpu.PrefetchScalarGridSpec(
            num_scalar_prefetch=3, grid=(t // tm,),
            in_specs=[pl.BlockSpec((tm, D_MODEL), lambda i, *_: (i, 0))],
            out_specs=pl.BlockSpec(memory_space=pl.ANY),
            scratch_shapes=[pltpu.VMEM((2, tm, D_MODEL), F32),
                            pltpu.VMEM((EXPERT_BLOCK, D_MODEL), F32),
                            pltpu.SemaphoreType.DMA((1,)), pltpu.SemaphoreType.DMA((2,))]),
        out_shape=jax.ShapeDtypeStruct((n_blocks * EXPERT_BLOCK, D_MODEL), F32),
        compiler_params=_params(("arbitrary",)),
        name="dispatch",
    )(dest, zstart, nused, h)


def _experts_kernel(be_ref, nused_ref, xs_ref, wg_ref, wu_ref, wd_ref, ys_ref, wg_bf, wu_bf, wd_bf):
    i = pl.program_id(0)
    nused = nused_ref[0]

    @pl.when(i < nused)
    def _():
        new_expert = (i == 0) | (be_ref[i] != be_ref[jnp.maximum(i - 1, 0)])

        @pl.when(new_expert)
        def _():
            wg_bf[...] = wg_ref[0].astype(BF16)
            wu_bf[...] = wu_ref[0].astype(BF16)
            wd_bf[...] = wd_ref[0].astype(BF16)

        x = xs_ref[...].astype(BF16)
        a = jnp.dot(x, wg_bf[...], preferred_element_type=F32)
        u = jnp.dot(x, wu_bf[...], preferred_element_type=F32)
        act = (jax.nn.silu(a) * u).astype(BF16)
        ys_ref[...] = jnp.dot(act, wd_bf[...], preferred_element_type=F32)

    @pl.when(i >= nused)
    def _():
        ys_ref[...] = jnp.zeros_like(ys_ref)


def _experts(xs, block_e, nused, w_gate, w_up, w_down, layer):
    n_blocks = block_e.shape[0]
    wsel = lambda i, be, nu: (layer, be[i], 0, 0)
    return pl.pallas_call(
        _experts_kernel,
        grid_spec=pltpu.PrefetchScalarGridSpec(
            num_scalar_prefetch=2, grid=(n_blocks,),
            in_specs=[pl.BlockSpec((EXPERT_BLOCK, D_MODEL), lambda i, be, nu: (jnp.minimum(i, nu[0] - 1), 0)),
                      pl.BlockSpec((None, 1, D_MODEL, D_EXPERT), wsel),
                      pl.BlockSpec((None, 1, D_MODEL, D_EXPERT), wsel),
                      pl.BlockSpec((None, 1, D_EXPERT, D_MODEL), wsel)],
            out_specs=pl.BlockSpec((EXPERT_BLOCK, D_MODEL), lambda i, be, nu: (i, 0)),
            scratch_shapes=[pltpu.VMEM((D_MODEL, D_EXPERT), BF16),
                            pltpu.VMEM((D_MODEL, D_EXPERT), BF16),
                            pltpu.VMEM((D_EXPERT, D_MODEL), BF16)]),
        out_shape=jax.ShapeDtypeStruct((n_blocks * EXPERT_BLOCK, D_MODEL), F32),
        compiler_params=_params(("arbitrary",)),
        name="experts",
    )(block_e, nused, xs, w_gate, w_up, w_down)


def _final_kernel(dest_ref, x_ref, meta_ref, ys_hbm, g_ref, yp_ref, ys_ref, mbuf, sem, *, prompt_tiles):
    i = pl.program_id(0)
    y = _rms(_moe_combine(dest_ref, x_ref, meta_ref, ys_hbm, mbuf, sem), g_ref[...])

    @pl.when(i < prompt_tiles)
    def _():
        yp_ref[...] = y

    @pl.when(i >= prompt_tiles)
    def _():
        ys_ref[...] = y


def _final_norm(x, moe, g, *, tp, tm):
    t = x.shape[0]
    dest, meta, ys = moe
    npt = tp // tm
    c_specs, c_scratch = _combine_specs(tm)
    return pl.pallas_call(
        functools.partial(_final_kernel, prompt_tiles=npt),
        grid_spec=pltpu.PrefetchScalarGridSpec(
            num_scalar_prefetch=1, grid=(t // tm,),
            in_specs=c_specs + [pl.BlockSpec((1, D_MODEL), lambda i, *_: (0, 0))],
            out_specs=[pl.BlockSpec((tm, D_MODEL), lambda i, *_: (jnp.minimum(i, npt - 1), 0)),
                       pl.BlockSpec((tm, D_MODEL), lambda i, *_: (jnp.maximum(i - npt, 0), 0))],
            scratch_shapes=c_scratch),
        out_shape=[jax.ShapeDtypeStruct((tp, D_MODEL), F32), jax.ShapeDtypeStruct((t - tp, D_MODEL), F32)],
        compiler_params=_params(("arbitrary",)),
        name="final_norm",
    )(dest, x, meta, ys, g.reshape(1, D_MODEL))


def kernel(x_prompt, x_sample, cache_k, cache_v, state_conv, norm_mix, norm_ffn, norm_final, conv_w_pw1, conv_b_pw1, conv_w_dw, conv_b_dw, conv_ln_g, conv_ln_b, conv_w_pw2, conv_b_pw2, attn_w_qkv, attn_b_qkv, attn_sinks, attn_w_o, attn_b_o, moe_w_router_group, moe_b_router_group, moe_w_router_expert, moe_b_router_expert, moe_w_gate, moe_w_up, moe_w_down):
    batch, seq, _ = x_prompt.shape
    n_dec, dec_seq, _ = x_sample.shape
    depth = norm_mix.shape[0]
    tp, ts = batch * seq, n_dec * dec_seq
    t = tp + ts
    tm = _token_tile(tp, ts, largest=COMBINE_TILE)

    x = (x_prompt.reshape(tp, D_MODEL), x_sample.reshape(ts, D_MODEL))
    moe = None
    kp_list, vp_list, cp_list, ks_list, vs_list, cs_list = [], [], [], [], [], []
    for layer in range(depth):
        j = layer // 2
        if layer % 2 == 0:
            x, u = _mixer_in(x, moe, norm_mix[layer], conv_w_pw1[j], conv_b_pw1[j], is_conv=True, tm=tm)
            u_p = u[:tp].reshape(batch, seq, D_MODEL)
            ucat_s = jnp.concatenate([state_conv[j], u[tp:].reshape(n_dec, dec_seq, D_MODEL)], axis=1)
            cp_list.append(u_p[:, seq - (CONV_WIDTH - 1):])
            cs_list.append(ucat_s[:, dec_seq:])
            z_p = _conv_prompt(u, conv_w_dw[j], conv_b_dw[j], conv_ln_g[j], conv_ln_b[j], batch=batch, seq=seq)
            z_s = _conv_sample(jnp.swapaxes(ucat_s, 0, 1), conv_w_dw[j], conv_b_dw[j], conv_ln_g[j], conv_ln_b[j],
                               dec_seq=dec_seq)
            z_s = jnp.swapaxes(z_s, 0, 1).reshape(ts, D_MODEL)
            w_out, b_out = conv_w_pw2[j], conv_b_pw2[j]
        else:
            x, q, k, v = _mixer_in(x, moe, norm_mix[layer], attn_w_qkv[j], attn_b_qkv[j], is_conv=False, tm=tm)
            k_p = k[:tp].reshape(batch, seq, N_KV_HEADS, HEAD_DIM)
            v_p = v[:tp].reshape(batch, seq, N_KV_HEADS, HEAD_DIM)
            kp_list.append(k_p[:, seq - WINDOW:])
            vp_list.append(v_p[:, seq - WINDOW:])
            kcat = jnp.concatenate([cache_k[j], k[tp:].reshape(n_dec, dec_seq, N_KV_HEADS, HEAD_DIM)], axis=1)
            vcat = jnp.concatenate([cache_v[j], v[tp:].reshape(n_dec, dec_seq, N_KV_HEADS, HEAD_DIM)], axis=1)
            ks_list.append(kcat[:, dec_seq:])
            vs_list.append(vcat[:, dec_seq:])
            z_p = _attn_prompt(q, k, v, attn_sinks[j], batch=batch, seq=seq)
            q_s = q[tp:].reshape(n_dec, dec_seq, N_KV_HEADS, GROUP, HEAD_DIM).transpose(0, 2, 3, 1, 4)
            q_s = q_s.reshape(n_dec, N_KV_HEADS, GROUP * dec_seq, HEAD_DIM)
            o_s = _attn_sample(q_s, kcat.transpose(0, 2, 1, 3), vcat.transpose(0, 2, 1, 3), attn_sinks[j],
                               dec_seq=dec_seq)
            o_s = o_s.reshape(n_dec, N_KV_HEADS, GROUP, dec_seq, HEAD_DIM).transpose(0, 3, 1, 2, 4)
            z_s = o_s.reshape(ts, D_MODEL)
            w_out, b_out = attn_w_o[j], attn_b_o[j]
        x, h, meta, counts = _proj_router(x, z_p, z_s, w_out, b_out, norm_ffn[layer],
                                          moe_w_router_group[layer], moe_b_router_group[layer],
                                          moe_w_router_expert[layer], moe_b_router_expert[layer])
        dest, block_e, nused, zstart, n_blocks = _dispatch_plan(meta, counts, t)
        xs = _dispatch(h, dest, zstart, nused, n_blocks)
        ys = _experts(xs, block_e, nused, moe_w_gate, moe_w_up, moe_w_down, layer)
        moe = (dest, meta, ys)
    y_p, y_s = _final_norm(x, moe, norm_final, tp=tp, tm=tm)
    return (y_p.reshape(batch, seq, D_MODEL), y_s.reshape(n_dec, dec_seq, D_MODEL),
            jnp.stack(kp_list), jnp.stack(vp_list), jnp.stack(cp_list),
            jnp.stack(ks_list), jnp.stack(vs_list), jnp.stack(cs_list))
```

```python
import functools

import jax
import jax.numpy as jnp
from jax import lax
from jax.experimental import pallas as pl
from jax.experimental.pallas import tpu as pltpu

F32 = jnp.float32
BF16 = jnp.bfloat16

D_MODEL = 1024
CONV_WIDTH = 31
N_HEADS = 16
N_KV_HEADS = 4
HEAD_DIM = D_MODEL // N_HEADS
GROUP = N_HEADS // N_KV_HEADS
KV_DIM = N_KV_HEADS * HEAD_DIM
WINDOW = 128
N_GROUPS = 8
EXPERTS_PER_GROUP = 8
N_EXPERTS = N_GROUPS * EXPERTS_PER_GROUP
TOP_K = 2
D_EXPERT = D_MODEL // 2
EXPERT_BLOCK = 256
RMS_EPS = 1e-6
LN_EPS = 1e-5

LANES = 128
SUBLANES = 8
ROW_TILES = D_MODEL // LANES
assert ROW_TILES == SUBLANES
ROUTER_LANE0 = N_GROUPS
META_GATE_LANE = 2 * TOP_K
CONV_HALO = 32
CONV_ROWS = 32
WAIT_CHUNK = 128
ISSUE_UNROLL = 8
ISSUE_LAG = 2
COMBINE_TILE = 512
VMEM_LIMIT = 56 * 1024 * 1024


def _params(sem):
    return pltpu.CompilerParams(dimension_semantics=sem, vmem_limit_bytes=VMEM_LIMIT)


def _rms(x, g):
    return x * lax.rsqrt(jnp.mean(x * x, axis=-1, keepdims=True) + RMS_EPS) * g


def _token_tile(*counts, largest=512):
    for tm in (512, 256, 128):
        if tm <= largest and all(c % tm == 0 for c in counts):
            return tm
    raise ValueError("token counts must be multiples of 128")


def _to_token_major(ref, x):
    n = x.shape[0]
    for s in range(ROW_TILES):
        ref[pl.ds(s, n, stride=ROW_TILES), :] = x[:, s * LANES:(s + 1) * LANES]


def _from_token_major(ref, n):
    return jnp.concatenate([ref[pl.ds(s, n, stride=ROW_TILES), :] for s in range(ROW_TILES)], axis=1)


def _row_slice(row):
    start = row * ROW_TILES
    return pl.ds(start if isinstance(start, int) else pl.multiple_of(start, ROW_TILES), ROW_TILES)


def _row_copy(src, src_row, dst, dst_row, sem):
    return pltpu.make_async_copy(src.at[_row_slice(src_row), :], dst.at[_row_slice(dst_row), :], sem)


def _wait_row_copies(src_hbm, dst, sem, n_rows):
    def chunk(c, carry):
        for _ in range(WAIT_CHUNK):
            _row_copy(src_hbm, 0, dst, 0, sem).wait()
        return carry

    lax.fori_loop(0, n_rows // WAIT_CHUNK, chunk, 0)


def _issue_rows_rolling(n_tokens, start_token, wait_chunk):
    chunk_tokens = WAIT_CHUNK // TOP_K

    def chunk(c, carry):
        def body(j, inner):
            start_token(c * chunk_tokens + j)
            return inner

        lax.fori_loop(0, chunk_tokens, body, 0, unroll=ISSUE_UNROLL)

        @pl.when(c >= ISSUE_LAG)
        def _():
            wait_chunk()

        return carry

    lax.fori_loop(0, n_tokens // chunk_tokens, chunk, 0)


def _moe_combine(dest_ref, x_ref, meta_ref, ys_hbm, mbuf, sem):
    i = pl.program_id(0)
    tm = x_ref.shape[0]

    def issue(tile, slot):
        def start_token(j):
            tok = tile * tm + j
            for k in range(TOP_K):
                _row_copy(ys_hbm, dest_ref[tok * TOP_K + k], mbuf.at[slot, k], j, sem.at[slot]).start()

        _issue_rows_rolling(tm, start_token,
                            lambda: _wait_row_copies(ys_hbm, mbuf.at[slot, 0], sem.at[slot], WAIT_CHUNK))

    @pl.when(i == 0)
    def _():
        issue(0, 0)

    slot = i % 2
    _wait_row_copies(ys_hbm, mbuf.at[slot, 0], sem.at[slot], ISSUE_LAG * WAIT_CHUNK)

    @pl.when(i + 1 < pl.num_programs(0))
    def _():
        issue(i + 1, (i + 1) % 2)

    moe = meta_ref[:, META_GATE_LANE:META_GATE_LANE + 1] * _from_token_major(mbuf.at[slot, 0], tm)
    for k in range(1, TOP_K):
        moe = moe + (meta_ref[:, META_GATE_LANE + k:META_GATE_LANE + k + 1]
                     * _from_token_major(mbuf.at[slot, k], tm))
    return x_ref[...] + moe


def _combine_specs(tm):
    in_specs = [pl.BlockSpec((tm, D_MODEL), lambda i, *_: (i, 0)),
                pl.BlockSpec((tm, LANES), lambda i, *_: (i, 0)),
                pl.BlockSpec(memory_space=pl.ANY)]
    scratch = [pltpu.VMEM((2, TOP_K, tm * ROW_TILES, LANES), F32), pltpu.SemaphoreType.DMA((2,))]
    return in_specs, scratch


def _mixer_proj(x, g_ref, w_ref, b_ref, outs, is_conv):
    h = _rms(x, g_ref[...]).astype(BF16)
    y = jnp.dot(h, w_ref[...], preferred_element_type=F32) + b_ref[...]
    if is_conv:
        (u_ref,) = outs
        u_ref[...] = y[:, :D_MODEL] * jax.nn.sigmoid(y[:, D_MODEL:])
    else:
        q_ref, k_ref, v_ref = outs
        q_ref[...] = (y[:, :D_MODEL] * (HEAD_DIM ** -0.5)).astype(BF16)
        k_ref[...] = y[:, D_MODEL:D_MODEL + KV_DIM]
        v_ref[...] = y[:, D_MODEL + KV_DIM:]


def _mixer_in_first_kernel(xp_ref, xs_ref, g_ref, w_ref, b_ref, xo_ref, *outs, is_conv, prompt_tiles):
    x = jnp.where(pl.program_id(0) < prompt_tiles, xp_ref[...], xs_ref[...])
    xo_ref[...] = x
    _mixer_proj(x, g_ref, w_ref, b_ref, outs, is_conv)


def _mixer_in_kernel(dest_ref, x_ref, meta_ref, ys_hbm, g_ref, w_ref, b_ref, xo_ref, *rest, is_conv):
    *outs, mbuf, sem = rest
    x = _moe_combine(dest_ref, x_ref, meta_ref, ys_hbm, mbuf, sem)
    xo_ref[...] = x
    _mixer_proj(x, g_ref, w_ref, b_ref, outs, is_conv)


def _mixer_in(x, moe, g, w, b, *, is_conv, tm):
    t = x.shape[0] if moe is not None else x[0].shape[0] + x[1].shape[0]
    n_out = w.shape[1]
    row = lambda i, *_: (i, 0)
    const = lambda i, *_: (0, 0)
    w_specs = [pl.BlockSpec((1, D_MODEL), const), pl.BlockSpec((D_MODEL, n_out), const),
               pl.BlockSpec((1, n_out), const)]
    w_args = [g.reshape(1, D_MODEL), w.astype(BF16), b.reshape(1, n_out)]
    if is_conv:
        out_shape = [jax.ShapeDtypeStruct((t, D_MODEL), F32)]
        out_specs = [pl.BlockSpec((tm, D_MODEL), row)]
    else:
        out_shape = [jax.ShapeDtypeStruct((t, D_MODEL), BF16), jax.ShapeDtypeStruct((t, KV_DIM), F32),
                     jax.ShapeDtypeStruct((t, KV_DIM), F32)]
        out_specs = [pl.BlockSpec((tm, D_MODEL), row), pl.BlockSpec((tm, KV_DIM), row),
                     pl.BlockSpec((tm, KV_DIM), row)]
    name = "mixer_in_conv" if is_conv else "mixer_in_attn"
    if moe is None:
        x_p, x_s = x
        npt = x_p.shape[0] // tm
        return pl.pallas_call(
            functools.partial(_mixer_in_first_kernel, is_conv=is_conv, prompt_tiles=npt),
            grid=(t // tm,),
            in_specs=[pl.BlockSpec((tm, D_MODEL), lambda i: (jnp.minimum(i, npt - 1), 0)),
                      pl.BlockSpec((tm, D_MODEL), lambda i: (jnp.maximum(i - npt, 0), 0))] + w_specs,
            out_specs=[pl.BlockSpec((tm, D_MODEL), row)] + out_specs,
            out_shape=[jax.ShapeDtypeStruct((t, D_MODEL), F32)] + out_shape,
            compiler_params=_params(("parallel",)), name=name + "_first",
        )(x_p, x_s, *w_args)
    dest, meta, ys = moe
    c_specs, c_scratch = _combine_specs(tm)
    return pl.pallas_call(
        functools.partial(_mixer_in_kernel, is_conv=is_conv),
        grid_spec=pltpu.PrefetchScalarGridSpec(
            num_scalar_prefetch=1, grid=(t // tm,),
            in_specs=c_specs + w_specs,
            out_specs=[pl.BlockSpec((tm, D_MODEL), row)] + out_specs,
            scratch_shapes=c_scratch),
        out_shape=[jax.ShapeDtypeStruct((t, D_MODEL), F32)] + out_shape,
        compiler_params=_params(("arbitrary",)), name=name,
    )(dest, x, meta, ys, *w_args)


def _ln_silu(c, g, b):
    mu = jnp.mean(c, axis=-1, keepdims=True)
    d = c - mu
    var = jnp.mean(d * d, axis=-1, keepdims=True)
    return jax.nn.silu(d * lax.rsqrt(var + LN_EPS) * g + b)


def _conv_prompt_kernel(u_ref, w_ref, bdw_ref, lg_ref, lb_ref, z_ref, ubuf, shifted, *, ts):
    s = pl.program_id(1)
    nbuf = ts + CONV_HALO

    @pl.when(s == 0)
    def _():
        ubuf[0:CONV_HALO, :] = jnp.zeros((CONV_HALO, D_MODEL), F32)

    ubuf[CONV_HALO:nbuf, :] = u_ref[...]
    ubuf[nbuf:nbuf + SUBLANES, :] = jnp.zeros((SUBLANES, D_MODEL), F32)

    def shift_chunk(c, carry):
        r0 = pl.multiple_of(c * CONV_ROWS, CONV_ROWS)
        win = ubuf[pl.ds(r0, CONV_ROWS + SUBLANES), :]
        for b in range(1, SUBLANES):
            rolled = pltpu.roll(win, CONV_ROWS + SUBLANES - b, axis=0)
            shifted[b - 1, pl.ds(r0, CONV_ROWS), :] = rolled[:CONV_ROWS]
        return carry

    lax.fori_loop(0, nbuf // CONV_ROWS, shift_chunk, 0)
    first = CONV_HALO - (CONV_WIDTH - 1)

    def chunk(c, carry):
        base = pl.multiple_of(c * CONV_ROWS, CONV_ROWS)
        groups = CONV_ROWS // SUBLANES
        accs = [jnp.zeros((SUBLANES, D_MODEL), F32) for _ in range(groups)]
        for k in range(CONV_WIDTH):
            a, b = divmod(first + k, SUBLANES)
            wk = w_ref[k]
            for g in range(groups):
                rows = pl.ds(base + (a + g) * SUBLANES, SUBLANES)
                tap = ubuf[rows, :] if b == 0 else shifted[b - 1, rows, :]
                accs[g] = accs[g] + wk * tap
        acc = jnp.concatenate(accs, axis=0)
        z = _ln_silu(acc + bdw_ref[...], lg_ref[...], lb_ref[...])
        z_ref[pl.ds(base, CONV_ROWS), :] = z.astype(BF16)
        return carry

    lax.fori_loop(0, ts // CONV_ROWS, chunk, 0)
    ubuf[0:CONV_HALO, :] = ubuf[ts:nbuf, :]


def _conv_prompt(u, w_dw, b_dw, ln_g, ln_b, *, batch, seq):
    ts = 512 if seq % 512 == 0 else seq
    ns = seq // ts
    const = lambda b, s: (0, 0)
    return pl.pallas_call(
        functools.partial(_conv_prompt_kernel, ts=ts),
        grid=(batch, ns),
        in_specs=[pl.BlockSpec((ts, D_MODEL), lambda b, s: (b * ns + s, 0)),
                  pl.BlockSpec((CONV_WIDTH, SUBLANES, D_MODEL), lambda b, s: (0, 0, 0)),
                  pl.BlockSpec((1, D_MODEL), const), pl.BlockSpec((1, D_MODEL), const),
                  pl.BlockSpec((1, D_MODEL), const)],
        out_specs=pl.BlockSpec((ts, D_MODEL), lambda b, s: (b * ns + s, 0)),
        out_shape=jax.ShapeDtypeStruct((batch * seq, D_MODEL), BF16),
        scratch_shapes=[pltpu.VMEM((ts + CONV_HALO + SUBLANES, D_MODEL), F32),
                        pltpu.VMEM((SUBLANES - 1, ts + CONV_HALO, D_MODEL), F32)],
        compiler_params=_params(("arbitrary", "arbitrary")),
        name="conv_prompt",
    )(u, jnp.broadcast_to(w_dw[:, None, :], (CONV_WIDTH, SUBLANES, D_MODEL)), b_dw.reshape(1, D_MODEL), ln_g.reshape(1, D_MODEL), ln_b.reshape(1, D_MODEL))


def _conv_sample_kernel(u_ref, w_ref, bdw_ref, lg_ref, lb_ref, z_ref, *, dec_seq):
    for t in range(dec_seq):
        acc = jnp.zeros(u_ref.shape[1:], F32)
        for k in range(CONV_WIDTH):
            acc = acc + w_ref[k:k + 1, :] * u_ref[t + k]
        z_ref[t] = _ln_silu(acc + bdw_ref[...], lg_ref[...], lb_ref[...]).astype(BF16)


def _conv_sample(ucat_t, w_dw, b_dw, ln_g, ln_b, *, dec_seq):
    rows, n, _ = ucat_t.shape
    nb = 32 if n % 32 == 0 else n
    const = lambda i: (0, 0)
    return pl.pallas_call(
        functools.partial(_conv_sample_kernel, dec_seq=dec_seq),
        grid=(n // nb,),
        in_specs=[pl.BlockSpec((rows, nb, D_MODEL), lambda i: (0, i, 0)),
                  pl.BlockSpec((CONV_WIDTH, D_MODEL), const),
                  pl.BlockSpec((1, D_MODEL), const), pl.BlockSpec((1, D_MODEL), const),
                  pl.BlockSpec((1, D_MODEL), const)],
        out_specs=pl.BlockSpec((dec_seq, nb, D_MODEL), lambda i: (0, i, 0)),
        out_shape=jax.ShapeDtypeStruct((dec_seq, n, D_MODEL), BF16),
        compiler_params=_params(("parallel",)),
        name="conv_sample",
    )(ucat_t, w_dw, b_dw.reshape(1, D_MODEL), ln_g.reshape(1, D_MODEL), ln_b.reshape(1, D_MODEL))


def _softmax_with_sink(s, allowed, sink):
    s = jnp.where(allowed, s, -jnp.inf)
    m = jnp.maximum(jnp.max(s, axis=-1, keepdims=True), sink)
    e = jnp.exp(s - m)
    return e / (jnp.sum(e, axis=-1, keepdims=True) + jnp.exp(sink - m))


def _attn_prompt_kernel(sink_ref, q_ref, kp_ref, kc_ref, vp_ref, vc_ref, o_ref):
    i = pl.program_id(1)
    blk = q_ref.shape[0]
    qi = lax.broadcasted_iota(jnp.int32, (blk, 2 * blk), 0)
    kc = lax.broadcasted_iota(jnp.int32, (blk, 2 * blk), 1)
    allowed = (kc >= qi) & (kc <= qi + WINDOW) & ((i > 0) | (kc >= blk))
    for hk in range(N_KV_HEADS):
        cols = slice(hk * HEAD_DIM, (hk + 1) * HEAD_DIM)
        kcat = jnp.concatenate([kp_ref[:, cols], kc_ref[:, cols]], axis=0).astype(BF16)
        vcat = jnp.concatenate([vp_ref[:, cols], vc_ref[:, cols]], axis=0).astype(BF16)
        for pair in range(GROUP // 2):
            outs = []
            for g in (2 * pair, 2 * pair + 1):
                h = hk * GROUP + g
                q = q_ref[:, h * HEAD_DIM:(h + 1) * HEAD_DIM]
                s = lax.dot_general(q, kcat, (((1,), (1,)), ((), ())), preferred_element_type=F32)
                p = _softmax_with_sink(s, allowed, sink_ref[h])
                outs.append(jnp.dot(p.astype(BF16), vcat, preferred_element_type=F32))
            lo = (hk * GROUP + 2 * pair) * HEAD_DIM
            o_ref[:, lo:lo + 2 * HEAD_DIM] = jnp.concatenate(outs, axis=-1).astype(BF16)


def _attn_prompt(q, k, v, sinks, *, batch, seq):
    blk = WINDOW
    nb = seq // blk
    cur = lambda b, i, *_: (b * nb + i, 0)
    prev = lambda b, i, *_: (b * nb + jnp.maximum(i - 1, 0), 0)
    return pl.pallas_call(
        _attn_prompt_kernel,
        grid_spec=pltpu.PrefetchScalarGridSpec(
            num_scalar_prefetch=1, grid=(batch, nb),
            in_specs=[pl.BlockSpec((blk, D_MODEL), cur),
                      pl.BlockSpec((blk, KV_DIM), prev), pl.BlockSpec((blk, KV_DIM), cur),
                      pl.BlockSpec((blk, KV_DIM), prev), pl.BlockSpec((blk, KV_DIM), cur)],
            out_specs=pl.BlockSpec((blk, D_MODEL), cur)),
        out_shape=jax.ShapeDtypeStruct((batch * seq, D_MODEL), BF16),
        compiler_params=_params(("parallel", "parallel")),
        name="attn_prompt",
    )(sinks, q, k, k, v, v)


def _attn_sample_kernel(q_ref, k_ref, v_ref, sink_ref, o_ref, *, dec_seq):
    ns, kvh, nq, hd = q_ref.shape
    nk = k_ref.shape[2]
    q = q_ref[...].reshape(ns * kvh, nq, hd)
    k = k_ref[...].reshape(ns * kvh, nk, hd).astype(BF16)
    v = v_ref[...].reshape(ns * kvh, nk, hd).astype(BF16)
    s = jnp.einsum('bqd,bkd->bqk', q, k, preferred_element_type=F32).reshape(ns, kvh, nq, nk)
    t = lax.broadcasted_iota(jnp.int32, (nq, nk), 0) % dec_seq
    c = lax.broadcasted_iota(jnp.int32, (nq, nk), 1)
    allowed = (c >= t) & (c <= t + WINDOW)
    p = _softmax_with_sink(s, allowed[None, None], sink_ref[...][None])
    p = p.reshape(ns * kvh, nq, nk).astype(BF16)
    o = jnp.einsum('bqk,bkd->bqd', p, v, preferred_element_type=F32)
    o_ref[...] = o.reshape(ns, kvh, nq, hd).astype(BF16)


def _attn_sample(q, kcat, vcat, sinks, *, dec_seq):
    n, kvh, nq, hd = q.shape
    nk = kcat.shape[2]
    ns = 8 if n % 8 == 0 else n
    sink_rows = jnp.repeat(sinks.reshape(N_KV_HEADS, GROUP), dec_seq, axis=1)[..., None]
    blk4 = lambda i: (i, 0, 0, 0)
    return pl.pallas_call(
        functools.partial(_attn_sample_kernel, dec_seq=dec_seq),
        grid=(n // ns,),
        in_specs=[pl.BlockSpec((ns, kvh, nq, hd), blk4), pl.BlockSpec((ns, kvh, nk, hd), blk4),
                  pl.BlockSpec((ns, kvh, nk, hd), blk4), pl.BlockSpec((kvh, nq, 1), lambda i: (0, 0, 0))],
        out_specs=pl.BlockSpec((ns, kvh, nq, hd), blk4),
        out_shape=jax.ShapeDtypeStruct((n, kvh, nq, hd), BF16),
        compiler_params=_params(("parallel",)),
        name="attn_sample",
    )(q, kcat, vcat, sink_rows)


def _proj_router_kernel(x_ref, zp_ref, zs_ref, w_ref, b_ref, g_ref, wrh_ref, wrl_ref, br_ref,
                        x1_ref, h_ref, meta_ref, cnt_ref, carry_ref, *, prompt_tiles):
    i = pl.program_id(0)
    tm = x_ref.shape[0]

    @pl.when(i == 0)
    def _():
        carry_ref[...] = jnp.zeros_like(carry_ref)

    z = jnp.where(i < prompt_tiles, zp_ref[...], zs_ref[...])
    x1 = x_ref[...] + (jnp.dot(z, w_ref[...], preferred_element_type=F32) + b_ref[...])
    x1_ref[...] = x1
    h = _rms(x1, g_ref[...])
    _to_token_major(h_ref, h)
    h_hi = h.astype(BF16)
    h_lo = (h - h_hi.astype(F32)).astype(BF16)
    logits = (jnp.dot(h_hi, wrh_ref[...], preferred_element_type=F32)
              + (jnp.dot(h_hi, wrl_ref[...], preferred_element_type=F32)
                 + jnp.dot(h_lo, wrh_ref[...], preferred_element_type=F32))) + br_ref[...]

    lane = lax.broadcasted_iota(jnp.int32, (tm, LANES), 1).astype(F32)
    first_lane = lambda hit: jnp.min(jnp.where(hit, lane, float(LANES)), axis=-1, keepdims=True)
    lg = jnp.where(lane < N_GROUPS, logits, -jnp.inf)
    mg = jnp.max(lg, axis=-1, keepdims=True)
    gsel = first_lane(lg == mg)
    gprob = 1.0 / jnp.sum(jnp.exp(lg - mg), axis=-1, keepdims=True)
    lo = ROUTER_LANE0 + gsel * EXPERTS_PER_GROUP
    in_group = (lane >= lo) & (lane < lo + EXPERTS_PER_GROUP)
    le = jnp.where(in_group, logits, -jnp.inf)
    ee = jnp.exp(le - jnp.max(le, axis=-1, keepdims=True))
    pe = jnp.where(in_group, ee / jnp.sum(ee, axis=-1, keepdims=True), -1.0)
    p1 = jnp.max(pe, axis=-1, keepdims=True)
    i1 = first_lane(pe == p1)
    pe2 = jnp.where(lane == i1, -1.0, pe)
    p2 = jnp.max(pe2, axis=-1, keepdims=True)
    i2 = first_lane(pe2 == p2)
    denom = p1 + p2
    g1 = gprob * p1 / denom
    g2 = gprob * p2 / denom
    sel1 = lane == i1
    sel2 = lane == i2
    onehot = jnp.where(sel1 | sel2, 1.0, 0.0)
    r = lax.broadcasted_iota(jnp.int32, (tm, tm), 0)
    c = lax.broadcasted_iota(jnp.int32, (tm, tm), 1)
    before = jnp.where(c < r, 1.0, 0.0).astype(BF16)
    rank = jnp.dot(before, onehot.astype(BF16), preferred_element_type=F32) + carry_ref[...]
    r1 = jnp.sum(jnp.where(sel1, rank, 0.0), axis=-1, keepdims=True)
    r2 = jnp.sum(jnp.where(sel2, rank, 0.0), axis=-1, keepdims=True)
    carry = carry_ref[...] + jnp.sum(onehot, axis=0, keepdims=True)
    carry_ref[...] = carry
    cnt_ref[...] = carry
    e1 = i1 - ROUTER_LANE0
    e2 = i2 - ROUTER_LANE0
    meta = jnp.zeros((tm, LANES), F32)
    for col, val in enumerate((e1, e2, r1, r2, g1, g2)):
        meta = jnp.where(lane == col, val, meta)
    meta_ref[...] = meta


def _proj_router(x, z_p, z_s, w, b, g, w_rg, b_rg, w_re, b_re):
    t = x.shape[0]
    tp, ts = z_p.shape[0], z_s.shape[0]
    tm = _token_tile(tp, ts)
    npt = tp // tm
    pad = LANES - N_GROUPS - N_EXPERTS
    wr = jnp.concatenate([w_rg, w_re, jnp.zeros((D_MODEL, pad), F32)], axis=1)
    wr_hi = wr.astype(BF16)
    wr_lo = (wr - wr_hi.astype(F32)).astype(BF16)
    br = jnp.concatenate([b_rg, b_re, jnp.zeros((pad,), F32)]).reshape(1, LANES)
    row = lambda i: (i, 0)
    const = lambda i: (0, 0)
    return pl.pallas_call(
        functools.partial(_proj_router_kernel, prompt_tiles=npt),
        grid=(t // tm,),
        in_specs=[pl.BlockSpec((tm, D_MODEL), row),
                  pl.BlockSpec((tm, D_MODEL), lambda i: (jnp.minimum(i, npt - 1), 0)),
                  pl.BlockSpec((tm, D_MODEL), lambda i: (jnp.maximum(i - npt, 0), 0)),
                  pl.BlockSpec((D_MODEL, D_MODEL), const), pl.BlockSpec((1, D_MODEL), const),
                  pl.BlockSpec((1, D_MODEL), const), pl.BlockSpec((D_MODEL, LANES), const),
                  pl.BlockSpec((D_MODEL, LANES), const), pl.BlockSpec((1, LANES), const)],
        out_specs=[pl.BlockSpec((tm, D_MODEL), row), pl.BlockSpec((tm * ROW_TILES, LANES), row),
                   pl.BlockSpec((tm, LANES), row), pl.BlockSpec((1, LANES), const)],
        out_shape=[jax.ShapeDtypeStruct((t, D_MODEL), F32), jax.ShapeDtypeStruct((t * ROW_TILES, LANES), F32),
                   jax.ShapeDtypeStruct((t, LANES), F32), jax.ShapeDtypeStruct((1, LANES), F32)],
        scratch_shapes=[pltpu.VMEM((1, LANES), F32)],
        compiler_params=_params(("arbitrary",)),
        name="proj_router",
    )(x, z_p, z_s, w.astype(BF16), b.reshape(1, D_MODEL), g.reshape(1, D_MODEL), wr_hi, wr_lo, br)


def _dispatch_plan(meta, counts, t):
    n = t * TOP_K
    n_blocks = -(-(n + N_EXPERTS * (EXPERT_BLOCK - 1)) // EXPERT_BLOCK)
    eid = meta[:, 0:TOP_K].astype(jnp.int32)
    rank = meta[:, TOP_K:2 * TOP_K].astype(jnp.int32)
    cnt = counts[0, ROUTER_LANE0:ROUTER_LANE0 + N_EXPERTS].astype(jnp.int32)
    padded = (cnt + EXPERT_BLOCK - 1) // EXPERT_BLOCK * EXPERT_BLOCK
    pad_end = jnp.cumsum(padded)
    pad_start = pad_end - padded
    experts = jnp.arange(N_EXPERTS, dtype=jnp.int32)
    start_of = jnp.sum(jnp.where(eid[..., None] == experts, pad_start, 0), axis=-1)
    dest = (start_of + rank).reshape(n)
    block_row0 = jnp.arange(n_blocks, dtype=jnp.int32) * EXPERT_BLOCK
    block_e = jnp.minimum(jnp.sum((pad_end[None, :] <= block_row0[:, None]).astype(jnp.int32), axis=1),
                          N_EXPERTS - 1)
    nused = (pad_end[-1] // EXPERT_BLOCK).reshape(1)
    zstart = jnp.where(padded > cnt, pad_end - EXPERT_BLOCK, -1)
    return dest, block_e, nused, zstart, n_blocks


def _dispatch_kernel(dest_ref, zstart_ref, nused_ref, h_ref, xs_hbm, sbuf, zbuf, zsem, dsem, *, n_blocks):
    i = pl.program_id(0)
    tm = h_ref.shape[0] // ROW_TILES

    @pl.when(i == 0)
    def _():
        zbuf[...] = jnp.zeros_like(zbuf)
        nused = nused_ref[0]

        def zero_block(row0):
            rows = pl.ds(pl.multiple_of(row0 * ROW_TILES, EXPERT_BLOCK * ROW_TILES), EXPERT_BLOCK * ROW_TILES)
            return pltpu.make_async_copy(zbuf, xs_hbm.at[rows, :], zsem.at[0])

        def per_partial_block(fn):
            def body(e, carry):
                @pl.when(zstart_ref[e] >= 0)
                def _():
                    fn(pl.multiple_of(zstart_ref[e], EXPERT_BLOCK))
                return carry

            lax.fori_loop(0, N_EXPERTS, body, 0)

        def per_unused_block(fn):
            def body(j, carry):
                fn(pl.multiple_of(j * EXPERT_BLOCK, EXPERT_BLOCK))
                return carry

            lax.fori_loop(nused, n_blocks, body, 0)

        per_partial_block(lambda row0: zero_block(row0).start())
        per_unused_block(lambda row0: zero_block(row0).start())
        per_partial_block(lambda row0: zero_block(row0).wait())
        per_unused_block(lambda row0: zero_block(row0).wait())

    slot = i % 2
    sbuf[slot] = h_ref[...]

    @pl.when(i > 0)
    def _():
        _wait_row_copies(sbuf.at[1 - slot], xs_hbm, dsem.at[1 - slot], ISSUE_LAG * WAIT_CHUNK)

    def start_token(j):
        tok = i * tm + j
        for k in range(TOP_K):
            _row_copy(sbuf.at[slot], j, xs_hbm, dest_ref[tok * TOP_K + k], dsem.at[slot]).start()

    _issue_rows_rolling(tm, start_token,
                        lambda: _wait_row_copies(sbuf.at[slot], xs_hbm, dsem.at[slot], WAIT_CHUNK))

    @pl.when(i == pl.num_programs(0) - 1)
    def _():
        _wait_row_copies(sbuf.at[slot], xs_hbm, dsem.at[slot], ISSUE_LAG * WAIT_CHUNK)


def _dispatch(h, dest, zstart, nused, n_blocks):
    t = h.shape[0] // ROW_TILES
    tm = _token_tile(t, largest=COMBINE_TILE)
    return pl.pallas_call(
        functools.partial(_dispatch_kernel, n_blocks=n_blocks),
        grid_spec=pltpu.PrefetchScalarGridSpec(
            num_scalar_prefetch=3, grid=(t // tm,),
            in_specs=[pl.BlockSpec((tm * ROW_TILES, LANES), lambda i, *_: (i, 0))],
            out_specs=pl.BlockSpec(memory_space=pl.ANY),
            scratch_shapes=[pltpu.VMEM((2, tm * ROW_TILES, LANES), F32),
                            pltpu.VMEM((EXPERT_BLOCK * ROW_TILES, LANES), F32),
                            pltpu.SemaphoreType.DMA((1,)), pltpu.SemaphoreType.DMA((2,))]),
        out_shape=jax.ShapeDtypeStruct((n_blocks * EXPERT_BLOCK * ROW_TILES, LANES), F32),
        compiler_params=_params(("arbitrary",)),
        name="dispatch",
    )(dest, zstart, nused, h)


def _experts_kernel(be_ref, nused_ref, xs_ref, wg_ref, wu_ref, wd_ref, ys_ref, wg_bf, wu_bf, wd_bf):
    i = pl.program_id(0)
    nused = nused_ref[0]

    @pl.when(i < nused)
    def _():
        new_expert = (i == 0) | (be_ref[i] != be_ref[jnp.maximum(i - 1, 0)])

        @pl.when(new_expert)
        def _():
            wg_bf[...] = wg_ref[0].astype(BF16)
            wu_bf[...] = wu_ref[0].astype(BF16)
            wd_bf[...] = wd_ref[0].astype(BF16)

        x = _from_token_major(xs_ref, EXPERT_BLOCK).astype(BF16)
        a = jnp.dot(x, wg_bf[...], preferred_element_type=F32)
        u = jnp.dot(x, wu_bf[...], preferred_element_type=F32)
        act = (jax.nn.silu(a) * u).astype(BF16)
        _to_token_major(ys_ref, jnp.dot(act, wd_bf[...], preferred_element_type=F32))

    @pl.when(i >= nused)
    def _():
        ys_ref[...] = jnp.zeros_like(ys_ref)


def _experts(xs, block_e, nused, w_gate, w_up, w_down, layer):
    n_blocks = block_e.shape[0]
    wsel = lambda i, be, nu: (layer, be[i], 0, 0)
    return pl.pallas_call(
        _experts_kernel,
        grid_spec=pltpu.PrefetchScalarGridSpec(
            num_scalar_prefetch=2, grid=(n_blocks,),
            in_specs=[pl.BlockSpec((EXPERT_BLOCK * ROW_TILES, LANES),
                                   lambda i, be, nu: (jnp.minimum(i, nu[0] - 1), 0)),
                      pl.BlockSpec((None, 1, D_MODEL, D_EXPERT), wsel),
                      pl.BlockSpec((None, 1, D_MODEL, D_EXPERT), wsel),
                      pl.BlockSpec((None, 1, D_EXPERT, D_MODEL), wsel)],
            out_specs=pl.BlockSpec((EXPERT_BLOCK * ROW_TILES, LANES), lambda i, be, nu: (i, 0)),
            scratch_shapes=[pltpu.VMEM((D_MODEL, D_EXPERT), BF16),
                            pltpu.VMEM((D_MODEL, D_EXPERT), BF16),
                            pltpu.VMEM((D_EXPERT, D_MODEL), BF16)]),
        out_shape=jax.ShapeDtypeStruct((n_blocks * EXPERT_BLOCK * ROW_TILES, LANES), F32),
        compiler_params=_params(("arbitrary",)),
        name="experts",
    )(block_e, nused, xs, w_gate, w_up, w_down)


def _final_kernel(dest_ref, x_ref, meta_ref, ys_hbm, g_ref, yp_ref, ys_ref, mbuf, sem, *, prompt_tiles):
    i = pl.program_id(0)
    y = _rms(_moe_combine(dest_ref, x_ref, meta_ref, ys_hbm, mbuf, sem), g_ref[...])

    @pl.when(i < prompt_tiles)
    def _():
        yp_ref[...] = y

    @pl.when(i >= prompt_tiles)
    def _():
        ys_ref[...] = y


def _final_norm(x, moe, g, *, tp, tm):
    t = x.shape[0]
    dest, meta, ys = moe
    npt = tp // tm
    c_specs, c_scratch = _combine_specs(tm)
    return pl.pallas_call(
        functools.partial(_final_kernel, prompt_tiles=npt),
        grid_spec=pltpu.PrefetchScalarGridSpec(
            num_scalar_prefetch=1, grid=(t // tm,),
            in_specs=c_specs + [pl.BlockSpec((1, D_MODEL), lambda i, *_: (0, 0))],
            out_specs=[pl.BlockSpec((tm, D_MODEL), lambda i, *_: (jnp.minimum(i, npt - 1), 0)),
                       pl.BlockSpec((tm, D_MODEL), lambda i, *_: (jnp.maximum(i - npt, 0), 0))],
            scratch_shapes=c_scratch),
        out_shape=[jax.ShapeDtypeStruct((tp, D_MODEL), F32), jax.ShapeDtypeStruct((t - tp, D_MODEL), F32)],
        compiler_params=_params(("arbitrary",)),
        name="final_norm",
    )(dest, x, meta, ys, g.reshape(1, D_MODEL))


def kernel(x_prompt, x_sample, cache_k, cache_v, state_conv, norm_mix, norm_ffn, norm_final, conv_w_pw1, conv_b_pw1, conv_w_dw, conv_b_dw, conv_ln_g, conv_ln_b, conv_w_pw2, conv_b_pw2, attn_w_qkv, attn_b_qkv, attn_sinks, attn_w_o, attn_b_o, moe_w_router_group, moe_b_router_group, moe_w_router_expert, moe_b_router_expert, moe_w_gate, moe_w_up, moe_w_down):
    batch, seq, _ = x_prompt.shape
    n_dec, dec_seq, _ = x_sample.shape
    depth = norm_mix.shape[0]
    tp, ts = batch * seq, n_dec * dec_seq
    t = tp + ts
    tm = _token_tile(tp, ts, largest=COMBINE_TILE)

    x = (x_prompt.reshape(tp, D_MODEL), x_sample.reshape(ts, D_MODEL))
    moe = None
    kp_list, vp_list, cp_list, ks_list, vs_list, cs_list = [], [], [], [], [], []
    for layer in range(depth):
        j = layer // 2
        if layer % 2 == 0:
            x, u = _mixer_in(x, moe, norm_mix[layer], conv_w_pw1[j], conv_b_pw1[j], is_conv=True, tm=tm)
            u_p = u[:tp].reshape(batch, seq, D_MODEL)
            ucat_s = jnp.concatenate([state_conv[j], u[tp:].reshape(n_dec, dec_seq, D_MODEL)], axis=1)
            cp_list.append(u_p[:, seq - (CONV_WIDTH - 1):])
            cs_list.append(ucat_s[:, dec_seq:])
            z_p = _conv_prompt(u, conv_w_dw[j], conv_b_dw[j], conv_ln_g[j], conv_ln_b[j], batch=batch, seq=seq)
            z_s = _conv_sample(jnp.swapaxes(ucat_s, 0, 1), conv_w_dw[j], conv_b_dw[j], conv_ln_g[j], conv_ln_b[j],
                               dec_seq=dec_seq)
            z_s = jnp.swapaxes(z_s, 0, 1).reshape(ts, D_MODEL)
            w_out, b_out = conv_w_pw2[j], conv_b_pw2[j]
        else:
            x, q, k, v = _mixer_in(x, moe, norm_mix[layer], attn_w_qkv[j], attn_b_qkv[j], is_conv=False, tm=tm)
            k_p = k[:tp].reshape(batch, seq, N_KV_HEADS, HEAD_DIM)
            v_p = v[:tp].reshape(batch, seq, N_KV_HEADS, HEAD_DIM)
            kp_list.append(k_p[:, seq - WINDOW:])
            vp_list.append(v_p[:, seq - WINDOW:])
            kcat = jnp.concatenate([cache_k[j], k[tp:].reshape(n_dec, dec_seq, N_KV_HEADS, HEAD_DIM)], axis=1)
            vcat = jnp.concatenate([cache_v[j], v[tp:].reshape(n_dec, dec_seq, N_KV_HEADS, HEAD_DIM)], axis=1)
            ks_list.append(kcat[:, dec_seq:])
            vs_list.append(vcat[:, dec_seq:])
            z_p = _attn_prompt(q, k, v, attn_sinks[j], batch=batch, seq=seq)
            q_s = q[tp:].reshape(n_dec, dec_seq, N_KV_HEADS, GROUP, HEAD_DIM).transpose(0, 2, 3, 1, 4)
            q_s = q_s.reshape(n_dec, N_KV_HEADS, GROUP * dec_seq, HEAD_DIM)
            o_s = _attn_sample(q_s, kcat.transpose(0, 2, 1, 3), vcat.transpose(0, 2, 1, 3), attn_sinks[j],
                               dec_seq=dec_seq)
            o_s = o_s.reshape(n_dec, N_KV_HEADS, GROUP, dec_seq, HEAD_DIM).transpose(0, 3, 1, 2, 4)
            z_s = o_s.reshape(ts, D_MODEL)
            w_out, b_out = attn_w_o[j], attn_b_o[j]
        x, h, meta, counts = _proj_router(x, z_p, z_s, w_out, b_out, norm_ffn[layer],
                                          moe_w_router_group[layer], moe_b_router_group[layer],
                                          moe_w_router_expert[layer], moe_b_router_expert[layer])
        dest, block_e, nused, zstart, n_blocks = _dispatch_plan(meta, counts, t)
        xs = _dispatch(h, dest, zstart, nused, n_blocks)
        ys = _experts(xs, block_e, nused, moe_w_gate, moe_w_up, moe_w_down, layer)
        moe = (dest, meta, ys)
    y_p, y_s = _final_norm(x, moe, norm_final, tp=tp, tm=tm)
    return (y_p.reshape(batch, seq, D_MODEL), y_s.reshape(n_dec, dec_seq, D_MODEL),
            jnp.stack(kp_list), jnp.stack(vp_list), jnp.stack(cp_list),
            jnp.stack(ks_list), jnp.stack(vs_list), jnp.stack(cs_list))
```

```python
import functools

import jax
import jax.numpy as jnp
from jax import lax
from jax.experimental import pallas as pl
from jax.experimental.pallas import tpu as pltpu

F32 = jnp.float32
BF16 = jnp.bfloat16

D_MODEL = 1024
CONV_WIDTH = 31
N_HEADS = 16
N_KV_HEADS = 4
HEAD_DIM = D_MODEL // N_HEADS
GROUP = N_HEADS // N_KV_HEADS
KV_DIM = N_KV_HEADS * HEAD_DIM
WINDOW = 128
N_GROUPS = 8
EXPERTS_PER_GROUP = 8
N_EXPERTS = N_GROUPS * EXPERTS_PER_GROUP
TOP_K = 2
D_EXPERT = D_MODEL // 2
EXPERT_BLOCK = 256
RMS_EPS = 1e-6
LN_EPS = 1e-5

LANES = 128
SUBLANES = 8
ROW_TILES = D_MODEL // LANES
assert ROW_TILES == SUBLANES
ROUTER_LANE0 = N_GROUPS
META_GATE_LANE = 2 * TOP_K
CONV_HALO = 32
CONV_ROWS = 32
WAIT_CHUNK = 128
ISSUE_UNROLL = 8
DISPATCH_LAG = 2
COMBINE_LAG = 5
COMBINE_TILE = 512
VMEM_LIMIT = 56 * 1024 * 1024


def _params(sem):
    return pltpu.CompilerParams(dimension_semantics=sem, vmem_limit_bytes=VMEM_LIMIT)


def _rms(x, g):
    return x * lax.rsqrt(jnp.mean(x * x, axis=-1, keepdims=True) + RMS_EPS) * g


def _token_tile(*counts, largest=512):
    for tm in (512, 256, 128):
        if tm <= largest and all(c % tm == 0 for c in counts):
            return tm
    raise ValueError("token counts must be multiples of 128")


def _to_token_major(ref, x):
    n = x.shape[0]
    for s in range(ROW_TILES):
        ref[pl.ds(s, n, stride=ROW_TILES), :] = x[:, s * LANES:(s + 1) * LANES]


def _from_token_major(ref, n):
    return jnp.concatenate([ref[pl.ds(s, n, stride=ROW_TILES), :] for s in range(ROW_TILES)], axis=1)


def _row_slice(row):
    start = row * ROW_TILES
    return pl.ds(start if isinstance(start, int) else pl.multiple_of(start, ROW_TILES), ROW_TILES)


def _row_copy(src, src_row, dst, dst_row, sem):
    return pltpu.make_async_copy(src.at[_row_slice(src_row), :], dst.at[_row_slice(dst_row), :], sem)


def _wait_row_copies(src_hbm, dst, sem, n_rows):
    def chunk(c, carry):
        for _ in range(WAIT_CHUNK):
            _row_copy(src_hbm, 0, dst, 0, sem).wait()
        return carry

    lax.fori_loop(0, n_rows // WAIT_CHUNK, chunk, 0)


def _issue_rows_rolling(n_tokens, start_token, wait_chunk, lag):
    chunk_tokens = WAIT_CHUNK // TOP_K
    n_chunks = n_tokens // chunk_tokens

    def chunk(c, carry):
        def body(j, inner):
            start_token(c * chunk_tokens + j)
            return inner

        lax.fori_loop(0, chunk_tokens, body, 0, unroll=ISSUE_UNROLL)

        @pl.when(c >= lag)
        def _():
            wait_chunk()

        return carry

    lax.fori_loop(0, n_chunks, chunk, 0)
    return min(lag, n_chunks)


def _moe_combine(dest_ref, x_ref, meta_ref, ys_hbm, mbuf, sem):
    i = pl.program_id(0)
    tm = x_ref.shape[0]

    def issue(tile, slot):
        def start_token(j):
            tok = tile * tm + j
            for k in range(TOP_K):
                _row_copy(ys_hbm, dest_ref[tok * TOP_K + k], mbuf.at[slot, k], j, sem.at[slot]).start()

        return _issue_rows_rolling(
            tm, start_token, lambda: _wait_row_copies(ys_hbm, mbuf.at[slot, 0], sem.at[slot], WAIT_CHUNK),
            COMBINE_LAG)

    in_flight = min(COMBINE_LAG, tm * TOP_K // WAIT_CHUNK)

    @pl.when(i == 0)
    def _():
        issue(0, 0)

    slot = i % 2
    _wait_row_copies(ys_hbm, mbuf.at[slot, 0], sem.at[slot], in_flight * WAIT_CHUNK)

    @pl.when(i + 1 < pl.num_programs(0))
    def _():
        issue(i + 1, (i + 1) % 2)

    moe = meta_ref[:, META_GATE_LANE:META_GATE_LANE + 1] * _from_token_major(mbuf.at[slot, 0], tm)
    for k in range(1, TOP_K):
        moe = moe + (meta_ref[:, META_GATE_LANE + k:META_GATE_LANE + k + 1]
                     * _from_token_major(mbuf.at[slot, k], tm))
    return x_ref[...] + moe


def _combine_specs(tm):
    in_specs = [pl.BlockSpec((tm, D_MODEL), lambda i, *_: (i, 0)),
                pl.BlockSpec((tm, LANES), lambda i, *_: (i, 0)),
                pl.BlockSpec(memory_space=pl.ANY)]
    scratch = [pltpu.VMEM((2, TOP_K, tm * ROW_TILES, LANES), F32), pltpu.SemaphoreType.DMA((2,))]
    return in_specs, scratch


def _mixer_proj(x, g_ref, w_ref, b_ref, outs, is_conv):
    h = _rms(x, g_ref[...]).astype(BF16)
    y = jnp.dot(h, w_ref[...], preferred_element_type=F32) + b_ref[...]
    if is_conv:
        (u_ref,) = outs
        u_ref[...] = y[:, :D_MODEL] * jax.nn.sigmoid(y[:, D_MODEL:])
    else:
        q_ref, k_ref, v_ref = outs
        q_ref[...] = (y[:, :D_MODEL] * (HEAD_DIM ** -0.5)).astype(BF16)
        k_ref[...] = y[:, D_MODEL:D_MODEL + KV_DIM]
        v_ref[...] = y[:, D_MODEL + KV_DIM:]


def _mixer_in_first_kernel(xp_ref, xs_ref, g_ref, w_ref, b_ref, xo_ref, *outs, is_conv, prompt_tiles):
    x = jnp.where(pl.program_id(0) < prompt_tiles, xp_ref[...], xs_ref[...])
    xo_ref[...] = x
    _mixer_proj(x, g_ref, w_ref, b_ref, outs, is_conv)


def _mixer_in_kernel(dest_ref, x_ref, meta_ref, ys_hbm, g_ref, w_ref, b_ref, xo_ref, *rest, is_conv):
    *outs, mbuf, sem = rest
    x = _moe_combine(dest_ref, x_ref, meta_ref, ys_hbm, mbuf, sem)
    xo_ref[...] = x
    _mixer_proj(x, g_ref, w_ref, b_ref, outs, is_conv)


def _mixer_in(x, moe, g, w, b, *, is_conv, tm):
    t = x.shape[0] if moe is not None else x[0].shape[0] + x[1].shape[0]
    n_out = w.shape[1]
    row = lambda i, *_: (i, 0)
    const = lambda i, *_: (0, 0)
    w_specs = [pl.BlockSpec((1, D_MODEL), const), pl.BlockSpec((D_MODEL, n_out), const),
               pl.BlockSpec((1, n_out), const)]
    w_args = [g.reshape(1, D_MODEL), w.astype(BF16), b.reshape(1, n_out)]
    if is_conv:
        out_shape = [jax.ShapeDtypeStruct((t, D_MODEL), F32)]
        out_specs = [pl.BlockSpec((tm, D_MODEL), row)]
    else:
        out_shape = [jax.ShapeDtypeStruct((t, D_MODEL), BF16), jax.ShapeDtypeStruct((t, KV_DIM), F32),
                     jax.ShapeDtypeStruct((t, KV_DIM), F32)]
        out_specs = [pl.BlockSpec((tm, D_MODEL), row), pl.BlockSpec((tm, KV_DIM), row),
                     pl.BlockSpec((tm, KV_DIM), row)]
    name = "mixer_in_conv" if is_conv else "mixer_in_attn"
    if moe is None:
        x_p, x_s = x
        npt = x_p.shape[0] // tm
        return pl.pallas_call(
            functools.partial(_mixer_in_first_kernel, is_conv=is_conv, prompt_tiles=npt),
            grid=(t // tm,),
            in_specs=[pl.BlockSpec((tm, D_MODEL), lambda i: (jnp.minimum(i, npt - 1), 0)),
                      pl.BlockSpec((tm, D_MODEL), lambda i: (jnp.maximum(i - npt, 0), 0))] + w_specs,
            out_specs=[pl.BlockSpec((tm, D_MODEL), row)] + out_specs,
            out_shape=[jax.ShapeDtypeStruct((t, D_MODEL), F32)] + out_shape,
            compiler_params=_params(("parallel",)), name=name + "_first",
        )(x_p, x_s, *w_args)
    dest, meta, ys = moe
    c_specs, c_scratch = _combine_specs(tm)
    return pl.pallas_call(
        functools.partial(_mixer_in_kernel, is_conv=is_conv),
        grid_spec=pltpu.PrefetchScalarGridSpec(
            num_scalar_prefetch=1, grid=(t // tm,),
            in_specs=c_specs + w_specs,
            out_specs=[pl.BlockSpec((tm, D_MODEL), row)] + out_specs,
            scratch_shapes=c_scratch),
        out_shape=[jax.ShapeDtypeStruct((t, D_MODEL), F32)] + out_shape,
        compiler_params=_params(("arbitrary",)), name=name,
    )(dest, x, meta, ys, *w_args)


def _ln_silu(c, g, b):
    mu = jnp.mean(c, axis=-1, keepdims=True)
    d = c - mu
    var = jnp.mean(d * d, axis=-1, keepdims=True)
    return jax.nn.silu(d * lax.rsqrt(var + LN_EPS) * g + b)


def _conv_prompt_kernel(u_ref, w_ref, bdw_ref, lg_ref, lb_ref, z_ref, ubuf, shifted, *, ts):
    s = pl.program_id(1)
    nbuf = ts + CONV_HALO

    @pl.when(s == 0)
    def _():
        ubuf[0:CONV_HALO, :] = jnp.zeros((CONV_HALO, D_MODEL), F32)

    ubuf[CONV_HALO:nbuf, :] = u_ref[...]
    ubuf[nbuf:nbuf + SUBLANES, :] = jnp.zeros((SUBLANES, D_MODEL), F32)

    def shift_chunk(c, carry):
        r0 = pl.multiple_of(c * CONV_ROWS, CONV_ROWS)
        win = ubuf[pl.ds(r0, CONV_ROWS + SUBLANES), :]
        for b in range(1, SUBLANES):
            rolled = pltpu.roll(win, CONV_ROWS + SUBLANES - b, axis=0)
            shifted[b - 1, pl.ds(r0, CONV_ROWS), :] = rolled[:CONV_ROWS]
        return carry

    lax.fori_loop(0, nbuf // CONV_ROWS, shift_chunk, 0)
    first = CONV_HALO - (CONV_WIDTH - 1)

    def chunk(c, carry):
        base = pl.multiple_of(c * CONV_ROWS, CONV_ROWS)
        groups = CONV_ROWS // SUBLANES
        accs = [jnp.zeros((SUBLANES, D_MODEL), F32) for _ in range(groups)]
        for k in range(CONV_WIDTH):
            a, b = divmod(first + k, SUBLANES)
            wk = w_ref[k]
            for g in range(groups):
                rows = pl.ds(base + (a + g) * SUBLANES, SUBLANES)
                tap = ubuf[rows, :] if b == 0 else shifted[b - 1, rows, :]
                accs[g] = accs[g] + wk * tap
        acc = jnp.concatenate(accs, axis=0)
        z = _ln_silu(acc + bdw_ref[...], lg_ref[...], lb_ref[...])
        z_ref[pl.ds(base, CONV_ROWS), :] = z.astype(BF16)
        return carry

    lax.fori_loop(0, ts // CONV_ROWS, chunk, 0)
    ubuf[0:CONV_HALO, :] = ubuf[ts:nbuf, :]


def _conv_prompt(u, w_dw, b_dw, ln_g, ln_b, *, batch, seq):
    ts = 512 if seq % 512 == 0 else seq
    ns = seq // ts
    const = lambda b, s: (0, 0)
    return pl.pallas_call(
        functools.partial(_conv_prompt_kernel, ts=ts),
        grid=(batch, ns),
        in_specs=[pl.BlockSpec((ts, D_MODEL), lambda b, s: (b * ns + s, 0)),
                  pl.BlockSpec((CONV_WIDTH, SUBLANES, D_MODEL), lambda b, s: (0, 0, 0)),
                  pl.BlockSpec((1, D_MODEL), const), pl.BlockSpec((1, D_MODEL), const),
                  pl.BlockSpec((1, D_MODEL), const)],
        out_specs=pl.BlockSpec((ts, D_MODEL), lambda b, s: (b * ns + s, 0)),
        out_shape=jax.ShapeDtypeStruct((batch * seq, D_MODEL), BF16),
        scratch_shapes=[pltpu.VMEM((ts + CONV_HALO + SUBLANES, D_MODEL), F32),
                        pltpu.VMEM((SUBLANES - 1, ts + CONV_HALO, D_MODEL), F32)],
        compiler_params=_params(("arbitrary", "arbitrary")),
        name="conv_prompt",
    )(u, jnp.broadcast_to(w_dw[:, None, :], (CONV_WIDTH, SUBLANES, D_MODEL)), b_dw.reshape(1, D_MODEL), ln_g.reshape(1, D_MODEL), ln_b.reshape(1, D_MODEL))


def _conv_sample_kernel(u_ref, w_ref, bdw_ref, lg_ref, lb_ref, z_ref, *, dec_seq):
    for t in range(dec_seq):
        acc = jnp.zeros(u_ref.shape[1:], F32)
        for k in range(CONV_WIDTH):
            acc = acc + w_ref[k:k + 1, :] * u_ref[t + k]
        z_ref[t] = _ln_silu(acc + bdw_ref[...], lg_ref[...], lb_ref[...]).astype(BF16)


def _conv_sample(ucat_t, w_dw, b_dw, ln_g, ln_b, *, dec_seq):
    rows, n, _ = ucat_t.shape
    nb = 32 if n % 32 == 0 else n
    const = lambda i: (0, 0)
    return pl.pallas_call(
        functools.partial(_conv_sample_kernel, dec_seq=dec_seq),
        grid=(n // nb,),
        in_specs=[pl.BlockSpec((rows, nb, D_MODEL), lambda i: (0, i, 0)),
                  pl.BlockSpec((CONV_WIDTH, D_MODEL), const),
                  pl.BlockSpec((1, D_MODEL), const), pl.BlockSpec((1, D_MODEL), const),
                  pl.BlockSpec((1, D_MODEL), const)],
        out_specs=pl.BlockSpec((dec_seq, nb, D_MODEL), lambda i: (0, i, 0)),
        out_shape=jax.ShapeDtypeStruct((dec_seq, n, D_MODEL), BF16),
        compiler_params=_params(("parallel",)),
        name="conv_sample",
    )(ucat_t, w_dw, b_dw.reshape(1, D_MODEL), ln_g.reshape(1, D_MODEL), ln_b.reshape(1, D_MODEL))


def _softmax_with_sink(s, allowed, sink):
    if allowed is not None:
        s = jnp.where(allowed, s, -jnp.inf)
    m =jnp.maximum(jnp.max(s, axis=-1, keepdims=True), sink)
    e = jnp.exp(s - m)
    return e / (jnp.sum(e, axis=-1, keepdims=True) + jnp.exp(sink - m))


def _attn_prompt_kernel(sink_ref, q_ref, kp_ref, kc_ref, vp_ref, vc_ref, o_ref, bias_ref):
    i = pl.program_id(1)
    blk = q_ref.shape[0]
    qi = lax.broadcasted_iota(jnp.int32, (blk, 2 * blk), 0)
    kc = lax.broadcasted_iota(jnp.int32, (blk, 2 * blk), 1)
    allowed = (kc >= qi) & (kc <= qi + WINDOW) & ((i > 0) | (kc >= blk))
    bias_ref[...] = jnp.where(allowed, 0.0, -jnp.inf)
    for hk in range(N_KV_HEADS):
        cols = slice(hk * HEAD_DIM, (hk + 1) * HEAD_DIM)
        kcat = jnp.concatenate([kp_ref[:, cols], kc_ref[:, cols]], axis=0).astype(BF16)
        vcat = jnp.concatenate([vp_ref[:, cols], vc_ref[:, cols]], axis=0).astype(BF16)
        for pair in range(GROUP // 2):
            outs = []
            for g in (2 * pair, 2 * pair + 1):
                h = hk * GROUP + g
                q = q_ref[:, h * HEAD_DIM:(h + 1) * HEAD_DIM]
                s = lax.dot_general(q, kcat, (((1,), (1,)), ((), ())), preferred_element_type=F32)
                p = _softmax_with_sink(s + bias_ref[...], None, sink_ref[h])
                outs.append(jnp.dot(p.astype(BF16), vcat, preferred_element_type=F32))
            lo = (hk * GROUP + 2 * pair) * HEAD_DIM
            o_ref[:, lo:lo + 2 * HEAD_DIM] = jnp.concatenate(outs, axis=-1).astype(BF16)


def _attn_prompt(q, k, v, sinks, *, batch, seq):
    blk = WINDOW
    nb = seq // blk
    cur = lambda b, i, *_: (b * nb + i, 0)
    prev = lambda b, i, *_: (b * nb + jnp.maximum(i - 1, 0), 0)
    return pl.pallas_call(
        _attn_prompt_kernel,
        grid_spec=pltpu.PrefetchScalarGridSpec(
            num_scalar_prefetch=1, grid=(batch, nb),
            in_specs=[pl.BlockSpec((blk, D_MODEL), cur),
                      pl.BlockSpec((blk, KV_DIM), prev), pl.BlockSpec((blk, KV_DIM), cur),
                      pl.BlockSpec((blk, KV_DIM), prev), pl.BlockSpec((blk, KV_DIM), cur)],
            out_specs=pl.BlockSpec((blk, D_MODEL), cur),
            scratch_shapes=[pltpu.VMEM((blk, 2 * blk), F32)]),
        out_shape=jax.ShapeDtypeStruct((batch * seq, D_MODEL), BF16),
        compiler_params=_params(("parallel", "parallel")),
        name="attn_prompt",
    )(sinks, q, k, k, v, v)


def _attn_sample_kernel(q_ref, k_ref, v_ref, sink_ref, o_ref, *, dec_seq):
    ns, kvh, nq, hd = q_ref.shape
    nk = k_ref.shape[2]
    q = q_ref[...].reshape(ns * kvh, nq, hd)
    k = k_ref[...].reshape(ns * kvh, nk, hd).astype(BF16)
    v = v_ref[...].reshape(ns * kvh, nk, hd).astype(BF16)
    s = jnp.einsum('bqd,bkd->bqk', q, k, preferred_element_type=F32).reshape(ns, kvh, nq, nk)
    t = lax.broadcasted_iota(jnp.int32, (nq, nk), 0) % dec_seq
    c = lax.broadcasted_iota(jnp.int32, (nq, nk), 1)
    allowed = (c >= t) & (c <= t + WINDOW)
    p = _softmax_with_sink(s, allowed[None, None], sink_ref[...][None])
    p = p.reshape(ns * kvh, nq, nk).astype(BF16)
    o = jnp.einsum('bqk,bkd->bqd', p, v, preferred_element_type=F32)
    o_ref[...] = o.reshape(ns, kvh, nq, hd).astype(BF16)


def _attn_sample(q, kcat, vcat, sinks, *, dec_seq):
    n, kvh, nq, hd = q.shape
    nk = kcat.shape[2]
    ns = 8 if n % 8 == 0 else n
    sink_rows = jnp.repeat(sinks.reshape(N_KV_HEADS, GROUP), dec_seq, axis=1)[..., None]
    blk4 = lambda i: (i, 0, 0, 0)
    return pl.pallas_call(
        functools.partial(_attn_sample_kernel, dec_seq=dec_seq),
        grid=(n // ns,),
        in_specs=[pl.BlockSpec((ns, kvh, nq, hd), blk4), pl.BlockSpec((ns, kvh, nk, hd), blk4),
                  pl.BlockSpec((ns, kvh, nk, hd), blk4), pl.BlockSpec((kvh, nq, 1), lambda i: (0, 0, 0))],
        out_specs=pl.BlockSpec((ns, kvh, nq, hd), blk4),
        out_shape=jax.ShapeDtypeStruct((n, kvh, nq, hd), BF16),
        compiler_params=_params(("parallel",)),
        name="attn_sample",
    )(q, kcat, vcat, sink_rows)


def _proj_router_kernel(x_ref, zp_ref, zs_ref, w_ref, b_ref, g_ref, wrh_ref, wrl_ref, br_ref,
                        x1_ref, h_ref, meta_ref, cnt_ref, carry_ref, *, prompt_tiles):
    i = pl.program_id(0)
    tm = x_ref.shape[0]

    @pl.when(i == 0)
    def _():
        carry_ref[...] = jnp.zeros_like(carry_ref)

    z = jnp.where(i < prompt_tiles, zp_ref[...], zs_ref[...])
    x1 = x_ref[...] + (jnp.dot(z, w_ref[...], preferred_element_type=F32) + b_ref[...])
    x1_ref[...] = x1
    h = _rms(x1, g_ref[...])
    _to_token_major(h_ref, h)
    h_hi = h.astype(BF16)
    h_lo = (h - h_hi.astype(F32)).astype(BF16)
    logits = (jnp.dot(h_hi, wrh_ref[...], preferred_element_type=F32)
              + (jnp.dot(h_hi, wrl_ref[...], preferred_element_type=F32)
                 + jnp.dot(h_lo, wrh_ref[...], preferred_element_type=F32))) + br_ref[...]

    lane = lax.broadcasted_iota(jnp.int32, (tm, LANES), 1).astype(F32)
    first_lane = lambda hit: jnp.min(jnp.where(hit, lane, float(LANES)), axis=-1, keepdims=True)
    lg = jnp.where(lane < N_GROUPS, logits, -jnp.inf)
    mg = jnp.max(lg, axis=-1, keepdims=True)
    gsel = first_lane(lg == mg)
    gprob = 1.0 / jnp.sum(jnp.exp(lg - mg), axis=-1, keepdims=True)
    lo = ROUTER_LANE0 + gsel * EXPERTS_PER_GROUP
    in_group = (lane >= lo) & (lane < lo + EXPERTS_PER_GROUP)
    le = jnp.where(in_group, logits, -jnp.inf)
    ee = jnp.exp(le - jnp.max(le, axis=-1, keepdims=True))
    pe = jnp.where(in_group, ee / jnp.sum(ee, axis=-1, keepdims=True), -1.0)
    p1 = jnp.max(pe, axis=-1, keepdims=True)
    i1 = first_lane(pe == p1)
    pe2 = jnp.where(lane == i1, -1.0, pe)
    p2 = jnp.max(pe2, axis=-1, keepdims=True)
    i2 = first_lane(pe2 == p2)
    denom = p1 + p2
    g1 = gprob * p1 / denom
    g2 = gprob * p2 / denom
    sel1 = lane == i1
    sel2 = lane == i2
    onehot = jnp.where(sel1 | sel2, 1.0, 0.0)
    r = lax.broadcasted_iota(jnp.int32, (tm, tm), 0)
    c = lax.broadcasted_iota(jnp.int32, (tm, tm), 1)
    before = jnp.where(c < r, 1.0, 0.0).astype(BF16)
    rank = jnp.dot(before, onehot.astype(BF16), preferred_element_type=F32) + carry_ref[...]
    r1 = jnp.sum(jnp.where(sel1, rank, 0.0), axis=-1, keepdims=True)
    r2 = jnp.sum(jnp.where(sel2, rank, 0.0), axis=-1, keepdims=True)
    carry = carry_ref[...] + jnp.sum(onehot, axis=0, keepdims=True)
    carry_ref[...] = carry
    cnt_ref[...] = carry
    e1 = i1 - ROUTER_LANE0
    e2 = i2 - ROUTER_LANE0
    meta = jnp.zeros((tm, LANES), F32)
    for col, val in enumerate((e1, e2, r1, r2, g1, g2)):
        meta = jnp.where(lane == col, val, meta)
    meta_ref[...] = meta


def _proj_router(x, z_p, z_s, w, b, g, w_rg, b_rg, w_re, b_re):
    t = x.shape[0]
    tp, ts = z_p.shape[0], z_s.shape[0]
    tm = _token_tile(tp, ts)
    npt = tp // tm
    pad = LANES - N_GROUPS - N_EXPERTS
    wr = jnp.concatenate([w_rg, w_re, jnp.zeros((D_MODEL, pad), F32)], axis=1)
    wr_hi = wr.astype(BF16)
    wr_lo = (wr - wr_hi.astype(F32)).astype(BF16)
    br = jnp.concatenate([b_rg, b_re, jnp.zeros((pad,), F32)]).reshape(1, LANES)
    row = lambda i: (i, 0)
    const = lambda i: (0, 0)
    return pl.pallas_call(
        functools.partial(_proj_router_kernel, prompt_tiles=npt),
        grid=(t // tm,),
        in_specs=[pl.BlockSpec((tm, D_MODEL), row),
                  pl.BlockSpec((tm, D_MODEL), lambda i: (jnp.minimum(i, npt - 1), 0)),
                  pl.BlockSpec((tm, D_MODEL), lambda i: (jnp.maximum(i - npt, 0), 0)),
                  pl.BlockSpec((D_MODEL, D_MODEL), const), pl.BlockSpec((1, D_MODEL), const),
                  pl.BlockSpec((1, D_MODEL), const), pl.BlockSpec((D_MODEL, LANES), const),
                  pl.BlockSpec((D_MODEL, LANES), const), pl.BlockSpec((1, LANES), const)],
        out_specs=[pl.BlockSpec((tm, D_MODEL), row), pl.BlockSpec((tm * ROW_TILES, LANES), row),
                   pl.BlockSpec((tm, LANES), row), pl.BlockSpec((1, LANES), const)],
        out_shape=[jax.ShapeDtypeStruct((t, D_MODEL), F32), jax.ShapeDtypeStruct((t * ROW_TILES, LANES), F32),
                   jax.ShapeDtypeStruct((t, LANES), F32), jax.ShapeDtypeStruct((1, LANES), F32)],
        scratch_shapes=[pltpu.VMEM((1, LANES), F32)],
        compiler_params=_params(("arbitrary",)),
        name="proj_router",
    )(x, z_p, z_s, w.astype(BF16), b.reshape(1, D_MODEL), g.reshape(1, D_MODEL), wr_hi, wr_lo, br)


def _dispatch_plan(meta, counts, t):
    n = t * TOP_K
    n_blocks = -(-(n + N_EXPERTS * (EXPERT_BLOCK - 1)) // EXPERT_BLOCK)
    eid = meta[:, 0:TOP_K].astype(jnp.int32)
    rank = meta[:, TOP_K:2 * TOP_K].astype(jnp.int32)
    cnt = counts[0, ROUTER_LANE0:ROUTER_LANE0 + N_EXPERTS].astype(jnp.int32)
    padded = (cnt + EXPERT_BLOCK - 1) // EXPERT_BLOCK * EXPERT_BLOCK
    pad_end = jnp.cumsum(padded)
    pad_start = pad_end - padded
    experts = jnp.arange(N_EXPERTS, dtype=jnp.int32)
    start_of = jnp.sum(jnp.where(eid[..., None] == experts, pad_start, 0), axis=-1)
    dest = (start_of + rank).reshape(n)
    block_row0 = jnp.arange(n_blocks, dtype=jnp.int32) * EXPERT_BLOCK
    block_e = jnp.minimum(jnp.sum((pad_end[None, :] <= block_row0[:, None]).astype(jnp.int32), axis=1),
                          N_EXPERTS - 1)
    nused = (pad_end[-1] // EXPERT_BLOCK).reshape(1)
    zstart = jnp.where(padded > cnt, pad_end - EXPERT_BLOCK, -1)
    return dest, block_e, nused, zstart, n_blocks


def _dispatch_kernel(dest_ref, zstart_ref, nused_ref, h_ref, xs_hbm, sbuf, zbuf, zsem, dsem, *, n_blocks):
    i = pl.program_id(0)
    tm = h_ref.shape[0] // ROW_TILES

    @pl.when(i == 0)
    def _():
        zbuf[...] = jnp.zeros_like(zbuf)
        nused = nused_ref[0]

        def zero_block(row0):
            rows = pl.ds(pl.multiple_of(row0 * ROW_TILES, EXPERT_BLOCK * ROW_TILES), EXPERT_BLOCK * ROW_TILES)
            return pltpu.make_async_copy(zbuf, xs_hbm.at[rows, :], zsem.at[0])

        def per_partial_block(fn):
            def body(e, carry):
                @pl.when(zstart_ref[e] >= 0)
                def _():
                    fn(pl.multiple_of(zstart_ref[e], EXPERT_BLOCK))
                return carry

            lax.fori_loop(0, N_EXPERTS, body, 0)

        def per_unused_block(fn):
            def body(j, carry):
                fn(pl.multiple_of(j * EXPERT_BLOCK, EXPERT_BLOCK))
                return carry

            lax.fori_loop(nused, n_blocks, body, 0)

        per_partial_block(lambda row0: zero_block(row0).start())
        per_unused_block(lambda row0: zero_block(row0).start())
        per_partial_block(lambda row0: zero_block(row0).wait())
        per_unused_block(lambda row0: zero_block(row0).wait())

    slot = i % 2
    sbuf[slot] = h_ref[...]

    in_flight = min(DISPATCH_LAG, tm * TOP_K // WAIT_CHUNK)

    @pl.when(i > 0)
    def _():
        _wait_row_copies(sbuf.at[1 - slot], xs_hbm, dsem.at[1 - slot], in_flight * WAIT_CHUNK)

    def start_token(j):
        tok = i * tm + j
        for k in range(TOP_K):
            _row_copy(sbuf.at[slot], j, xs_hbm, dest_ref[tok * TOP_K + k], dsem.at[slot]).start()

    _issue_rows_rolling(tm, start_token,
                        lambda: _wait_row_copies(sbuf.at[slot], xs_hbm, dsem.at[slot], WAIT_CHUNK), DISPATCH_LAG)

    @pl.when(i == pl.num_programs(0) - 1)
    def _():
        _wait_row_copies(sbuf.at[slot], xs_hbm, dsem.at[slot], in_flight * WAIT_CHUNK)


def _dispatch(h, dest, zstart, nused, n_blocks):
    t = h.shape[0] // ROW_TILES
    tm = _token_tile(t, largest=COMBINE_TILE)
    return pl.pallas_call(
        functools.partial(_dispatch_kernel, n_blocks=n_blocks),
        grid_spec=pltpu.PrefetchScalarGridSpec(
            num_scalar_prefetch=3, grid=(t // tm,),
            in_specs=[pl.BlockSpec((tm * ROW_TILES, LANES), lambda i, *_: (i, 0))],
            out_specs=pl.BlockSpec(memory_space=pl.ANY),
            scratch_shapes=[pltpu.VMEM((2, tm * ROW_TILES, LANES), F32),
                            pltpu.VMEM((EXPERT_BLOCK * ROW_TILES, LANES), F32),
                            pltpu.SemaphoreType.DMA((1,)), pltpu.SemaphoreType.DMA((2,))]),
        out_shape=jax.ShapeDtypeStruct((n_blocks * EXPERT_BLOCK * ROW_TILES, LANES), F32),
        compiler_params=_params(("arbitrary",)),
        name="dispatch",
    )(dest, zstart, nused, h)


def _experts_kernel(be_ref, nused_ref, xs_ref, wg_ref, wu_ref, wd_ref, ys_ref, wg_bf, wu_bf, wd_bf):
    i = pl.program_id(0)
    nused = nused_ref[0]

    @pl.when(i < nused)
    def _():
        new_expert = (i == 0) | (be_ref[i] != be_ref[jnp.maximum(i - 1, 0)])

        @pl.when(new_expert)
        def _():
            wg_bf[...] = wg_ref[0].astype(BF16)
            wu_bf[...] = wu_ref[0].astype(BF16)
            wd_bf[...] = wd_ref[0].astype(BF16)

        x = _from_token_major(xs_ref, EXPERT_BLOCK).astype(BF16)
        a = jnp.dot(x, wg_bf[...], preferred_element_type=F32)
        u = jnp.dot(x, wu_bf[...], preferred_element_type=F32)
        act = (jax.nn.silu(a) * u).astype(BF16)
        _to_token_major(ys_ref, jnp.dot(act, wd_bf[...], preferred_element_type=F32))

    @pl.when(i >= nused)
    def _():
        ys_ref[...] = jnp.zeros_like(ys_ref)


def _experts(xs, block_e, nused, w_gate, w_up, w_down, layer):
    n_blocks = block_e.shape[0]
    wsel = lambda i, be, nu: (layer, be[i], 0, 0)
    return pl.pallas_call(
        _experts_kernel,
        grid_spec=pltpu.PrefetchScalarGridSpec(
            num_scalar_prefetch=2, grid=(n_blocks,),
            in_specs=[pl.BlockSpec((EXPERT_BLOCK * ROW_TILES, LANES),
                                   lambda i, be, nu: (jnp.minimum(i, nu[0] - 1), 0)),
                      pl.BlockSpec((None, 1, D_MODEL, D_EXPERT), wsel),
                      pl.BlockSpec((None, 1, D_MODEL, D_EXPERT), wsel),
                      pl.BlockSpec((None, 1, D_EXPERT, D_MODEL), wsel)],
            out_specs=pl.BlockSpec((EXPERT_BLOCK * ROW_TILES, LANES), lambda i, be, nu: (i, 0)),
            scratch_shapes=[pltpu.VMEM((D_MODEL, D_EXPERT), BF16),
                            pltpu.VMEM((D_MODEL, D_EXPERT), BF16),
                            pltpu.VMEM((D_EXPERT, D_MODEL), BF16)]),
        out_shape=jax.ShapeDtypeStruct((n_blocks * EXPERT_BLOCK * ROW_TILES, LANES), F32),
        compiler_params=_params(("arbitrary",)),
        name="experts",
    )(block_e, nused, xs, w_gate, w_up, w_down)


def _final_kernel(dest_ref, x_ref, meta_ref, ys_hbm, g_ref, yp_ref, ys_ref, mbuf, sem, *, prompt_tiles):
    i = pl.program_id(0)
    y = _rms(_moe_combine(dest_ref, x_ref, meta_ref, ys_hbm, mbuf, sem), g_ref[...])

    @pl.when(i < prompt_tiles)
    def _():
        yp_ref[...] = y

    @pl.when(i >= prompt_tiles)
    def _():
        ys_ref[...] = y


def _final_norm(x, moe, g, *, tp, tm):
    t = x.shape[0]
    dest, meta, ys = moe
    npt = tp // tm
    c_specs, c_scratch = _combine_specs(tm)
    return pl.pallas_call(
        functools.partial(_final_kernel, prompt_tiles=npt),
        grid_spec=pltpu.PrefetchScalarGridSpec(
            num_scalar_prefetch=1, grid=(t // tm,),
            in_specs=c_specs + [pl.BlockSpec((1, D_MODEL), lambda i, *_: (0, 0))],
            out_specs=[pl.BlockSpec((tm, D_MODEL), lambda i, *_: (jnp.minimum(i, npt - 1), 0)),
                       pl.BlockSpec((tm, D_MODEL), lambda i, *_: (jnp.maximum(i - npt, 0), 0))],
            scratch_shapes=c_scratch),
        out_shape=[jax.ShapeDtypeStruct((tp, D_MODEL), F32), jax.ShapeDtypeStruct((t - tp, D_MODEL), F32)],
        compiler_params=_params(("arbitrary",)),
        name="final_norm",
    )(dest, x, meta, ys, g.reshape(1, D_MODEL))


def kernel(x_prompt, x_sample, cache_k, cache_v, state_conv, norm_mix, norm_ffn, norm_final, conv_w_pw1, conv_b_pw1, conv_w_dw, conv_b_dw, conv_ln_g, conv_ln_b, conv_w_pw2, conv_b_pw2, attn_w_qkv, attn_b_qkv, attn_sinks, attn_w_o, attn_b_o, moe_w_router_group, moe_b_router_group, moe_w_router_expert, moe_b_router_expert, moe_w_gate, moe_w_up, moe_w_down):
    batch, seq, _ = x_prompt.shape
    n_dec, dec_seq, _ = x_sample.shape
    depth = norm_mix.shape[0]
    tp, ts = batch * seq, n_dec * dec_seq
    t = tp + ts
    tm = _token_tile(tp, ts, largest=COMBINE_TILE)

    x = (x_prompt.reshape(tp, D_MODEL), x_sample.reshape(ts, D_MODEL))
    moe = None
    kp_list, vp_list, cp_list, ks_list, vs_list, cs_list = [], [], [], [], [], []
    for layer in range(depth):
        j = layer // 2
        if layer % 2 == 0:
            x, u = _mixer_in(x, moe, norm_mix[layer], conv_w_pw1[j], conv_b_pw1[j], is_conv=True, tm=tm)
            u_p = u[:tp].reshape(batch, seq, D_MODEL)
            ucat_s = jnp.concatenate([state_conv[j], u[tp:].reshape(n_dec, dec_seq, D_MODEL)], axis=1)
            cp_list.append(u_p[:, seq - (CONV_WIDTH - 1):])
            cs_list.append(ucat_s[:, dec_seq:])
            z_p = _conv_prompt(u, conv_w_dw[j], conv_b_dw[j], conv_ln_g[j], conv_ln_b[j], batch=batch, seq=seq)
            z_s = _conv_sample(jnp.swapaxes(ucat_s, 0, 1), conv_w_dw[j], conv_b_dw[j], conv_ln_g[j], conv_ln_b[j],
                               dec_seq=dec_seq)
            z_s = jnp.swapaxes(z_s, 0, 1).reshape(ts, D_MODEL)
            w_out, b_out = conv_w_pw2[j], conv_b_pw2[j]
        else:
            x, q, k, v = _mixer_in(x, moe, norm_mix[layer], attn_w_qkv[j], attn_b_qkv[j], is_conv=False, tm=tm)
            k_p = k[:tp].reshape(batch, seq, N_KV_HEADS, HEAD_DIM)
            v_p = v[:tp].reshape(batch, seq, N_KV_HEADS, HEAD_DIM)
            kp_list.append(k_p[:, seq - WINDOW:])
            vp_list.append(v_p[:, seq - WINDOW:])
            kcat = jnp.concatenate([cache_k[j], k[tp:].reshape(n_dec, dec_seq, N_KV_HEADS, HEAD_DIM)], axis=1)
            vcat = jnp.concatenate([cache_v[j], v[tp:].reshape(n_dec, dec_seq, N_KV_HEADS, HEAD_DIM)], axis=1)
            ks_list.append(kcat[:, dec_seq:])
            vs_list.append(vcat[:, dec_seq:])
            z_p = _attn_prompt(q, k, v, attn_sinks[j], batch=batch, seq=seq)
            q_s = q[tp:].reshape(n_dec, dec_seq, N_KV_HEADS, GROUP, HEAD_DIM).transpose(0, 2, 3, 1, 4)
            q_s = q_s.reshape(n_dec, N_KV_HEADS, GROUP * dec_seq, HEAD_DIM)
            o_s = _attn_sample(q_s, kcat.transpose(0, 2, 1, 3), vcat.transpose(0, 2, 1, 3), attn_sinks[j],
                               dec_seq=dec_seq)
            o_s = o_s.reshape(n_dec, N_KV_HEADS, GROUP, dec_seq, HEAD_DIM).transpose(0, 3, 1, 2, 4)
            z_s = o_s.reshape(ts, D_MODEL)
            w_out, b_out = attn_w_o[j], attn_b_o[j]
        x, h, meta, counts = _proj_router(x, z_p, z_s, w_out, b_out, norm_ffn[layer],
                                          moe_w_router_group[layer], moe_b_router_group[layer],
                                          moe_w_router_expert[layer], moe_b_router_expert[layer])
        dest, block_e, nused, zstart, n_blocks = _dispatch_plan(meta, counts, t)
        xs = _dispatch(h, dest, zstart, nused, n_blocks)
        ys = _experts(xs, block_e, nused, moe_w_gate, moe_w_up, moe_w_down, layer)
        moe = (dest, meta, ys)
    y_p, y_s = _final_norm(x, moe, norm_final, tp=tp, tm=tm)
    return (y_p.reshape(batch, seq, D_MODEL), y_s.reshape(n_dec, dec_seq, D_MODEL),
            jnp.stack(kp_list), jnp.stack(vp_list), jnp.stack(cp_list),
            jnp.stack(ks_list), jnp.stack(vs_list), jnp.stack(cs_list))
```

```python
import functools

import jax
import jax.numpy as jnp
from jax import lax
from jax.experimental import pallas as pl
from jax.experimental.pallas import tpu as pltpu

F32 = jnp.float32
BF16 = jnp.bfloat16

D_MODEL = 1024
CONV_WIDTH = 31
N_HEADS = 16
N_KV_HEADS = 4
HEAD_DIM = D_MODEL // N_HEADS
GROUP = N_HEADS // N_KV_HEADS
KV_DIM = N_KV_HEADS * HEAD_DIM
WINDOW = 128
N_GROUPS = 8
EXPERTS_PER_GROUP = 8
N_EXPERTS = N_GROUPS * EXPERTS_PER_GROUP
TOP_K = 2
D_EXPERT = D_MODEL // 2
EXPERT_BLOCK = 256
RMS_EPS = 1e-6
LN_EPS = 1e-5

LANES = 128
SUBLANES = 8
ROW_TILES = D_MODEL // LANES
assert ROW_TILES == SUBLANES
ROUTER_LANE0 = N_GROUPS
META_GATE_LANE = 2 * TOP_K
CONV_HALO = 32
CONV_ROWS = 32
WAIT_CHUNK = 128
ISSUE_UNROLL = 8
DISPATCH_LAG = 2
COMBINE_LAG = 5
COMBINE_TILE = 512
VMEM_LIMIT =56 * 1024 * 1024


def _params(sem):
    return pltpu.CompilerParams(dimension_semantics=sem, vmem_limit_bytes=VMEM_LIMIT)


def _rms(x, g):
    return x * lax.rsqrt(jnp.mean(x * x, axis=-1, keepdims=True) + RMS_EPS) * g


def _token_tile(*counts, largest=512):
    for tm in (512, 256, 128):
        if tm <= largest and all(c % tm == 0 for c in counts):
            return tm
    raise ValueError("token counts must be multiples of 128")


def _to_token_major(ref, x):
    n = x.shape[0]
    for s in range(ROW_TILES):
        ref[pl.ds(s, n, stride=ROW_TILES), :] = x[:, s * LANES:(s + 1) * LANES]


def _from_token_major(ref, n):
    return jnp.concatenate([ref[pl.ds(s, n, stride=ROW_TILES), :] for s in range(ROW_TILES)], axis=1)


def _row_slice(row):
    start = row * ROW_TILES
    return pl.ds(start if isinstance(start, int) else pl.multiple_of(start, ROW_TILES), ROW_TILES)


def _row_copy(src, src_row, dst, dst_row, sem):
    return pltpu.make_async_copy(src.at[_row_slice(src_row), :], dst.at[_row_slice(dst_row), :], sem)


def _wait_row_copies(src_hbm, dst, sem, n_rows):
    def chunk(c, carry):
        for _ in range(WAIT_CHUNK):
            _row_copy(src_hbm, 0, dst, 0, sem).wait()
        return carry

    lax.fori_loop(0, n_rows // WAIT_CHUNK, chunk, 0)


def _issue_rows_rolling(n_tokens, start_token, wait_chunk, lag):
    chunk_tokens = WAIT_CHUNK // TOP_K
    n_chunks = n_tokens // chunk_tokens

    def chunk(c, carry):
        def body(j, inner):
            start_token(c * chunk_tokens + j)
            return inner

        lax.fori_loop(0, chunk_tokens, body, 0, unroll=ISSUE_UNROLL)

        @pl.when(c >= lag)
        def _():
            wait_chunk()

        return carry

    lax.fori_loop(0, n_chunks, chunk, 0)
    return min(lag, n_chunks)


def _moe_combine(dest_ref, x_ref, meta_ref, ys_hbm, mbuf, sem):
    i = pl.program_id(0)
    tm = x_ref.shape[0]

    def issue(tile, slot):
        def start_token(j):
            tok = tile * tm + j
            for k in range(TOP_K):
                _row_copy(ys_hbm, dest_ref[tok * TOP_K + k], mbuf.at[slot, k], j, sem.at[slot]).start()

        return _issue_rows_rolling(
            tm, start_token, lambda: _wait_row_copies(ys_hbm, mbuf.at[slot, 0], sem.at[slot], WAIT_CHUNK),
            COMBINE_LAG)

    in_flight = min(COMBINE_LAG, tm * TOP_K // WAIT_CHUNK)

    @pl.when(i == 0)
    def _():
        issue(0, 0)

    slot = i % 2
    _wait_row_copies(ys_hbm, mbuf.at[slot, 0], sem.at[slot], in_flight * WAIT_CHUNK)

    @pl.when(i + 1 < pl.num_programs(0))
    def _():
        issue(i + 1, (i + 1) % 2)

    moe = meta_ref[:, META_GATE_LANE:META_GATE_LANE + 1] * _from_token_major(mbuf.at[slot, 0], tm)
    for k in range(1, TOP_K):
        moe = moe + (meta_ref[:, META_GATE_LANE + k:META_GATE_LANE + k + 1]
                     * _from_token_major(mbuf.at[slot, k], tm))
    return x_ref[...] + moe


def _combine_specs(tm):
    in_specs = [pl.BlockSpec((tm, D_MODEL), lambda i, *_: (i, 0)),
                pl.BlockSpec((tm, LANES), lambda i, *_: (i, 0)),
                pl.BlockSpec(memory_space=pl.ANY)]
    scratch = [pltpu.VMEM((2, TOP_K, tm * ROW_TILES, LANES), F32), pltpu.SemaphoreType.DMA((2,))]
    return in_specs, scratch


def _mixer_proj(x, g_ref, w_ref, b_ref, outs, is_conv):
    h = _rms(x, g_ref[...]).astype(BF16)
    y = jnp.dot(h, w_ref[...], preferred_element_type=F32) + b_ref[...]
    if is_conv:
        (u_ref,) = outs
        u_ref[...] = y[:, :D_MODEL] * jax.nn.sigmoid(y[:, D_MODEL:])
    else:
        q_ref, k_ref, v_ref = outs
        q_ref[...] = (y[:, :D_MODEL] * (HEAD_DIM ** -0.5)).astype(BF16)
        k_ref[...] = y[:, D_MODEL:D_MODEL + KV_DIM]
        v_ref[...] = y[:, D_MODEL + KV_DIM:]


def _mixer_in_first_kernel(xp_ref, xs_ref, g_ref, w_ref, b_ref, xo_ref, *outs, is_conv, prompt_tiles):
    x = jnp.where(pl.program_id(0) < prompt_tiles, xp_ref[...], xs_ref[...])
    xo_ref[...] = x
    _mixer_proj(x, g_ref, w_ref, b_ref, outs, is_conv)


def _mixer_in_kernel(dest_ref, x_ref, meta_ref, ys_hbm, g_ref, w_ref, b_ref, xo_ref, *rest, is_conv):
    *outs, mbuf, sem = rest
    x = _moe_combine(dest_ref, x_ref, meta_ref, ys_hbm, mbuf, sem)
    xo_ref[...] = x
    _mixer_proj(x, g_ref, w_ref, b_ref, outs, is_conv)


def _mixer_in(x, moe, g, w, b, *, is_conv, tm):
    t = x.shape[0] if moe is not None else x[0].shape[0] + x[1].shape[0]
    n_out = w.shape[1]
    row = lambda i, *_: (i, 0)
    const = lambda i, *_: (0, 0)
    w_specs = [pl.BlockSpec((1, D_MODEL), const), pl.BlockSpec((D_MODEL, n_out), const),
               pl.BlockSpec((1, n_out), const)]
    w_args = [g.reshape(1, D_MODEL), w.astype(BF16), b.reshape(1, n_out)]
    if is_conv:
        out_shape = [jax.ShapeDtypeStruct((t, D_MODEL), F32)]
        out_specs = [pl.BlockSpec((tm, D_MODEL), row)]
    else:
        out_shape = [jax.ShapeDtypeStruct((t, D_MODEL), BF16), jax.ShapeDtypeStruct((t, KV_DIM), F32),
                     jax.ShapeDtypeStruct((t, KV_DIM), F32)]
        out_specs = [pl.BlockSpec((tm, D_MODEL), row), pl.BlockSpec((tm, KV_DIM), row),
                     pl.BlockSpec((tm, KV_DIM), row)]
    name = "mixer_in_conv" if is_conv else "mixer_in_attn"
    if moe is None:
        x_p, x_s = x
        npt = x_p.shape[0] // tm
        return pl.pallas_call(
            functools.partial(_mixer_in_first_kernel, is_conv=is_conv, prompt_tiles=npt),
            grid=(t // tm,),
            in_specs=[pl.BlockSpec((tm, D_MODEL), lambda i: (jnp.minimum(i, npt - 1), 0)),
                      pl.BlockSpec((tm, D_MODEL), lambda i: (jnp.maximum(i - npt, 0), 0))] + w_specs,
            out_specs=[pl.BlockSpec((tm, D_MODEL), row)] + out_specs,
            out_shape=[jax.ShapeDtypeStruct((t, D_MODEL), F32)] + out_shape,
            compiler_params=_params(("parallel",)), name=name + "_first",
        )(x_p, x_s, *w_args)
    dest, meta, ys = moe
    c_specs, c_scratch = _combine_specs(tm)
    return pl.pallas_call(
        functools.partial(_mixer_in_kernel, is_conv=is_conv),
        grid_spec=pltpu.PrefetchScalarGridSpec(
            num_scalar_prefetch=1, grid=(t // tm,),
            in_specs=c_specs + w_specs,
            out_specs=[pl.BlockSpec((tm, D_MODEL), row)] + out_specs,
            scratch_shapes=c_scratch),
        out_shape=[jax.ShapeDtypeStruct((t, D_MODEL), F32)] + out_shape,
        compiler_params=_params(("arbitrary",)), name=name,
    )(dest, x, meta, ys, *w_args)


def _ln_silu(c, g, b):
    mu = jnp.mean(c, axis=-1, keepdims=True)
    d = c - mu
    var = jnp.mean(d * d, axis=-1, keepdims=True)
    return jax.nn.silu(d * lax.rsqrt(var + LN_EPS) * g + b)


def _conv_prompt_kernel(u_ref, w_ref, bdw_ref, lg_ref, lb_ref, z_ref, ubuf, shifted, *, ts):
    s = pl.program_id(1)
    nbuf = ts + CONV_HALO

    @pl.when(s == 0)
    def _():
        ubuf[0:CONV_HALO, :] = jnp.zeros((CONV_HALO, D_MODEL), F32)

    ubuf[CONV_HALO:nbuf, :] = u_ref[...]
    ubuf[nbuf:nbuf + SUBLANES, :] = jnp.zeros((SUBLANES, D_MODEL), F32)

    def shift_chunk(c, carry):
        r0 = pl.multiple_of(c * CONV_ROWS, CONV_ROWS)
        win = ubuf[pl.ds(r0, CONV_ROWS + SUBLANES), :]
        for b in range(1, SUBLANES):
            rolled = pltpu.roll(win, CONV_ROWS + SUBLANES - b, axis=0)
            shifted[b - 1, pl.ds(r0, CONV_ROWS), :] = rolled[:CONV_ROWS]
        return carry

    lax.fori_loop(0, nbuf // CONV_ROWS, shift_chunk, 0)
    first = CONV_HALO - (CONV_WIDTH - 1)

    def chunk(c, carry):
        base = pl.multiple_of(c * CONV_ROWS, CONV_ROWS)
        groups = CONV_ROWS // SUBLANES
        accs = [jnp.zeros((SUBLANES, D_MODEL), F32) for _ in range(groups)]
        for k in range(CONV_WIDTH):
            a, b = divmod(first + k, SUBLANES)
            wk = w_ref[k]
            for g in range(groups):
                rows = pl.ds(base + (a + g) * SUBLANES, SUBLANES)
                tap = ubuf[rows, :] if b == 0 else shifted[b - 1, rows, :]
                accs[g] = accs[g] + wk * tap
        acc = jnp.concatenate(accs, axis=0)
        z = _ln_silu(acc + bdw_ref[...], lg_ref[...], lb_ref[...])
        z_ref[pl.ds(base, CONV_ROWS), :] = z.astype(BF16)
        return carry

    lax.fori_loop(0, ts // CONV_ROWS, chunk, 0)
    ubuf[0:CONV_HALO, :] = ubuf[ts:nbuf, :]


def _conv_prompt(u, w_dw, b_dw, ln_g, ln_b, *, batch, seq):
    ts = 512 if seq % 512 == 0 else seq
    ns = seq // ts
    const = lambda b, s: (0, 0)
    return pl.pallas_call(
        functools.partial(_conv_prompt_kernel, ts=ts),
        grid=(batch, ns),
        in_specs=[pl.BlockSpec((ts, D_MODEL), lambda b, s: (b * ns + s, 0)),
                  pl.BlockSpec((CONV_WIDTH, SUBLANES, D_MODEL), lambda b, s: (0, 0, 0)),
                  pl.BlockSpec((1, D_MODEL), const), pl.BlockSpec((1, D_MODEL), const),
                  pl.BlockSpec((1, D_MODEL), const)],
        out_specs=pl.BlockSpec((ts, D_MODEL), lambda b, s: (b * ns + s, 0)),
        out_shape=jax.ShapeDtypeStruct((batch * seq, D_MODEL), BF16),
        scratch_shapes=[pltpu.VMEM((ts + CONV_HALO + SUBLANES, D_MODEL), F32),
                        pltpu.VMEM((SUBLANES - 1, ts + CONV_HALO, D_MODEL), F32)],
        compiler_params=_params(("arbitrary", "arbitrary")),
        name="conv_prompt",
    )(u, jnp.broadcast_to(w_dw[:, None, :], (CONV_WIDTH, SUBLANES, D_MODEL)), b_dw.reshape(1, D_MODEL), ln_g.reshape(1, D_MODEL), ln_b.reshape(1, D_MODEL))


def _conv_sample_kernel(st_ref, us_ref, w_ref, bdw_ref, lg_ref, lb_ref, z_ref, *, dec_seq):
    n_past = st_ref.shape[0]
    for t in range(dec_seq):
        acc = jnp.zeros(us_ref.shape[1:], F32)
        for k in range(CONV_WIDTH):
            r = t + k
            acc = acc + w_ref[k:k + 1, :] * (st_ref[r] if r < n_past else us_ref[r - n_past])
        z_ref[t] = _ln_silu(acc + bdw_ref[...], lg_ref[...], lb_ref[...]).astype(BF16)


def _conv_sample(state_t, j, us_t, w_dw, b_dw, ln_g, ln_b):
    _, n_past, n, _ = state_t.shape
    dec_seq = us_t.shape[0]
    nb = 32 if n % 32 == 0 else n
    const = lambda i: (0, 0)
    return pl.pallas_call(
        functools.partial(_conv_sample_kernel, dec_seq=dec_seq),
        grid=(n // nb,),
        in_specs=[pl.BlockSpec((None, n_past, nb, D_MODEL), lambda i: (j, 0, i, 0)),
                  pl.BlockSpec((dec_seq, nb, D_MODEL), lambda i: (0, i, 0)),
                  pl.BlockSpec((CONV_WIDTH, D_MODEL), const),
                  pl.BlockSpec((1, D_MODEL), const), pl.BlockSpec((1, D_MODEL), const),
                  pl.BlockSpec((1, D_MODEL), const)],
        out_specs=pl.BlockSpec((dec_seq, nb, D_MODEL), lambda i: (0, i, 0)),
        out_shape=jax.ShapeDtypeStruct((dec_seq, n, D_MODEL), BF16),
        compiler_params=_params(("parallel",)),
        name="conv_sample",
    )(state_t, us_t, w_dw, b_dw.reshape(1, D_MODEL), ln_g.reshape(1, D_MODEL), ln_b.reshape(1, D_MODEL))


def _softmax_with_sink(s, allowed, sink):
    if allowed is not None:
        s = jnp.where(allowed, s, -jnp.inf)
    m =jnp.maximum(jnp.max(s, axis=-1, keepdims=True), sink)
    e = jnp.exp(s - m)
    return e / (jnp.sum(e, axis=-1, keepdims=True) + jnp.exp(sink - m))


def _attn_prompt_kernel(sink_ref, q_ref, kp_ref, kc_ref, vp_ref, vc_ref, o_ref, bias_ref):
    i = pl.program_id(1)
    blk = q_ref.shape[0]
    qi = lax.broadcasted_iota(jnp.int32, (blk, 2 * blk), 0)
    kc = lax.broadcasted_iota(jnp.int32, (blk, 2 * blk), 1)
    allowed = (kc >= qi) & (kc <= qi + WINDOW) & ((i > 0) | (kc >= blk))
    bias_ref[...] = jnp.where(allowed, 0.0, -jnp.inf)
    for hk in range(N_KV_HEADS):
        cols = slice(hk * HEAD_DIM, (hk + 1) * HEAD_DIM)
        kcat = jnp.concatenate([kp_ref[:, cols], kc_ref[:, cols]], axis=0).astype(BF16)
        vcat = jnp.concatenate([vp_ref[:, cols], vc_ref[:, cols]], axis=0).astype(BF16)
        for pair in range(GROUP // 2):
            outs = []
            for g in (2 * pair, 2 * pair + 1):
                h = hk * GROUP + g
                q = q_ref[:, h * HEAD_DIM:(h + 1) * HEAD_DIM]
                s = lax.dot_general(q, kcat, (((1,), (1,)), ((), ())), preferred_element_type=F32)
                p = _softmax_with_sink(s + bias_ref[...], None, sink_ref[h])
                outs.append(jnp.dot(p.astype(BF16), vcat, preferred_element_type=F32))
            lo = (hk * GROUP + 2 * pair) * HEAD_DIM
            o_ref[:, lo:lo + 2 * HEAD_DIM] = jnp.concatenate(outs, axis=-1).astype(BF16)


def _attn_prompt(q, k, v, sinks, *, batch, seq):
    blk = WINDOW
    nb = seq // blk
    cur = lambda b, i, *_: (b * nb + i, 0)
    prev = lambda b, i, *_: (b * nb + jnp.maximum(i - 1, 0), 0)
    return pl.pallas_call(
        _attn_prompt_kernel,
        grid_spec=pltpu.PrefetchScalarGridSpec(
            num_scalar_prefetch=1, grid=(batch, nb),
            in_specs=[pl.BlockSpec((blk, D_MODEL), cur),
                      pl.BlockSpec((blk, KV_DIM), prev), pl.BlockSpec((blk, KV_DIM), cur),
                      pl.BlockSpec((blk, KV_DIM), prev), pl.BlockSpec((blk, KV_DIM), cur)],
            out_specs=pl.BlockSpec((blk, D_MODEL), cur),
            scratch_shapes=[pltpu.VMEM((blk, 2 * blk), F32)]),
        out_shape=jax.ShapeDtypeStruct((batch * seq, D_MODEL), BF16),
        compiler_params=_params(("parallel", "parallel")),
        name="attn_prompt",
    )(sinks, q, k, k, v, v)


def _attn_sample_kernel(q_ref, kc_ref, vc_ref, kn_ref, vn_ref, sink_ref, o_ref, *, dec_seq):
    ns, kvh, nq, hd = q_ref.shape
    nc = kc_ref.shape[2]
    b = ns * kvh
    q = q_ref[...].reshape(b, nq, hd)
    kc = kc_ref[...].reshape(b, nc, hd).astype(BF16)
    vc = vc_ref[...].reshape(b, nc, hd).astype(BF16)
    qf = q.astype(F32)
    kn = kn_ref[...].reshape(b, dec_seq, hd).astype(BF16).astype(F32)
    vn = vn_ref[...].reshape(b, dec_seq, hd).astype(BF16).astype(F32)
    t_c = lax.broadcasted_iota(jnp.int32, (nq, nc), 0) % dec_seq
    c = lax.broadcasted_iota(jnp.int32, (nq, nc), 1)
    t_n = lax.broadcasted_iota(jnp.int32, (nq, 1), 0) % dec_seq
    s_c = jnp.einsum('bqd,bkd->bqk', q, kc, preferred_element_type=F32)
    s_c = jnp.where((c >= t_c)[None], s_c, -jnp.inf).reshape(ns, kvh, nq, nc)
    s_n = [jnp.where((t_n >= n)[None], jnp.sum(qf * kn[:, n:n + 1, :], axis=-1, keepdims=True), -jnp.inf)
           .reshape(ns, kvh, nq, 1) for n in range(dec_seq)]
    sink = sink_ref[...][None]
    m = jnp.maximum(jnp.max(s_c, axis=-1, keepdims=True), sink)
    for s in s_n:
        m = jnp.maximum(m, s)
    e_c = jnp.exp(s_c - m)
    e_n = [jnp.exp(s - m) for s in s_n]
    denom = jnp.sum(e_c, axis=-1, keepdims=True) + jnp.exp(sink - m)
    for e in e_n:
        denom = denom + e
    p_c = (e_c / denom).reshape(b, nq, nc).astype(BF16)
    o = jnp.einsum('bqk,bkd->bqd', p_c, vc, preferred_element_type=F32)
    for n in range(dec_seq):
        p_n = (e_n[n] / denom).reshape(b, nq, 1).astype(BF16).astype(F32)
        o = o + p_n * vn[:, n:n + 1, :]
    o_ref[...] = o.reshape(ns, kvh, nq, hd).astype(BF16)


def _attn_sample(q, cache_k_t, cache_v_t, j, k_new, v_new, sinks, *, dec_seq):
    n, kvh, nq, hd = q.shape
    nc = cache_k_t.shape[3]
    ns = 8 if n % 8 == 0 else n
    sink_rows = jnp.repeat(sinks.reshape(N_KV_HEADS, GROUP), dec_seq, axis=1)[..., None]
    blk4 = lambda i: (i, 0, 0, 0)
    cache_spec = pl.BlockSpec((None, ns, kvh, nc, hd), lambda i: (j, i, 0, 0, 0))
    return pl.pallas_call(
        functools.partial(_attn_sample_kernel, dec_seq=dec_seq),
        grid=(n // ns,),
        in_specs=[pl.BlockSpec((ns, kvh, nq, hd), blk4), cache_spec, cache_spec,
                  pl.BlockSpec((ns, kvh, dec_seq, hd), blk4), pl.BlockSpec((ns, kvh, dec_seq, hd), blk4),
                  pl.BlockSpec((kvh, nq, 1), lambda i: (0, 0, 0))],
        out_specs=pl.BlockSpec((ns, kvh, nq, hd), blk4),
        out_shape=jax.ShapeDtypeStruct((n, kvh, nq, hd), BF16),
        compiler_params=_params(("parallel",)),
        name="attn_sample",
    )(q, cache_k_t, cache_v_t, k_new, v_new, sink_rows)


def _proj_router_kernel(x_ref, zp_ref, zs_ref, w_ref, b_ref, g_ref, wrh_ref, wrl_ref, br_ref,
                        x1_ref, h_ref, meta_ref, cnt_ref, carry_ref, *, prompt_tiles):
    i = pl.program_id(0)
    tm = x_ref.shape[0]

    @pl.when(i == 0)
    def _():
        carry_ref[...] = jnp.zeros_like(carry_ref)

    z = jnp.where(i < prompt_tiles, zp_ref[...], zs_ref[...])
    x1 = x_ref[...] + (jnp.dot(z, w_ref[...], preferred_element_type=F32) + b_ref[...])
    x1_ref[...] = x1
    h = _rms(x1, g_ref[...])
    _to_token_major(h_ref, h)
    h_hi = h.astype(BF16)
    h_lo = (h - h_hi.astype(F32)).astype(BF16)
    logits = (jnp.dot(h_hi, wrh_ref[...], preferred_element_type=F32)
              + (jnp.dot(h_hi, wrl_ref[...], preferred_element_type=F32)
                 + jnp.dot(h_lo, wrh_ref[...], preferred_element_type=F32))) + br_ref[...]

    lane = lax.broadcasted_iota(jnp.int32, (tm, LANES), 1).astype(F32)
    first_lane = lambda hit: jnp.min(jnp.where(hit, lane, float(LANES)), axis=-1, keepdims=True)
    lg = jnp.where(lane < N_GROUPS, logits, -jnp.inf)
    mg = jnp.max(lg, axis=-1, keepdims=True)
    gsel = first_lane(lg == mg)
    gprob = 1.0 / jnp.sum(jnp.exp(lg - mg), axis=-1, keepdims=True)
    lo = ROUTER_LANE0 + gsel * EXPERTS_PER_GROUP
    in_group = (lane >= lo) & (lane < lo + EXPERTS_PER_GROUP)
    le = jnp.where(in_group, logits, -jnp.inf)
    ee = jnp.exp(le - jnp.max(le, axis=-1, keepdims=True))
    pe = jnp.where(in_group, ee / jnp.sum(ee, axis=-1, keepdims=True), -1.0)
    p1 = jnp.max(pe, axis=-1, keepdims=True)
    i1 = first_lane(pe == p1)
    pe2 = jnp.where(lane == i1, -1.0, pe)
    p2 = jnp.max(pe2, axis=-1, keepdims=True)
    i2 = first_lane(pe2 == p2)
    denom = p1 + p2
    g1 = gprob * p1 / denom
    g2 = gprob * p2 / denom
    sel1 = lane == i1
    sel2 = lane == i2
    onehot = jnp.where(sel1 | sel2, 1.0, 0.0)
    r = lax.broadcasted_iota(jnp.int32, (tm, tm), 0)
    c = lax.broadcasted_iota(jnp.int32, (tm, tm), 1)
    before = jnp.where(c < r, 1.0, 0.0).astype(BF16)
    rank = jnp.dot(before, onehot.astype(BF16), preferred_element_type=F32) + carry_ref[...]
    r1 = jnp.sum(jnp.where(sel1, rank, 0.0), axis=-1, keepdims=True)
    r2 = jnp.sum(jnp.where(sel2, rank, 0.0), axis=-1, keepdims=True)
    carry = carry_ref[...] + jnp.sum(onehot, axis=0, keepdims=True)
    carry_ref[...] = carry
    cnt_ref[...] = carry
    e1 = i1 - ROUTER_LANE0
    e2 = i2 - ROUTER_LANE0
    meta = jnp.zeros((tm, LANES), F32)
    for col, val in enumerate((e1, e2, r1, r2, g1, g2)):
        meta = jnp.where(lane == col, val, meta)
    meta_ref[...] = meta


def _proj_router(x, z_p, z_s, w, b, g, w_rg, b_rg, w_re, b_re):
    t = x.shape[0]
    tp, ts = z_p.shape[0], z_s.shape[0]
    tm = _token_tile(tp, ts)
    npt = tp // tm
    pad = LANES - N_GROUPS - N_EXPERTS
    wr = jnp.concatenate([w_rg, w_re, jnp.zeros((D_MODEL, pad), F32)], axis=1)
    wr_hi = wr.astype(BF16)
    wr_lo = (wr - wr_hi.astype(F32)).astype(BF16)
    br = jnp.concatenate([b_rg, b_re, jnp.zeros((pad,), F32)]).reshape(1, LANES)
    row = lambda i: (i, 0)
    const = lambda i: (0, 0)
    return pl.pallas_call(
        functools.partial(_proj_router_kernel, prompt_tiles=npt),
        grid=(t // tm,),
        in_specs=[pl.BlockSpec((tm, D_MODEL), row),
                  pl.BlockSpec((tm, D_MODEL), lambda i: (jnp.minimum(i, npt - 1), 0)),
                  pl.BlockSpec((tm, D_MODEL), lambda i: (jnp.maximum(i - npt, 0), 0)),
                  pl.BlockSpec((D_MODEL, D_MODEL), const), pl.BlockSpec((1, D_MODEL), const),
                  pl.BlockSpec((1, D_MODEL), const), pl.BlockSpec((D_MODEL, LANES), const),
                  pl.BlockSpec((D_MODEL, LANES), const), pl.BlockSpec((1, LANES), const)],
        out_specs=[pl.BlockSpec((tm, D_MODEL), row), pl.BlockSpec((tm * ROW_TILES, LANES), row),
                   pl.BlockSpec((tm, LANES), row), pl.BlockSpec((1, LANES), const)],
        out_shape=[jax.ShapeDtypeStruct((t, D_MODEL), F32), jax.ShapeDtypeStruct((t * ROW_TILES, LANES), F32),
                   jax.ShapeDtypeStruct((t, LANES), F32), jax.ShapeDtypeStruct((1, LANES), F32)],
        scratch_shapes=[pltpu.VMEM((1, LANES), F32)],
        compiler_params=_params(("arbitrary",)),
        name="proj_router",
    )(x, z_p, z_s, w.astype(BF16), b.reshape(1, D_MODEL), g.reshape(1, D_MODEL), wr_hi, wr_lo, br)


def _dispatch_plan(meta, counts, t):
    n = t * TOP_K
    n_blocks = -(-(n + N_EXPERTS * (EXPERT_BLOCK - 1)) // EXPERT_BLOCK)
    eid = meta[:, 0:TOP_K].astype(jnp.int32)
    rank = meta[:, TOP_K:2 * TOP_K].astype(jnp.int32)
    cnt = counts[0, ROUTER_LANE0:ROUTER_LANE0 + N_EXPERTS].astype(jnp.int32)
    padded = (cnt + EXPERT_BLOCK - 1) // EXPERT_BLOCK * EXPERT_BLOCK
    pad_end = jnp.cumsum(padded)
    pad_start = pad_end - padded
    experts = jnp.arange(N_EXPERTS, dtype=jnp.int32)
    start_of = jnp.sum(jnp.where(eid[..., None] == experts, pad_start, 0), axis=-1)
    dest = (start_of + rank).reshape(n)
    block_row0 = jnp.arange(n_blocks, dtype=jnp.int32) * EXPERT_BLOCK
    block_e = jnp.minimum(jnp.sum((pad_end[None, :] <= block_row0[:, None]).astype(jnp.int32), axis=1),
                          N_EXPERTS - 1)
    nused = (pad_end[-1] // EXPERT_BLOCK).reshape(1)
    zstart = jnp.where(padded > cnt, pad_end - EXPERT_BLOCK, -1)
    return dest, block_e, nused, zstart, n_blocks


def _dispatch_kernel(dest_ref, zstart_ref, nused_ref, h_ref, xs_hbm, sbuf, zbuf, zsem, dsem, *, n_blocks):
    i = pl.program_id(0)
    tm = h_ref.shape[0] // ROW_TILES

    @pl.when(i == 0)
    def _():
        zbuf[...] = jnp.zeros_like(zbuf)
        nused = nused_ref[0]

        def zero_block(row0):
            rows = pl.ds(pl.multiple_of(row0 * ROW_TILES, EXPERT_BLOCK * ROW_TILES), EXPERT_BLOCK * ROW_TILES)
            return pltpu.make_async_copy(zbuf, xs_hbm.at[rows, :], zsem.at[0])

        def per_partial_block(fn):
            def body(e, carry):
                @pl.when(zstart_ref[e] >= 0)
                def _():
                    fn(pl.multiple_of(zstart_ref[e], EXPERT_BLOCK))
                return carry

            lax.fori_loop(0, N_EXPERTS, body, 0)

        def per_unused_block(fn):
            def body(j, carry):
                fn(pl.multiple_of(j * EXPERT_BLOCK, EXPERT_BLOCK))
                return carry

            lax.fori_loop(nused, n_blocks, body, 0)

        per_partial_block(lambda row0: zero_block(row0).start())
        per_unused_block(lambda row0: zero_block(row0).start())
        per_partial_block(lambda row0: zero_block(row0).wait())
        per_unused_block(lambda row0: zero_block(row0).wait())

    slot = i % 2
    sbuf[slot] = h_ref[...]

    in_flight = min(DISPATCH_LAG, tm * TOP_K // WAIT_CHUNK)

    @pl.when(i > 0)
    def _():
        _wait_row_copies(sbuf.at[1 - slot], xs_hbm, dsem.at[1 - slot], in_flight * WAIT_CHUNK)

    def start_token(j):
        tok = i * tm + j
        for k in range(TOP_K):
            _row_copy(sbuf.at[slot], j, xs_hbm, dest_ref[tok * TOP_K + k], dsem.at[slot]).start()

    _issue_rows_rolling(tm, start_token,
                        lambda: _wait_row_copies(sbuf.at[slot], xs_hbm, dsem.at[slot], WAIT_CHUNK), DISPATCH_LAG)

    @pl.when(i == pl.num_programs(0) - 1)
    def _():
        _wait_row_copies(sbuf.at[slot], xs_hbm, dsem.at[slot], in_flight * WAIT_CHUNK)


def _dispatch(h, dest, zstart, nused, n_blocks):
    t = h.shape[0] // ROW_TILES
    tm = _token_tile(t, largest=COMBINE_TILE)
    return pl.pallas_call(
        functools.partial(_dispatch_kernel, n_blocks=n_blocks),
        grid_spec=pltpu.PrefetchScalarGridSpec(
            num_scalar_prefetch=3, grid=(t // tm,),
            in_specs=[pl.BlockSpec((tm * ROW_TILES, LANES), lambda i, *_: (i, 0))],
            out_specs=pl.BlockSpec(memory_space=pl.ANY),
            scratch_shapes=[pltpu.VMEM((2, tm * ROW_TILES, LANES), F32),
                            pltpu.VMEM((EXPERT_BLOCK * ROW_TILES, LANES), F32),
                            pltpu.SemaphoreType.DMA((1,)), pltpu.SemaphoreType.DMA((2,))]),
        out_shape=jax.ShapeDtypeStruct((n_blocks * EXPERT_BLOCK * ROW_TILES, LANES), F32),
        compiler_params=_params(("arbitrary",)),
        name="dispatch",
    )(dest, zstart, nused, h)


def _experts_kernel(be_ref, nused_ref, xs_ref, wg_ref, wu_ref, wd_ref, ys_ref, wg_bf, wu_bf, wd_bf):
    i = pl.program_id(0)
    nused = nused_ref[0]

    @pl.when(i < nused)
    def _():
        new_expert = (i == 0) | (be_ref[i] != be_ref[jnp.maximum(i - 1, 0)])

        @pl.when(new_expert)
        def _():
            wg_bf[...] = wg_ref[0].astype(BF16)
            wu_bf[...] = wu_ref[0].astype(BF16)
            wd_bf[...] = wd_ref[0].astype(BF16)

        x = _from_token_major(xs_ref, EXPERT_BLOCK).astype(BF16)
        a = jnp.dot(x, wg_bf[...], preferred_element_type=F32)
        u = jnp.dot(x, wu_bf[...], preferred_element_type=F32)
        act = (jax.nn.silu(a) * u).astype(BF16)
        _to_token_major(ys_ref, jnp.dot(act, wd_bf[...], preferred_element_type=F32))

    @pl.when(i >= nused)
    def _():
        ys_ref[...] = jnp.zeros_like(ys_ref)


def _experts(xs, block_e, nused, w_gate, w_up, w_down, layer):
    n_blocks = block_e.shape[0]
    wsel = lambda i, be, nu: (layer, be[i], 0, 0)
    return pl.pallas_call(
        _experts_kernel,
        grid_spec=pltpu.PrefetchScalarGridSpec(
            num_scalar_prefetch=2, grid=(n_blocks,),
            in_specs=[pl.BlockSpec((EXPERT_BLOCK * ROW_TILES, LANES),
                                   lambda i, be, nu: (jnp.minimum(i, nu[0] - 1), 0)),
                      pl.BlockSpec((None, 1, D_MODEL, D_EXPERT), wsel),
                      pl.BlockSpec((None, 1, D_MODEL, D_EXPERT), wsel),
                      pl.BlockSpec((None, 1, D_EXPERT, D_MODEL), wsel)],
            out_specs=pl.BlockSpec((EXPERT_BLOCK * ROW_TILES, LANES), lambda i, be, nu: (i, 0)),
            scratch_shapes=[pltpu.VMEM((D_MODEL, D_EXPERT), BF16),
                            pltpu.VMEM((D_MODEL, D_EXPERT), BF16),
                            pltpu.VMEM((D_EXPERT, D_MODEL), BF16)]),
        out_shape=jax.ShapeDtypeStruct((n_blocks * EXPERT_BLOCK * ROW_TILES, LANES), F32),
        compiler_params=_params(("arbitrary",)),
        name="experts",
    )(block_e, nused, xs, w_gate, w_up, w_down)


def _final_kernel(dest_ref, x_ref, meta_ref, ys_hbm, g_ref, yp_ref, ys_ref, mbuf, sem, *, prompt_tiles):
    i = pl.program_id(0)
    y = _rms(_moe_combine(dest_ref, x_ref, meta_ref, ys_hbm, mbuf, sem), g_ref[...])

    @pl.when(i < prompt_tiles)
    def _():
        yp_ref[...] = y

    @pl.when(i >= prompt_tiles)
    def _():
        ys_ref[...] = y


def _final_norm(x, moe, g, *, tp, tm):
    t = x.shape[0]
    dest, meta, ys = moe
    npt = tp // tm
    c_specs, c_scratch = _combine_specs(tm)
    return pl.pallas_call(
        functools.partial(_final_kernel, prompt_tiles=npt),
        grid_spec=pltpu.PrefetchScalarGridSpec(
            num_scalar_prefetch=1, grid=(t // tm,),
            in_specs=c_specs + [pl.BlockSpec((1, D_MODEL), lambda i, *_: (0, 0))],
            out_specs=[pl.BlockSpec((tm, D_MODEL), lambda i, *_: (jnp.minimum(i, npt - 1), 0)),
                       pl.BlockSpec((tm, D_MODEL), lambda i, *_: (jnp.maximum(i - npt, 0), 0))],
            scratch_shapes=c_scratch),
        out_shape=[jax.ShapeDtypeStruct((tp, D_MODEL), F32), jax.ShapeDtypeStruct((t - tp, D_MODEL), F32)],
        compiler_params=_params(("arbitrary",)),
        name="final_norm",
    )(dest, x, meta, ys, g.reshape(1, D_MODEL))


def kernel(x_prompt, x_sample, cache_k, cache_v, state_conv, norm_mix, norm_ffn, norm_final, conv_w_pw1, conv_b_pw1, conv_w_dw, conv_b_dw, conv_ln_g, conv_ln_b, conv_w_pw2, conv_b_pw2, attn_w_qkv, attn_b_qkv, attn_sinks, attn_w_o, attn_b_o, moe_w_router_group, moe_b_router_group, moe_w_router_expert, moe_b_router_expert, moe_w_gate, moe_w_up, moe_w_down):
    batch, seq, _ = x_prompt.shape
    n_dec, dec_seq, _ = x_sample.shape
    depth = norm_mix.shape[0]
    tp, ts = batch * seq, n_dec * dec_seq
    t = tp + ts
    tm = _token_tile(tp, ts, largest=COMBINE_TILE)

    assert dec_seq <= CONV_WIDTH - 1 and dec_seq <= WINDOW

    def prompt_tail(a, n_rows):
        return jnp.stack([a[(b + 1) * seq - n_rows:(b + 1) * seq] for b in range(batch)])

    cache_k_t = cache_k.transpose(0, 1, 3, 2, 4)
    cache_v_t = cache_v.transpose(0, 1, 3, 2, 4)
    state_t = state_conv.transpose(0, 2, 1, 3)

    x = (x_prompt.reshape(tp, D_MODEL), x_sample.reshape(ts, D_MODEL))
    moe = None
    kp_list, vp_list, cp_list, ks_list, vs_list, cs_list = [], [], [], [], [], []
    for layer in range(depth):
        j = layer // 2
        if layer % 2 == 0:
            x, u = _mixer_in(x, moe, norm_mix[layer], conv_w_pw1[j], conv_b_pw1[j], is_conv=True, tm=tm)
            u_s = u[tp:].reshape(n_dec, dec_seq, D_MODEL)
            cp_list.append(prompt_tail(u, CONV_WIDTH - 1))
            cs_list.append(u_s)
            z_p = _conv_prompt(u, conv_w_dw[j], conv_b_dw[j], conv_ln_g[j], conv_ln_b[j], batch=batch, seq=seq)
            z_s = _conv_sample(state_t, j, jnp.swapaxes(u_s, 0, 1), conv_w_dw[j], conv_b_dw[j],
                               conv_ln_g[j], conv_ln_b[j])
            z_s = jnp.swapaxes(z_s, 0, 1).reshape(ts, D_MODEL)
            w_out, b_out = conv_w_pw2[j], conv_b_pw2[j]
        else:
            x, q, k, v = _mixer_in(x, moe, norm_mix[layer], attn_w_qkv[j], attn_b_qkv[j], is_conv=False, tm=tm)
            kp_list.append(prompt_tail(k, WINDOW).reshape(batch, WINDOW, N_KV_HEADS, HEAD_DIM))
            vp_list.append(prompt_tail(v, WINDOW).reshape(batch, WINDOW, N_KV_HEADS, HEAD_DIM))
            k_s = k[tp:].reshape(n_dec, dec_seq, N_KV_HEADS, HEAD_DIM)
            v_s = v[tp:].reshape(n_dec, dec_seq, N_KV_HEADS, HEAD_DIM)
            ks_list.append(k_s)
            vs_list.append(v_s)
            z_p = _attn_prompt(q, k, v, attn_sinks[j], batch=batch, seq=seq)
            q_s = q[tp:].reshape(n_dec, dec_seq, N_KV_HEADS, GROUP, HEAD_DIM).transpose(0, 2, 3, 1, 4)
            q_s = q_s.reshape(n_dec, N_KV_HEADS, GROUP * dec_seq, HEAD_DIM)
            o_s = _attn_sample(q_s, cache_k_t, cache_v_t, j, k_s.transpose(0, 2, 1, 3), v_s.transpose(0, 2, 1, 3),
                               attn_sinks[j], dec_seq=dec_seq)
            o_s = o_s.reshape(n_dec, N_KV_HEADS, GROUP, dec_seq, HEAD_DIM).transpose(0, 3, 1, 2, 4)
            z_s = o_s.reshape(ts, D_MODEL)
            w_out, b_out = attn_w_o[j], attn_b_o[j]
        x, h, meta, counts = _proj_router(x, z_p, z_s, w_out, b_out, norm_ffn[layer],
                                          moe_w_router_group[layer], moe_b_router_group[layer],
                                          moe_w_router_expert[layer], moe_b_router_expert[layer])
        dest, block_e, nused, zstart, n_blocks = _dispatch_plan(meta, counts, t)
        xs = _dispatch(h, dest, zstart, nused, n_blocks)
        ys = _experts(xs, block_e, nused, moe_w_gate, moe_w_up, moe_w_down, layer)
        moe = (dest, meta, ys)
    y_p, y_s = _final_norm(x, moe, norm_final, tp=tp, tm=tm)
    new_k_s = jnp.concatenate([cache_k[:, :, dec_seq:], jnp.stack(ks_list)], axis=2)
    new_v_s = jnp.concatenate([cache_v[:, :, dec_seq:], jnp.stack(vs_list)], axis=2)
    new_c_s = jnp.concatenate([state_conv[:, :, dec_seq:], jnp.stack(cs_list)], axis=2)
    return (y_p.reshape(batch, seq, D_MODEL), y_s.reshape(n_dec, dec_seq, D_MODEL),
            jnp.stack(kp_list), jnp.stack(vp_list), jnp.stack(cp_list), new_k_s, new_v_s, new_c_s)
```

```python
import functools

import jax
import jax.numpy as jnp
from jax import lax
from jax.experimental import pallas as pl
from jax.experimental.pallas import tpu as pltpu

F32 = jnp.float32
BF16 = jnp.bfloat16

D_MODEL = 1024
CONV_WIDTH = 31
N_HEADS = 16
N_KV_HEADS = 4
HEAD_DIM = D_MODEL // N_HEADS
GROUP = N_HEADS // N_KV_HEADS
KV_DIM = N_KV_HEADS * HEAD_DIM
WINDOW = 128
N_GROUPS = 8
EXPERTS_PER_GROUP = 8
N_EXPERTS = N_GROUPS * EXPERTS_PER_GROUP
TOP_K = 2
D_EXPERT = D_MODEL // 2
EXPERT_BLOCK = 256
RMS_EPS = 1e-6
LN_EPS = 1e-5

LANES = 128
SUBLANES = 8
ROW_TILES = D_MODEL // LANES
assert ROW_TILES == SUBLANES
ROUTER_LANE0 = N_GROUPS
META_GATE_LANE = 2 * TOP_K
CONV_HALO = 32
CONV_ROWS = 32
WAIT_CHUNK = 128
ISSUE_UNROLL = 8
DISPATCH_LAG = 2
COMBINE_LAG = 5
COMBINE_TILE = 512
VMEM_LIMIT =56 * 1024 * 1024


def _params(sem):
    return pltpu.CompilerParams(dimension_semantics=sem, vmem_limit_bytes=VMEM_LIMIT)


def _rms(x, g):
    return x * lax.rsqrt(jnp.mean(x * x, axis=-1, keepdims=True) + RMS_EPS) * g


def _token_tile(*counts, largest=512):
    for tm in (512, 256, 128):
        if tm <= largest and all(c % tm == 0 for c in counts):
            return tm
    raise ValueError("token counts must be multiples of 128")


def _to_token_major(ref, x):
    n = x.shape[0]
    for s in range(ROW_TILES):
        ref[pl.ds(s, n, stride=ROW_TILES), :] = x[:, s * LANES:(s + 1) * LANES]


def _from_token_major(ref, n):
    return jnp.concatenate([ref[pl.ds(s, n, stride=ROW_TILES), :] for s in range(ROW_TILES)], axis=1)


def _row_slice(row):
    start = row * ROW_TILES
    return pl.ds(start if isinstance(start, int) else pl.multiple_of(start, ROW_TILES), ROW_TILES)


def _row_copy(src, src_row, dst, dst_row, sem):
    return pltpu.make_async_copy(src.at[_row_slice(src_row), :], dst.at[_row_slice(dst_row), :], sem)


def _wait_row_copies(src_hbm, dst, sem, n_rows):
    def chunk(c, carry):
        for _ in range(WAIT_CHUNK):
            _row_copy(src_hbm, 0, dst, 0, sem).wait()
        return carry

    lax.fori_loop(0, n_rows // WAIT_CHUNK, chunk, 0)


def _issue_rows_rolling(n_tokens, start_token, wait_chunk, lag):
    chunk_tokens = WAIT_CHUNK // TOP_K
    n_chunks = n_tokens // chunk_tokens

    def chunk(c, carry):
        def body(j, inner):
            start_token(c * chunk_tokens + j)
            return inner

        lax.fori_loop(0, chunk_tokens, body, 0, unroll=ISSUE_UNROLL)

        @pl.when(c >= lag)
        def _():
            wait_chunk()

        return carry

    lax.fori_loop(0, n_chunks, chunk, 0)
    return min(lag, n_chunks)


def _moe_combine(dest_ref, x_ref, meta_ref, ys_hbm, mbuf, sem):
    i = pl.program_id(0)
    tm = x_ref.shape[0]

    def issue(tile, slot):
        def start_token(j):
            tok = tile * tm + j
            for k in range(TOP_K):
                _row_copy(ys_hbm, dest_ref[tok * TOP_K + k], mbuf.at[slot, k], j, sem.at[slot]).start()

        return _issue_rows_rolling(
            tm, start_token, lambda: _wait_row_copies(ys_hbm, mbuf.at[slot, 0], sem.at[slot], WAIT_CHUNK),
            COMBINE_LAG)

    in_flight = min(COMBINE_LAG, tm * TOP_K // WAIT_CHUNK)

    @pl.when(i == 0)
    def _():
        issue(0, 0)

    slot = i % 2
    _wait_row_copies(ys_hbm, mbuf.at[slot, 0], sem.at[slot], in_flight * WAIT_CHUNK)

    @pl.when(i + 1 < pl.num_programs(0))
    def _():
        issue(i + 1, (i + 1) % 2)

    moe = meta_ref[:, META_GATE_LANE:META_GATE_LANE + 1] * _from_token_major(mbuf.at[slot, 0], tm)
    for k in range(1, TOP_K):
        moe = moe + (meta_ref[:, META_GATE_LANE + k:META_GATE_LANE + k + 1]
                     * _from_token_major(mbuf.at[slot, k], tm))
    return x_ref[...] + moe


def _combine_specs(tm):
    in_specs = [pl.BlockSpec((tm, D_MODEL), lambda i, *_: (i, 0)),
                pl.BlockSpec((tm, LANES), lambda i, *_: (i, 0)),
                pl.BlockSpec(memory_space=pl.ANY)]
    scratch = [pltpu.VMEM((2, TOP_K, tm * ROW_TILES, LANES), F32), pltpu.SemaphoreType.DMA((2,))]
    return in_specs, scratch


def _mixer_proj(x, g_ref, w_ref, b_ref, outs, is_conv):
    h = _rms(x, g_ref[...]).astype(BF16)
    y = jnp.dot(h, w_ref[...], preferred_element_type=F32) + b_ref[...]
    if is_conv:
        (u_ref,) = outs
        u_ref[...] = y[:, :D_MODEL] * jax.nn.sigmoid(y[:, D_MODEL:])
    else:
        q_ref, k_ref, v_ref = outs
        q_ref[...] = (y[:, :D_MODEL] * (HEAD_DIM ** -0.5)).astype(BF16)
        k_ref[...] = y[:, D_MODEL:D_MODEL + KV_DIM]
        v_ref[...] = y[:, D_MODEL + KV_DIM:]


def _mixer_in_first_kernel(xp_ref, xs_ref, g_ref, w_ref, b_ref, xo_ref, *outs, is_conv, prompt_tiles):
    x = jnp.where(pl.program_id(0) < prompt_tiles, xp_ref[...], xs_ref[...])
    xo_ref[...] = x
    _mixer_proj(x, g_ref, w_ref, b_ref, outs, is_conv)


def _mixer_in_kernel(dest_ref, x_ref, meta_ref, ys_hbm, g_ref, w_ref, b_ref, xo_ref, *rest, is_conv):
    *outs, mbuf, sem = rest
    x = _moe_combine(dest_ref, x_ref, meta_ref, ys_hbm, mbuf, sem)
    xo_ref[...] = x
    _mixer_proj(x, g_ref, w_ref, b_ref, outs, is_conv)


def _mixer_in(x, moe, g, w, b, *, is_conv, tm):
    t = x.shape[0] if moe is not None else x[0].shape[0] + x[1].shape[0]
    n_out = w.shape[1]
    row = lambda i, *_: (i, 0)
    const = lambda i, *_: (0, 0)
    w_specs = [pl.BlockSpec((1, D_MODEL), const), pl.BlockSpec((D_MODEL, n_out), const),
               pl.BlockSpec((1, n_out), const)]
    w_args = [g.reshape(1, D_MODEL), w.astype(BF16), b.reshape(1, n_out)]
    if is_conv:
        out_shape = [jax.ShapeDtypeStruct((t, D_MODEL), F32)]
        out_specs = [pl.BlockSpec((tm, D_MODEL), row)]
    else:
        out_shape = [jax.ShapeDtypeStruct((t, D_MODEL), BF16), jax.ShapeDtypeStruct((t, KV_DIM), F32),
                     jax.ShapeDtypeStruct((t, KV_DIM), F32)]
        out_specs = [pl.BlockSpec((tm, D_MODEL), row), pl.BlockSpec((tm, KV_DIM), row),
                     pl.BlockSpec((tm, KV_DIM), row)]
    name = "mixer_in_conv" if is_conv else "mixer_in_attn"
    if moe is None:
        x_p, x_s = x
        npt = x_p.shape[0] // tm
        return pl.pallas_call(
            functools.partial(_mixer_in_first_kernel, is_conv=is_conv, prompt_tiles=npt),
            grid=(t // tm,),
            in_specs=[pl.BlockSpec((tm, D_MODEL), lambda i: (jnp.minimum(i, npt - 1), 0)),
                      pl.BlockSpec((tm, D_MODEL), lambda i: (jnp.maximum(i - npt, 0), 0))] + w_specs,
            out_specs=[pl.BlockSpec((tm, D_MODEL), row)] + out_specs,
            out_shape=[jax.ShapeDtypeStruct((t, D_MODEL), F32)] + out_shape,
            compiler_params=_params(("parallel",)), name=name + "_first",
        )(x_p, x_s, *w_args)
    dest, meta, ys = moe
    c_specs, c_scratch = _combine_specs(tm)
    return pl.pallas_call(
        functools.partial(_mixer_in_kernel, is_conv=is_conv),
        grid_spec=pltpu.PrefetchScalarGridSpec(
            num_scalar_prefetch=1, grid=(t // tm,),
            in_specs=c_specs + w_specs,
            out_specs=[pl.BlockSpec((tm, D_MODEL), row)] + out_specs,
            scratch_shapes=c_scratch),
        out_shape=[jax.ShapeDtypeStruct((t, D_MODEL), F32)] + out_shape,
        compiler_params=_params(("arbitrary",)), name=name,
    )(dest, x, meta, ys, *w_args)


def _ln_silu(c, g, b):
    mu = jnp.mean(c, axis=-1, keepdims=True)
    d = c - mu
    var = jnp.mean(d * d, axis=-1, keepdims=True)
    return jax.nn.silu(d * lax.rsqrt(var + LN_EPS) * g + b)


def _conv_prompt_kernel(u_ref, w_ref, bdw_ref, lg_ref, lb_ref, z_ref, ubuf, shifted, *, ts):
    s = pl.program_id(1)
    nbuf = ts + CONV_HALO

    @pl.when(s == 0)
    def _():
        ubuf[0:CONV_HALO, :] = jnp.zeros((CONV_HALO, D_MODEL), F32)

    ubuf[CONV_HALO:nbuf, :] = u_ref[...]
    ubuf[nbuf:nbuf + SUBLANES, :] = jnp.zeros((SUBLANES, D_MODEL), F32)

    def shift_chunk(c, carry):
        r0 = pl.multiple_of(c * CONV_ROWS, CONV_ROWS)
        win = ubuf[pl.ds(r0, CONV_ROWS + SUBLANES), :]
        for b in range(1, SUBLANES):
            rolled = pltpu.roll(win, CONV_ROWS + SUBLANES - b, axis=0)
            shifted[b - 1, pl.ds(r0, CONV_ROWS), :] = rolled[:CONV_ROWS]
        return carry

    lax.fori_loop(0, nbuf // CONV_ROWS, shift_chunk, 0)
    first = CONV_HALO - (CONV_WIDTH - 1)

    def chunk(c, carry):
        base = pl.multiple_of(c * CONV_ROWS, CONV_ROWS)
        groups = CONV_ROWS // SUBLANES
        accs = [jnp.zeros((SUBLANES, D_MODEL), F32) for _ in range(groups)]
        for k in range(CONV_WIDTH):
            a, b = divmod(first + k, SUBLANES)
            wk = w_ref[k]
            for g in range(groups):
                rows = pl.ds(base + (a + g) * SUBLANES, SUBLANES)
                tap = ubuf[rows, :] if b == 0 else shifted[b - 1, rows, :]
                accs[g] = accs[g] + wk * tap
        acc = jnp.concatenate(accs, axis=0)
        z = _ln_silu(acc + bdw_ref[...], lg_ref[...], lb_ref[...])
        z_ref[pl.ds(base, CONV_ROWS), :] = z.astype(BF16)
        return carry

    lax.fori_loop(0, ts // CONV_ROWS, chunk, 0)
    ubuf[0:CONV_HALO, :] = ubuf[ts:nbuf, :]


def _conv_prompt(u, w_dw, b_dw, ln_g, ln_b, *, batch, seq):
    ts = 512 if seq % 512 == 0 else seq
    ns = seq // ts
    const = lambda b, s: (0, 0)
    return pl.pallas_call(
        functools.partial(_conv_prompt_kernel, ts=ts),
        grid=(batch, ns),
        in_specs=[pl.BlockSpec((ts, D_MODEL), lambda b, s: (b * ns + s, 0)),
                  pl.BlockSpec((CONV_WIDTH, SUBLANES, D_MODEL), lambda b, s: (0, 0, 0)),
                  pl.BlockSpec((1, D_MODEL), const), pl.BlockSpec((1, D_MODEL), const),
                  pl.BlockSpec((1, D_MODEL), const)],
        out_specs=pl.BlockSpec((ts, D_MODEL), lambda b, s: (b * ns + s, 0)),
        out_shape=jax.ShapeDtypeStruct((batch * seq, D_MODEL), BF16),
        scratch_shapes=[pltpu.VMEM((ts + CONV_HALO + SUBLANES, D_MODEL), F32),
                        pltpu.VMEM((SUBLANES - 1, ts + CONV_HALO, D_MODEL), F32)],
        compiler_params=_params(("arbitrary", "arbitrary")),
        name="conv_prompt",
    )(u, jnp.broadcast_to(w_dw[:, None, :], (CONV_WIDTH, SUBLANES, D_MODEL)), b_dw.reshape(1, D_MODEL), ln_g.reshape(1, D_MODEL), ln_b.reshape(1, D_MODEL))


def _conv_sample_kernel(st_ref, us_ref, w_ref, bdw_ref, lg_ref, lb_ref, z_ref, *, dec_seq):
    n_past = st_ref.shape[0]
    for t in range(dec_seq):
        acc = jnp.zeros(us_ref.shape[1:], F32)
        for k in range(CONV_WIDTH):
            r = t + k
            acc = acc + w_ref[k:k + 1, :] * (st_ref[r] if r < n_past else us_ref[r - n_past])
        z_ref[t] = _ln_silu(acc + bdw_ref[...], lg_ref[...], lb_ref[...]).astype(BF16)


def _conv_sample(state_t, j, us_t, w_dw, b_dw, ln_g, ln_b):
    _, n_past, n, _ = state_t.shape
    dec_seq = us_t.shape[0]
    nb = 32 if n % 32 == 0 else n
    const = lambda i: (0, 0)
    return pl.pallas_call(
        functools.partial(_conv_sample_kernel, dec_seq=dec_seq),
        grid=(n // nb,),
        in_specs=[pl.BlockSpec((None, n_past, nb, D_MODEL), lambda i: (j, 0, i, 0)),
                  pl.BlockSpec((dec_seq, nb, D_MODEL), lambda i: (0, i, 0)),
                  pl.BlockSpec((CONV_WIDTH, D_MODEL), const),
                  pl.BlockSpec((1, D_MODEL), const), pl.BlockSpec((1, D_MODEL), const),
                  pl.BlockSpec((1, D_MODEL), const)],
        out_specs=pl.BlockSpec((dec_seq, nb, D_MODEL), lambda i: (0, i, 0)),
        out_shape=jax.ShapeDtypeStruct((dec_seq, n, D_MODEL), BF16),
        compiler_params=_params(("parallel",)),
        name="conv_sample",
    )(state_t, us_t, w_dw, b_dw.reshape(1, D_MODEL), ln_g.reshape(1, D_MODEL), ln_b.reshape(1, D_MODEL))


def _softmax_with_sink(s, allowed, sink):
    if allowed is not None:
        s = jnp.where(allowed, s, -jnp.inf)
    m =jnp.maximum(jnp.max(s, axis=-1, keepdims=True), sink)
    e = jnp.exp(s - m)
    return e / (jnp.sum(e, axis=-1, keepdims=True) + jnp.exp(sink - m))


def _attn_prompt_kernel(sink_ref, q_ref, kp_ref, kc_ref, vp_ref, vc_ref, o_ref, bias_ref):
    i = pl.program_id(1)
    blk = q_ref.shape[0]
    qi = lax.broadcasted_iota(jnp.int32, (blk, 2 * blk), 0)
    kc = lax.broadcasted_iota(jnp.int32, (blk, 2 * blk), 1)
    allowed = (kc >= qi) & (kc <= qi + WINDOW) & ((i > 0) | (kc >= blk))
    bias_ref[...] = jnp.where(allowed, 0.0, -jnp.inf)
    for hk in range(N_KV_HEADS):
        cols = slice(hk * HEAD_DIM, (hk + 1) * HEAD_DIM)
        kcat = jnp.concatenate([kp_ref[:, cols], kc_ref[:, cols]], axis=0).astype(BF16)
        vcat = jnp.concatenate([vp_ref[:, cols], vc_ref[:, cols]], axis=0).astype(BF16)
        for pair in range(GROUP // 2):
            outs = []
            for g in (2 * pair, 2 * pair + 1):
                h = hk * GROUP + g
                q = q_ref[:, h * HEAD_DIM:(h + 1) * HEAD_DIM]
                s = lax.dot_general(q, kcat, (((1,), (1,)), ((), ())), preferred_element_type=F32)
                p = _softmax_with_sink(s + bias_ref[...], None, sink_ref[h])
                outs.append(jnp.dot(p.astype(BF16), vcat, preferred_element_type=F32))
            lo = (hk * GROUP + 2 * pair) * HEAD_DIM
            o_ref[:, lo:lo + 2 * HEAD_DIM] = jnp.concatenate(outs, axis=-1).astype(BF16)


def _attn_prompt(q, k, v, sinks, *, batch, seq):
    blk = WINDOW
    nb = seq // blk
    cur = lambda b, i, *_: (b * nb + i, 0)
    prev = lambda b, i, *_: (b * nb + jnp.maximum(i - 1, 0), 0)
    return pl.pallas_call(
        _attn_prompt_kernel,
        grid_spec=pltpu.PrefetchScalarGridSpec(
            num_scalar_prefetch=1, grid=(batch, nb),
            in_specs=[pl.BlockSpec((blk, D_MODEL), cur),
                      pl.BlockSpec((blk, KV_DIM), prev), pl.BlockSpec((blk, KV_DIM), cur),
                      pl.BlockSpec((blk, KV_DIM), prev), pl.BlockSpec((blk, KV_DIM), cur)],
            out_specs=pl.BlockSpec((blk, D_MODEL), cur),
            scratch_shapes=[pltpu.VMEM((blk, 2 * blk), F32)]),
        out_shape=jax.ShapeDtypeStruct((batch * seq, D_MODEL), BF16),
        compiler_params=_params(("parallel", "parallel")),
        name="attn_prompt",
    )(sinks, q, k, k, v, v)


def _attn_sample_kernel(q_ref, kc_ref, vc_ref, kn_ref, vn_ref, sink_ref, o_ref, *, dec_seq):
    ns, kvh, nq, hd = q_ref.shape
    nc = kc_ref.shape[2]
    b = ns * kvh
    q = q_ref[...].reshape(b, nq, hd)
    kc = kc_ref[...].reshape(b, nc, hd).astype(BF16)
    vc = vc_ref[...].reshape(b, nc, hd).astype(BF16)
    qf = q.astype(F32)
    kn = kn_ref[...].reshape(b, dec_seq, hd).astype(BF16).astype(F32)
    vn = vn_ref[...].reshape(b, dec_seq, hd).astype(BF16).astype(F32)
    t_c = lax.broadcasted_iota(jnp.int32, (nq, nc), 0) % dec_seq
    c = lax.broadcasted_iota(jnp.int32, (nq, nc), 1)
    t_n = lax.broadcasted_iota(jnp.int32, (nq, 1), 0) % dec_seq
    s_c = jnp.einsum('bqd,bkd->bqk', q, kc, preferred_element_type=F32)
    s_c = jnp.where((c >= t_c)[None], s_c, -jnp.inf).reshape(ns, kvh, nq, nc)
    s_n = [jnp.where((t_n >= n)[None], jnp.sum(qf * kn[:, n:n + 1, :], axis=-1, keepdims=True), -jnp.inf)
           .reshape(ns, kvh, nq, 1) for n in range(dec_seq)]
    sink = sink_ref[...][None]
    m = jnp.maximum(jnp.max(s_c, axis=-1, keepdims=True), sink)
    for s in s_n:
        m = jnp.maximum(m, s)
    e_c = jnp.exp(s_c - m)
    e_n = [jnp.exp(s - m) for s in s_n]
    denom = jnp.sum(e_c, axis=-1, keepdims=True) + jnp.exp(sink - m)
    for e in e_n:
        denom = denom + e
    p_c = (e_c / denom).reshape(b, nq, nc).astype(BF16)
    o = jnp.einsum('bqk,bkd->bqd', p_c, vc, preferred_element_type=F32)
    for n in range(dec_seq):
        p_n = (e_n[n] / denom).reshape(b, nq, 1).astype(BF16).astype(F32)
        o = o + p_n * vn[:, n:n + 1, :]
    o_ref[...] = o.reshape(ns, kvh, nq, hd).astype(BF16)


def _attn_sample(q, cache_k_t, cache_v_t, j, k_new, v_new, sinks, *, dec_seq):
    n, kvh, nq, hd = q.shape
    nc = cache_k_t.shape[3]
    ns = 8 if n % 8 == 0 else n
    sink_rows = jnp.repeat(sinks.reshape(N_KV_HEADS, GROUP), dec_seq, axis=1)[..., None]
    blk4 = lambda i: (i, 0, 0, 0)
    cache_spec = pl.BlockSpec((None, ns, kvh, nc, hd), lambda i: (j, i, 0, 0, 0))
    return pl.pallas_call(
        functools.partial(_attn_sample_kernel, dec_seq=dec_seq),
        grid=(n // ns,),
        in_specs=[pl.BlockSpec((ns, kvh, nq, hd), blk4), cache_spec, cache_spec,
                  pl.BlockSpec((ns, kvh, dec_seq, hd), blk4), pl.BlockSpec((ns, kvh, dec_seq, hd), blk4),
                  pl.BlockSpec((kvh, nq, 1), lambda i: (0, 0, 0))],
        out_specs=pl.BlockSpec((ns, kvh, nq, hd), blk4),
        out_shape=jax.ShapeDtypeStruct((n, kvh, nq, hd), BF16),
        compiler_params=_params(("parallel",)),
        name="attn_sample",
    )(q, cache_k_t, cache_v_t, k_new, v_new, sink_rows)


def _proj_router_kernel(x_ref, zp_ref, zs_ref, w_ref, b_ref, g_ref, wrh_ref, wrl_ref, br_ref,
                        x1_ref, h_ref, meta_ref, cnt_ref, carry_ref, *, prompt_tiles):
    i = pl.program_id(0)
    tm = x_ref.shape[0]

    @pl.when(i == 0)
    def _():
        carry_ref[...] = jnp.zeros_like(carry_ref)

    z = jnp.where(i < prompt_tiles, zp_ref[...], zs_ref[...])
    x1 = x_ref[...] + (jnp.dot(z, w_ref[...], preferred_element_type=F32) + b_ref[...])
    x1_ref[...] = x1
    h = _rms(x1, g_ref[...])
    _to_token_major(h_ref, h)
    h_hi = h.astype(BF16)
    h_lo = (h - h_hi.astype(F32)).astype(BF16)
    logits = (jnp.dot(h_hi, wrh_ref[...], preferred_element_type=F32)
              + (jnp.dot(h_hi, wrl_ref[...], preferred_element_type=F32)
                 + jnp.dot(h_lo, wrh_ref[...], preferred_element_type=F32))) + br_ref[...]

    lane = lax.broadcasted_iota(jnp.int32, (tm, LANES), 1).astype(F32)
    first_lane = lambda hit: jnp.min(jnp.where(hit, lane, float(LANES)), axis=-1, keepdims=True)
    lg = jnp.where(lane < N_GROUPS, logits, -jnp.inf)
    mg = jnp.max(lg, axis=-1, keepdims=True)
    gsel = first_lane(lg == mg)
    gprob = 1.0 / jnp.sum(jnp.exp(lg - mg), axis=-1, keepdims=True)
    lo = ROUTER_LANE0 + gsel * EXPERTS_PER_GROUP
    in_group = (lane >= lo) & (lane < lo + EXPERTS_PER_GROUP)
    le = jnp.where(in_group, logits, -jnp.inf)
    ee = jnp.exp(le - jnp.max(le, axis=-1, keepdims=True))
    pe = jnp.where(in_group, ee / jnp.sum(ee, axis=-1, keepdims=True), -1.0)
    p1 = jnp.max(pe, axis=-1, keepdims=True)
    i1 = first_lane(pe == p1)
    pe2 = jnp.where(lane == i1, -1.0, pe)
    p2 = jnp.max(pe2, axis=-1, keepdims=True)
    i2 = first_lane(pe2 == p2)
    denom = p1 + p2
    g1 = gprob * p1 / denom
    g2 = gprob * p2 / denom
    sel1 = lane == i1
    sel2 = lane == i2
    onehot = jnp.where(sel1 | sel2, 1.0, 0.0)
    r = lax.broadcasted_iota(jnp.int32, (tm, tm), 0)
    c = lax.broadcasted_iota(jnp.int32, (tm, tm), 1)
    before = jnp.where(c < r, 1.0, 0.0).astype(BF16)
    rank = jnp.dot(before, onehot.astype(BF16), preferred_element_type=F32) + carry_ref[...]
    r1 = jnp.sum(jnp.where(sel1, rank, 0.0), axis=-1, keepdims=True)
    r2 = jnp.sum(jnp.where(sel2, rank, 0.0), axis=-1, keepdims=True)
    carry = carry_ref[...] + jnp.sum(onehot, axis=0, keepdims=True)
    carry_ref[...] = carry
    cnt_ref[...] = carry
    e1 = i1 - ROUTER_LANE0
    e2 = i2 - ROUTER_LANE0
    meta = jnp.zeros((tm, LANES), F32)
    for col, val in enumerate((e1, e2, r1, r2, g1, g2)):
        meta = jnp.where(lane == col, val, meta)
    meta_ref[...] = meta


def _proj_router(x, z_p, z_s, w, b, g, w_rg, b_rg, w_re, b_re):
    t = x.shape[0]
    tp, ts = z_p.shape[0], z_s.shape[0]
    tm = _token_tile(tp, ts)
    npt = tp // tm
    pad = LANES - N_GROUPS - N_EXPERTS
    wr = jnp.concatenate([w_rg, w_re, jnp.zeros((D_MODEL, pad), F32)], axis=1)
    wr_hi = wr.astype(BF16)
    wr_lo = (wr - wr_hi.astype(F32)).astype(BF16)
    br = jnp.concatenate([b_rg, b_re, jnp.zeros((pad,), F32)]).reshape(1, LANES)
    row = lambda i: (i, 0)
    const = lambda i: (0, 0)
    return pl.pallas_call(
        functools.partial(_proj_router_kernel, prompt_tiles=npt),
        grid=(t // tm,),
        in_specs=[pl.BlockSpec((tm, D_MODEL), row),
                  pl.BlockSpec((tm, D_MODEL), lambda i: (jnp.minimum(i, npt - 1), 0)),
                  pl.BlockSpec((tm, D_MODEL), lambda i: (jnp.maximum(i - npt, 0), 0)),
                  pl.BlockSpec((D_MODEL, D_MODEL), const), pl.BlockSpec((1, D_MODEL), const),
                  pl.BlockSpec((1, D_MODEL), const), pl.BlockSpec((D_MODEL, LANES), const),
                  pl.BlockSpec((D_MODEL, LANES), const), pl.BlockSpec((1, LANES), const)],
        out_specs=[pl.BlockSpec((tm, D_MODEL), row), pl.BlockSpec((tm * ROW_TILES, LANES), row),
                   pl.BlockSpec((tm, LANES), row), pl.BlockSpec((1, LANES), const)],
        out_shape=[jax.ShapeDtypeStruct((t, D_MODEL), F32), jax.ShapeDtypeStruct((t * ROW_TILES, LANES), F32),
                   jax.ShapeDtypeStruct((t, LANES), F32), jax.ShapeDtypeStruct((1, LANES), F32)],
        scratch_shapes=[pltpu.VMEM((1, LANES), F32)],
        compiler_params=_params(("arbitrary",)),
        name="proj_router",
    )(x, z_p, z_s, w.astype(BF16), b.reshape(1, D_MODEL), g.reshape(1, D_MODEL), wr_hi, wr_lo, br)


def _dispatch_plan(meta, counts, t):
    n = t * TOP_K
    n_blocks = -(-(n + N_EXPERTS * (EXPERT_BLOCK - 1)) // EXPERT_BLOCK)
    eid = meta[:, 0:TOP_K].astype(jnp.int32)
    rank = meta[:, TOP_K:2 * TOP_K].astype(jnp.int32)
    cnt = counts[0, ROUTER_LANE0:ROUTER_LANE0 + N_EXPERTS].astype(jnp.int32)
    padded = (cnt + EXPERT_BLOCK - 1) // EXPERT_BLOCK * EXPERT_BLOCK
    pad_end = jnp.cumsum(padded)
    pad_start = pad_end - padded
    experts = jnp.arange(N_EXPERTS, dtype=jnp.int32)
    start_of = jnp.sum(jnp.where(eid[..., None] == experts, pad_start, 0), axis=-1)
    dest = (start_of + rank).reshape(n)
    block_row0 = jnp.arange(n_blocks, dtype=jnp.int32) * EXPERT_BLOCK
    block_e = jnp.minimum(jnp.sum((pad_end[None, :] <= block_row0[:, None]).astype(jnp.int32), axis=1),
                          N_EXPERTS - 1)
    nused = (pad_end[-1] // EXPERT_BLOCK).reshape(1)
    zstart = jnp.where(padded > cnt, pad_end - EXPERT_BLOCK, -1)
    used = cnt > 0
    nxt_e = jnp.min(jnp.where((experts[None, :] > experts[:, None]) & used[None, :], experts[None, :], N_EXPERTS),
                    axis=1)
    slot_e = (jnp.cumsum(used.astype(jnp.int32)) - used.astype(jnp.int32)) % 2
    return dest, block_e, nused, zstart, nxt_e, slot_e, n_blocks


def _dispatch_kernel(dest_ref, zstart_ref, nused_ref, h_ref, xs_hbm, sbuf, zbuf, zsem, dsem, *, n_blocks):
    i = pl.program_id(0)
    tm = h_ref.shape[0] // ROW_TILES

    @pl.when(i == 0)
    def _():
        zbuf[...] = jnp.zeros_like(zbuf)
        nused = nused_ref[0]

        def zero_block(row0):
            rows = pl.ds(pl.multiple_of(row0 * ROW_TILES, EXPERT_BLOCK * ROW_TILES), EXPERT_BLOCK * ROW_TILES)
            return pltpu.make_async_copy(zbuf, xs_hbm.at[rows, :], zsem.at[0])

        def per_partial_block(fn):
            def body(e, carry):
                @pl.when(zstart_ref[e] >= 0)
                def _():
                    fn(pl.multiple_of(zstart_ref[e], EXPERT_BLOCK))
                return carry

            lax.fori_loop(0, N_EXPERTS, body, 0)

        def per_unused_block(fn):
            def body(j, carry):
                fn(pl.multiple_of(j * EXPERT_BLOCK, EXPERT_BLOCK))
                return carry

            lax.fori_loop(nused, n_blocks, body, 0)

        per_partial_block(lambda row0: zero_block(row0).start())
        per_unused_block(lambda row0: zero_block(row0).start())
        per_partial_block(lambda row0: zero_block(row0).wait())
        per_unused_block(lambda row0: zero_block(row0).wait())

    slot = i % 2
    sbuf[slot] = h_ref[...]

    in_flight = min(DISPATCH_LAG, tm * TOP_K // WAIT_CHUNK)

    @pl.when(i > 0)
    def _():
        _wait_row_copies(sbuf.at[1 - slot], xs_hbm, dsem.at[1 - slot], in_flight * WAIT_CHUNK)

    def start_token(j):
        tok = i * tm + j
        for k in range(TOP_K):
            _row_copy(sbuf.at[slot], j, xs_hbm, dest_ref[tok * TOP_K + k], dsem.at[slot]).start()

    _issue_rows_rolling(tm, start_token,
                        lambda: _wait_row_copies(sbuf.at[slot], xs_hbm, dsem.at[slot], WAIT_CHUNK), DISPATCH_LAG)

    @pl.when(i == pl.num_programs(0) - 1)
    def _():
        _wait_row_copies(sbuf.at[slot], xs_hbm, dsem.at[slot], in_flight * WAIT_CHUNK)


def _dispatch(h, dest, zstart, nused, n_blocks):
    t = h.shape[0] // ROW_TILES
    tm = _token_tile(t, largest=COMBINE_TILE)
    return pl.pallas_call(
        functools.partial(_dispatch_kernel, n_blocks=n_blocks),
        grid_spec=pltpu.PrefetchScalarGridSpec(
            num_scalar_prefetch=3, grid=(t // tm,),
            in_specs=[pl.BlockSpec((tm * ROW_TILES, LANES), lambda i, *_: (i, 0))],
            out_specs=pl.BlockSpec(memory_space=pl.ANY),
            scratch_shapes=[pltpu.VMEM((2, tm * ROW_TILES, LANES), F32),
                            pltpu.VMEM((EXPERT_BLOCK * ROW_TILES, LANES), F32),
                            pltpu.SemaphoreType.DMA((1,)), pltpu.SemaphoreType.DMA((2,))]),
        out_shape=jax.ShapeDtypeStruct((n_blocks * EXPERT_BLOCK * ROW_TILES, LANES), F32),
        compiler_params=_params(("arbitrary",)),
        name="dispatch",
    )(dest, zstart, nused, h)


def _experts_kernel(be_ref, nused_ref, nxt_ref, slot_ref, xs_ref, wg_hbm, wu_hbm, wd_hbm, ys_ref,
                    wbuf_g, wbuf_u, wbuf_d, wg_bf, wu_bf, wd_bf, wsem, *, layer):
    i = pl.program_id(0)
    nused = nused_ref[0]

    def weight_copies(e, slot):
        return (pltpu.make_async_copy(wg_hbm.at[layer, e], wbuf_g.at[slot], wsem.at[slot, 0]),
                pltpu.make_async_copy(wu_hbm.at[layer, e], wbuf_u.at[slot], wsem.at[slot, 1]),
                pltpu.make_async_copy(wd_hbm.at[layer, e], wbuf_d.at[slot], wsem.at[slot, 2]))

    @pl.when(i < nused)
    def _():
        e = be_ref[i]
        new_expert = (i == 0) | (e != be_ref[jnp.maximum(i - 1, 0)])

        @pl.when(new_expert)
        def _():
            slot = slot_ref[e]

            @pl.when(i == 0)
            def _():
                for c in weight_copies(e, slot):
                    c.start()

            for c in weight_copies(e, slot):
                c.wait()
            nxt = nxt_ref[e]

            @pl.when(nxt < N_EXPERTS)
            def _():
                for c in weight_copies(nxt, 1 - slot):
                    c.start()

            wg_bf[...] = wbuf_g[slot].astype(BF16)
            wu_bf[...] = wbuf_u[slot].astype(BF16)
            wd_bf[...] = wbuf_d[slot].astype(BF16)

        x = _from_token_major(xs_ref, EXPERT_BLOCK).astype(BF16)
        a = jnp.dot(x, wg_bf[...], preferred_element_type=F32)
        u = jnp.dot(x, wu_bf[...], preferred_element_type=F32)
        act = (jax.nn.silu(a) * u).astype(BF16)
        _to_token_major(ys_ref, jnp.dot(act, wd_bf[...], preferred_element_type=F32))

    @pl.when(i >= nused)
    def _():
        ys_ref[...] = jnp.zeros_like(ys_ref)


def _experts(xs, block_e, nused, nxt_e, slot_e, w_gate, w_up, w_down, layer):
    n_blocks = block_e.shape[0]
    return pl.pallas_call(
        functools.partial(_experts_kernel, layer=layer),
        grid_spec=pltpu.PrefetchScalarGridSpec(
            num_scalar_prefetch=4, grid=(n_blocks,),
            in_specs=[pl.BlockSpec((EXPERT_BLOCK * ROW_TILES, LANES),
                                   lambda i, be, nu, *_: (jnp.minimum(i, nu[0] - 1), 0)),
                      pl.BlockSpec(memory_space=pl.ANY), pl.BlockSpec(memory_space=pl.ANY),
                      pl.BlockSpec(memory_space=pl.ANY)],
            out_specs=pl.BlockSpec((EXPERT_BLOCK * ROW_TILES, LANES), lambda i, *_: (i, 0)),
            scratch_shapes=[pltpu.VMEM((2, D_MODEL, D_EXPERT), F32),
                            pltpu.VMEM((2, D_MODEL, D_EXPERT), F32),
                            pltpu.VMEM((2, D_EXPERT, D_MODEL), F32),
                            pltpu.VMEM((D_MODEL, D_EXPERT), BF16),
                            pltpu.VMEM((D_MODEL, D_EXPERT), BF16),
                            pltpu.VMEM((D_EXPERT, D_MODEL), BF16),
                            pltpu.SemaphoreType.DMA((2, 3))]),
        out_shape=jax.ShapeDtypeStruct((n_blocks * EXPERT_BLOCK * ROW_TILES, LANES), F32),
        compiler_params=_params(("arbitrary",)),
        name="experts",
    )(block_e, nused, nxt_e, slot_e, xs, w_gate, w_up, w_down)


def _final_kernel(dest_ref, x_ref, meta_ref, ys_hbm, g_ref, yp_ref, ys_ref, mbuf, sem, *, prompt_tiles):
    i = pl.program_id(0)
    y = _rms(_moe_combine(dest_ref, x_ref, meta_ref, ys_hbm, mbuf, sem), g_ref[...])

    @pl.when(i < prompt_tiles)
    def _():
        yp_ref[...] = y

    @pl.when(i >= prompt_tiles)
    def _():
        ys_ref[...] = y


def _final_norm(x, moe, g, *, tp, tm):
    t = x.shape[0]
    dest, meta, ys = moe
    npt = tp // tm
    c_specs, c_scratch = _combine_specs(tm)
    return pl.pallas_call(
        functools.partial(_final_kernel, prompt_tiles=npt),
        grid_spec=pltpu.PrefetchScalarGridSpec(
            num_scalar_prefetch=1, grid=(t // tm,),
            in_specs=c_specs + [pl.BlockSpec((1, D_MODEL), lambda i, *_: (0, 0))],
            out_specs=[pl.BlockSpec((tm, D_MODEL), lambda i, *_: (jnp.minimum(i, npt - 1), 0)),
                       pl.BlockSpec((tm, D_MODEL), lambda i, *_: (jnp.maximum(i - npt, 0), 0))],
            scratch_shapes=c_scratch),
        out_shape=[jax.ShapeDtypeStruct((tp, D_MODEL), F32), jax.ShapeDtypeStruct((t - tp, D_MODEL), F32)],
        compiler_params=_params(("arbitrary",)),
        name="final_norm",
    )(dest, x, meta, ys, g.reshape(1, D_MODEL))


def kernel(x_prompt, x_sample, cache_k, cache_v, state_conv, norm_mix, norm_ffn, norm_final, conv_w_pw1, conv_b_pw1, conv_w_dw, conv_b_dw, conv_ln_g, conv_ln_b, conv_w_pw2, conv_b_pw2, attn_w_qkv, attn_b_qkv, attn_sinks, attn_w_o, attn_b_o, moe_w_router_group, moe_b_router_group, moe_w_router_expert, moe_b_router_expert, moe_w_gate, moe_w_up, moe_w_down):
    batch, seq, _ = x_prompt.shape
    n_dec, dec_seq, _ = x_sample.shape
    depth = norm_mix.shape[0]
    tp, ts = batch * seq, n_dec * dec_seq
    t = tp + ts
    tm = _token_tile(tp, ts, largest=COMBINE_TILE)

    assert dec_seq <= CONV_WIDTH - 1 and dec_seq <= WINDOW

    def prompt_tail(a, n_rows):
        return jnp.stack([a[(b + 1) * seq - n_rows:(b + 1) * seq] for b in range(batch)])

    cache_k_t = cache_k.transpose(0, 1, 3, 2, 4)
    cache_v_t = cache_v.transpose(0, 1, 3, 2, 4)
    state_t = state_conv.transpose(0, 2, 1, 3)

    x = (x_prompt.reshape(tp, D_MODEL), x_sample.reshape(ts, D_MODEL))
    moe = None
    kp_list, vp_list, cp_list, ks_list, vs_list, cs_list = [], [], [], [], [], []
    for layer in range(depth):
        j = layer // 2
        if layer % 2 == 0:
            x, u = _mixer_in(x, moe, norm_mix[layer], conv_w_pw1[j], conv_b_pw1[j], is_conv=True, tm=tm)
            u_s = u[tp:].reshape(n_dec, dec_seq, D_MODEL)
            cp_list.append(prompt_tail(u, CONV_WIDTH - 1))
            cs_list.append(u_s)
            z_p = _conv_prompt(u, conv_w_dw[j], conv_b_dw[j], conv_ln_g[j], conv_ln_b[j], batch=batch, seq=seq)
            z_s = _conv_sample(state_t, j, jnp.swapaxes(u_s, 0, 1), conv_w_dw[j], conv_b_dw[j],
                               conv_ln_g[j], conv_ln_b[j])
            z_s = jnp.swapaxes(z_s, 0, 1).reshape(ts, D_MODEL)
            w_out, b_out = conv_w_pw2[j], conv_b_pw2[j]
        else:
            x, q, k, v = _mixer_in(x, moe, norm_mix[layer], attn_w_qkv[j], attn_b_qkv[j], is_conv=False, tm=tm)
            kp_list.append(prompt_tail(k, WINDOW).reshape(batch, WINDOW, N_KV_HEADS, HEAD_DIM))
            vp_list.append(prompt_tail(v, WINDOW).reshape(batch, WINDOW, N_KV_HEADS, HEAD_DIM))
            k_s = k[tp:].reshape(n_dec, dec_seq, N_KV_HEADS, HEAD_DIM)
            v_s = v[tp:].reshape(n_dec, dec_seq, N_KV_HEADS, HEAD_DIM)
            ks_list.append(k_s)
            vs_list.append(v_s)
            z_p = _attn_prompt(q, k, v, attn_sinks[j], batch=batch, seq=seq)
            q_s = q[tp:].reshape(n_dec, dec_seq, N_KV_HEADS, GROUP, HEAD_DIM).transpose(0, 2, 3, 1, 4)
            q_s = q_s.reshape(n_dec, N_KV_HEADS, GROUP * dec_seq, HEAD_DIM)
            o_s = _attn_sample(q_s, cache_k_t, cache_v_t, j, k_s.transpose(0, 2, 1, 3), v_s.transpose(0, 2, 1, 3),
                               attn_sinks[j], dec_seq=dec_seq)
            o_s = o_s.reshape(n_dec, N_KV_HEADS, GROUP, dec_seq, HEAD_DIM).transpose(0, 3, 1, 2, 4)
            z_s = o_s.reshape(ts, D_MODEL)
            w_out, b_out = attn_w_o[j], attn_b_o[j]
        x, h, meta, counts = _proj_router(x, z_p, z_s, w_out, b_out, norm_ffn[layer],
                                          moe_w_router_group[layer], moe_b_router_group[layer],
                                          moe_w_router_expert[layer], moe_b_router_expert[layer])
        dest, block_e, nused, zstart, nxt_e, slot_e, n_blocks = _dispatch_plan(meta, counts, t)
        xs = _dispatch(h, dest, zstart, nused, n_blocks)
        ys = _experts(xs, block_e, nused, nxt_e, slot_e, moe_w_gate, moe_w_up, moe_w_down, layer)
        moe = (dest, meta, ys)
    y_p, y_s = _final_norm(x, moe, norm_final, tp=tp, tm=tm)
    new_k_s = jnp.concatenate([cache_k[:, :, dec_seq:], jnp.stack(ks_list)], axis=2)
    new_v_s = jnp.concatenate([cache_v[:, :, dec_seq:], jnp.stack(vs_list)], axis=2)
    new_c_s = jnp.concatenate([state_conv[:, :, dec_seq:], jnp.stack(cs_list)], axis=2)
    return (y_p.reshape(batch, seq, D_MODEL), y_s.reshape(n_dec, dec_seq, D_MODEL),
            jnp.stack(kp_list), jnp.stack(vp_list), jnp.stack(cp_list), new_k_s, new_v_s, new_c_s)
```

```python
import functools

import jax
import jax.numpy as jnp
from jax import lax
from jax.experimental import pallas as pl
from jax.experimental.pallas import tpu as pltpu

F32 = jnp.float32
BF16 = jnp.bfloat16

D_MODEL = 1024
CONV_WIDTH = 31
N_HEADS = 16
N_KV_HEADS = 4
HEAD_DIM = D_MODEL // N_HEADS
GROUP = N_HEADS // N_KV_HEADS
KV_DIM = N_KV_HEADS * HEAD_DIM
WINDOW = 128
N_GROUPS = 8
EXPERTS_PER_GROUP = 8
N_EXPERTS = N_GROUPS * EXPERTS_PER_GROUP
TOP_K = 2
D_EXPERT = D_MODEL // 2
EXPERT_BLOCK = 256
RMS_EPS = 1e-6
LN_EPS = 1e-5

LANES = 128
SUBLANES = 8
ROW_TILES = D_MODEL // LANES
assert ROW_TILES == SUBLANES
ROUTER_LANE0 = N_GROUPS
META_GATE_LANE = 2 * TOP_K
CONV_HALO = 32
CONV_ROWS = 32
WAIT_CHUNK = 128
ISSUE_UNROLL = 8
DISPATCH_LAG = 2
COMBINE_LAG = 5
COMBINE_TILE = 512
VMEM_LIMIT =56 * 1024 * 1024


def _params(sem):
    return pltpu.CompilerParams(dimension_semantics=sem, vmem_limit_bytes=VMEM_LIMIT)


def _rms(x, g):
    return x * lax.rsqrt(jnp.mean(x * x, axis=-1, keepdims=True) + RMS_EPS) * g


def _token_tile(*counts, largest=512):
    for tm in (512, 256, 128):
        if tm <= largest and all(c % tm == 0 for c in counts):
            return tm
    raise ValueError("token counts must be multiples of 128")


def _to_token_major(ref, x):
    n = x.shape[0]
    for s in range(ROW_TILES):
        ref[pl.ds(s, n, stride=ROW_TILES), :] = x[:, s * LANES:(s + 1) * LANES]


def _from_token_major(ref, n):
    return jnp.concatenate([ref[pl.ds(s, n, stride=ROW_TILES), :] for s in range(ROW_TILES)], axis=1)


def _row_slice(row):
    start = row * ROW_TILES
    return pl.ds(start if isinstance(start, int) else pl.multiple_of(start, ROW_TILES), ROW_TILES)


def _row_copy(src, src_row, dst, dst_row, sem):
    return pltpu.make_async_copy(src.at[_row_slice(src_row), :], dst.at[_row_slice(dst_row), :], sem)


def _wait_row_copies(src_hbm, dst, sem, n_rows):
    def chunk(c, carry):
        for _ in range(WAIT_CHUNK):
            _row_copy(src_hbm, 0, dst, 0, sem).wait()
        return carry

    lax.fori_loop(0, n_rows // WAIT_CHUNK, chunk, 0)


def _issue_rows_rolling(n_tokens, start_token, wait_chunk, lag):
    chunk_tokens = WAIT_CHUNK // TOP_K
    n_chunks = n_tokens // chunk_tokens

    def chunk(c, carry):
        def body(j, inner):
            start_token(c * chunk_tokens + j)
            return inner

        lax.fori_loop(0, chunk_tokens, body, 0, unroll=ISSUE_UNROLL)

        @pl.when(c >= lag)
        def _():
            wait_chunk()

        return carry

    lax.fori_loop(0, n_chunks, chunk, 0)
    return min(lag, n_chunks)


def _moe_combine(dest_ref, x_ref, meta_ref, ys_hbm, mbuf, sem):
    i = pl.program_id(0)
    tm = x_ref.shape[0]

    def issue(tile, slot):
        def start_token(j):
            tok = tile * tm + j
            for k in range(TOP_K):
                _row_copy(ys_hbm, dest_ref[tok * TOP_K + k], mbuf.at[slot, k], j, sem.at[slot]).start()

        return _issue_rows_rolling(
            tm, start_token, lambda: _wait_row_copies(ys_hbm, mbuf.at[slot, 0], sem.at[slot], WAIT_CHUNK),
            COMBINE_LAG)

    in_flight = min(COMBINE_LAG, tm * TOP_K // WAIT_CHUNK)

    @pl.when(i == 0)
    def _():
        issue(0, 0)

    slot = i % 2
    _wait_row_copies(ys_hbm, mbuf.at[slot, 0], sem.at[slot], in_flight * WAIT_CHUNK)

    @pl.when(i + 1 < pl.num_programs(0))
    def _():
        issue(i + 1, (i + 1) % 2)

    moe = meta_ref[:, META_GATE_LANE:META_GATE_LANE + 1] * _from_token_major(mbuf.at[slot, 0], tm)
    for k in range(1, TOP_K):
        moe = moe + (meta_ref[:, META_GATE_LANE + k:META_GATE_LANE + k + 1]
                     * _from_token_major(mbuf.at[slot, k], tm))
    return x_ref[...] + moe


def _combine_specs(tm):
    in_specs = [pl.BlockSpec((tm, D_MODEL), lambda i, *_: (i, 0)),
                pl.BlockSpec((tm, LANES), lambda i, *_: (i, 0)),
                pl.BlockSpec(memory_space=pl.ANY)]
    scratch = [pltpu.VMEM((2, TOP_K, tm * ROW_TILES, LANES), F32), pltpu.SemaphoreType.DMA((2,))]
    return in_specs, scratch


def _mixer_proj(x, g_ref, w_ref, b_ref, outs, is_conv):
    h = _rms(x, g_ref[...]).astype(BF16)
    y = jnp.dot(h, w_ref[...], preferred_element_type=F32) + b_ref[...]
    if is_conv:
        (u_ref,) = outs
        u_ref[...] = y[:, :D_MODEL] * jax.nn.sigmoid(y[:, D_MODEL:])
    else:
        q_ref, k_ref, v_ref = outs
        q_ref[...] = (y[:, :D_MODEL] * (HEAD_DIM ** -0.5)).astype(BF16)
        k_ref[...] = y[:, D_MODEL:D_MODEL + KV_DIM]
        v_ref[...] = y[:, D_MODEL + KV_DIM:]


def _mixer_in_first_kernel(xp_ref, xs_ref, g_ref, w_ref, b_ref, xo_ref, *outs, is_conv, prompt_tiles):
    x = jnp.where(pl.program_id(0) < prompt_tiles, xp_ref[...], xs_ref[...])
    xo_ref[...] = x
    _mixer_proj(x, g_ref, w_ref, b_ref, outs, is_conv)


def _mixer_in_kernel(dest_ref, x_ref, meta_ref, ys_hbm, g_ref, w_ref, b_ref, xo_ref, *rest, is_conv):
    *outs, mbuf, sem = rest
    x = _moe_combine(dest_ref, x_ref, meta_ref, ys_hbm, mbuf, sem)
    xo_ref[...] = x
    _mixer_proj(x, g_ref, w_ref, b_ref, outs, is_conv)


def _mixer_in(x, moe, g, w, b, *, is_conv, tm):
    t = x.shape[0] if moe is not None else x[0].shape[0] + x[1].shape[0]
    w_all, j = w
    n_out = w_all.shape[2]
    row = lambda i, *_: (i, 0)
    const = lambda i, *_: (0, 0)
    w_specs = [pl.BlockSpec((1, D_MODEL), const), pl.BlockSpec((None, D_MODEL, n_out), lambda i, *_: (j, 0, 0)),
               pl.BlockSpec((1, n_out), const)]
    w_args = [g.reshape(1, D_MODEL), w_all, b.reshape(1, n_out)]
    if is_conv:
        out_shape = [jax.ShapeDtypeStruct((t, D_MODEL), F32)]
        out_specs = [pl.BlockSpec((tm, D_MODEL), row)]
    else:
        out_shape = [jax.ShapeDtypeStruct((t, D_MODEL), BF16), jax.ShapeDtypeStruct((t, KV_DIM), F32),
                     jax.ShapeDtypeStruct((t, KV_DIM), F32)]
        out_specs = [pl.BlockSpec((tm, D_MODEL), row), pl.BlockSpec((tm, KV_DIM), row),
                     pl.BlockSpec((tm, KV_DIM), row)]
    name = "mixer_in_conv" if is_conv else "mixer_in_attn"
    if moe is None:
        x_p, x_s = x
        npt = x_p.shape[0] // tm
        return pl.pallas_call(
            functools.partial(_mixer_in_first_kernel, is_conv=is_conv, prompt_tiles=npt),
            grid=(t // tm,),
            in_specs=[pl.BlockSpec((tm, D_MODEL), lambda i: (jnp.minimum(i, npt - 1), 0)),
                      pl.BlockSpec((tm, D_MODEL), lambda i: (jnp.maximum(i - npt, 0), 0))] + w_specs,
            out_specs=[pl.BlockSpec((tm, D_MODEL), row)] + out_specs,
            out_shape=[jax.ShapeDtypeStruct((t, D_MODEL), F32)] + out_shape,
            compiler_params=_params(("parallel",)), name=name + "_first",
        )(x_p, x_s, *w_args)
    dest, meta, ys = moe
    c_specs, c_scratch = _combine_specs(tm)
    return pl.pallas_call(
        functools.partial(_mixer_in_kernel, is_conv=is_conv),
        grid_spec=pltpu.PrefetchScalarGridSpec(
            num_scalar_prefetch=1, grid=(t // tm,),
            in_specs=c_specs + w_specs,
            out_specs=[pl.BlockSpec((tm, D_MODEL), row)] + out_specs,
            scratch_shapes=c_scratch),
        out_shape=[jax.ShapeDtypeStruct((t, D_MODEL), F32)] + out_shape,
        compiler_params=_params(("arbitrary",)), name=name,
    )(dest, x, meta, ys, *w_args)


def _ln_silu(c, g, b):
    mu = jnp.mean(c, axis=-1, keepdims=True)
    d = c - mu
    var = jnp.mean(d * d, axis=-1, keepdims=True)
    return jax.nn.silu(d * lax.rsqrt(var + LN_EPS) * g + b)


def _conv_prompt_kernel(u_ref, w_ref, bdw_ref, lg_ref, lb_ref, z_ref, ubuf, shifted, *, ts):
    s = pl.program_id(1)
    nbuf = ts + CONV_HALO

    @pl.when(s == 0)
    def _():
        ubuf[0:CONV_HALO, :] = jnp.zeros((CONV_HALO, D_MODEL), F32)

    ubuf[CONV_HALO:nbuf, :] = u_ref[...]
    ubuf[nbuf:nbuf + SUBLANES, :] = jnp.zeros((SUBLANES, D_MODEL), F32)

    def shift_chunk(c, carry):
        r0 = pl.multiple_of(c * CONV_ROWS, CONV_ROWS)
        win = ubuf[pl.ds(r0, CONV_ROWS + SUBLANES), :]
        for b in range(1, SUBLANES):
            rolled = pltpu.roll(win, CONV_ROWS + SUBLANES - b, axis=0)
            shifted[b - 1, pl.ds(r0, CONV_ROWS), :] = rolled[:CONV_ROWS]
        return carry

    lax.fori_loop(0, nbuf // CONV_ROWS, shift_chunk, 0)
    first = CONV_HALO - (CONV_WIDTH - 1)

    def chunk(c, carry):
        base = pl.multiple_of(c * CONV_ROWS, CONV_ROWS)
        groups = CONV_ROWS // SUBLANES
        accs = [jnp.zeros((SUBLANES, D_MODEL), F32) for _ in range(groups)]
        for k in range(CONV_WIDTH):
            a, b = divmod(first + k, SUBLANES)
            wk = w_ref[k]
            for g in range(groups):
                rows = pl.ds(base + (a + g) * SUBLANES, SUBLANES)
                tap = ubuf[rows, :] if b == 0 else shifted[b - 1, rows, :]
                accs[g] = accs[g] + wk * tap
        acc = jnp.concatenate(accs, axis=0)
        z = _ln_silu(acc + bdw_ref[...], lg_ref[...], lb_ref[...])
        z_ref[pl.ds(base, CONV_ROWS), :] = z.astype(BF16)
        return carry

    lax.fori_loop(0, ts // CONV_ROWS, chunk, 0)
    ubuf[0:CONV_HALO, :] = ubuf[ts:nbuf, :]


def _conv_prompt(u, w_dw, b_dw, ln_g, ln_b, *, batch, seq):
    ts = 512 if seq % 512 == 0 else seq
    ns = seq // ts
    const = lambda b, s: (0, 0)
    return pl.pallas_call(
        functools.partial(_conv_prompt_kernel, ts=ts),
        grid=(batch, ns),
        in_specs=[pl.BlockSpec((ts, D_MODEL), lambda b, s: (b * ns + s, 0)),
                  pl.BlockSpec((CONV_WIDTH, SUBLANES, D_MODEL), lambda b, s: (0, 0, 0)),
                  pl.BlockSpec((1, D_MODEL), const), pl.BlockSpec((1, D_MODEL), const),
                  pl.BlockSpec((1, D_MODEL), const)],
        out_specs=pl.BlockSpec((ts, D_MODEL), lambda b, s: (b * ns + s, 0)),
        out_shape=jax.ShapeDtypeStruct((batch * seq, D_MODEL), BF16),
        scratch_shapes=[pltpu.VMEM((ts + CONV_HALO + SUBLANES, D_MODEL), F32),
                        pltpu.VMEM((SUBLANES - 1, ts + CONV_HALO, D_MODEL), F32)],
        compiler_params=_params(("arbitrary", "arbitrary")),
        name="conv_prompt",
    )(u, jnp.broadcast_to(w_dw[:, None, :], (CONV_WIDTH, SUBLANES, D_MODEL)), b_dw.reshape(1, D_MODEL), ln_g.reshape(1, D_MODEL), ln_b.reshape(1, D_MODEL))


def _conv_sample_kernel(st_ref, us_ref, w_ref, bdw_ref, lg_ref, lb_ref, z_ref, *, dec_seq):
    n_past = st_ref.shape[0]
    for t in range(dec_seq):
        acc = jnp.zeros(us_ref.shape[1:], F32)
        for k in range(CONV_WIDTH):
            r = t + k
            acc = acc + w_ref[k:k + 1, :] * (st_ref[r] if r < n_past else us_ref[r - n_past])
        z_ref[t] = _ln_silu(acc + bdw_ref[...], lg_ref[...], lb_ref[...]).astype(BF16)


def _conv_sample(state_t, j, us_t, w_dw, b_dw, ln_g, ln_b):
    _, n_past, n, _ = state_t.shape
    dec_seq = us_t.shape[0]
    nb = 32 if n % 32 == 0 else n
    const = lambda i: (0, 0)
    return pl.pallas_call(
        functools.partial(_conv_sample_kernel, dec_seq=dec_seq),
        grid=(n // nb,),
        in_specs=[pl.BlockSpec((None, n_past, nb, D_MODEL), lambda i: (j, 0, i, 0)),
                  pl.BlockSpec((dec_seq, nb, D_MODEL), lambda i: (0, i, 0)),
                  pl.BlockSpec((CONV_WIDTH, D_MODEL), const),
                  pl.BlockSpec((1, D_MODEL), const), pl.BlockSpec((1, D_MODEL), const),
                  pl.BlockSpec((1, D_MODEL), const)],
        out_specs=pl.BlockSpec((dec_seq, nb, D_MODEL), lambda i: (0, i, 0)),
        out_shape=jax.ShapeDtypeStruct((dec_seq, n, D_MODEL), BF16),
        compiler_params=_params(("parallel",)),
        name="conv_sample",
    )(state_t, us_t, w_dw, b_dw.reshape(1, D_MODEL), ln_g.reshape(1, D_MODEL), ln_b.reshape(1, D_MODEL))


def _softmax_with_sink(s, allowed, sink):
    if allowed is not None:
        s = jnp.where(allowed, s, -jnp.inf)
    m =jnp.maximum(jnp.max(s, axis=-1, keepdims=True), sink)
    e = jnp.exp(s - m)
    return e / (jnp.sum(e, axis=-1, keepdims=True) + jnp.exp(sink - m))


def _attn_prompt_kernel(sink_ref, q_ref, kp_ref, kc_ref, vp_ref, vc_ref, o_ref, bias_ref):
    i = pl.program_id(1)
    blk = q_ref.shape[0]
    qi = lax.broadcasted_iota(jnp.int32, (blk, 2 * blk), 0)
    kc = lax.broadcasted_iota(jnp.int32, (blk, 2 * blk), 1)
    allowed = (kc >= qi) & (kc <= qi + WINDOW) & ((i > 0) | (kc >= blk))
    bias_ref[...] = jnp.where(allowed, 0.0, -jnp.inf)
    for hk in range(N_KV_HEADS):
        cols = slice(hk * HEAD_DIM, (hk + 1) * HEAD_DIM)
        kcat = jnp.concatenate([kp_ref[:, cols], kc_ref[:, cols]], axis=0).astype(BF16)
        vcat = jnp.concatenate([vp_ref[:, cols], vc_ref[:, cols]], axis=0).astype(BF16)
        for pair in range(GROUP // 2):
            outs = []
            for g in (2 * pair, 2 * pair + 1):
                h = hk * GROUP + g
                q = q_ref[:, h * HEAD_DIM:(h + 1) * HEAD_DIM]
                s = lax.dot_general(q, kcat, (((1,), (1,)), ((), ())), preferred_element_type=F32)
                p = _softmax_with_sink(s + bias_ref[...], None, sink_ref[h])
                outs.append(jnp.dot(p.astype(BF16), vcat, preferred_element_type=F32))
            lo = (hk * GROUP + 2 * pair) * HEAD_DIM
            o_ref[:, lo:lo + 2 * HEAD_DIM] = jnp.concatenate(outs, axis=-1).astype(BF16)


def _attn_prompt(q, k, v, sinks, *, batch, seq):
    blk = WINDOW
    nb = seq // blk
    cur = lambda b, i, *_: (b * nb + i, 0)
    prev = lambda b, i, *_: (b * nb + jnp.maximum(i - 1, 0), 0)
    return pl.pallas_call(
        _attn_prompt_kernel,
        grid_spec=pltpu.PrefetchScalarGridSpec(
            num_scalar_prefetch=1, grid=(batch, nb),
            in_specs=[pl.BlockSpec((blk, D_MODEL), cur),
                      pl.BlockSpec((blk, KV_DIM), prev), pl.BlockSpec((blk, KV_DIM), cur),
                      pl.BlockSpec((blk, KV_DIM), prev), pl.BlockSpec((blk, KV_DIM), cur)],
            out_specs=pl.BlockSpec((blk, D_MODEL), cur),
            scratch_shapes=[pltpu.VMEM((blk, 2 * blk), F32)]),
        out_shape=jax.ShapeDtypeStruct((batch * seq, D_MODEL), BF16),
        compiler_params=_params(("parallel", "parallel")),
        name="attn_prompt",
    )(sinks, q, k, k, v, v)


def _attn_sample_kernel(q_ref, kc_ref, vc_ref, kn_ref, vn_ref, sink_ref, o_ref, *, dec_seq):
    ns, kvh, nq, hd = q_ref.shape
    nc = kc_ref.shape[2]
    b = ns * kvh
    q = q_ref[...].reshape(b, nq, hd)
    kc = kc_ref[...].reshape(b, nc, hd).astype(BF16)
    vc = vc_ref[...].reshape(b, nc, hd).astype(BF16)
    qf = q.astype(F32)
    kn = kn_ref[...].reshape(b, dec_seq, hd).astype(BF16).astype(F32)
    vn = vn_ref[...].reshape(b, dec_seq, hd).astype(BF16).astype(F32)
    t_c = lax.broadcasted_iota(jnp.int32, (nq, nc), 0) % dec_seq
    c = lax.broadcasted_iota(jnp.int32, (nq, nc), 1)
    t_n = lax.broadcasted_iota(jnp.int32, (nq, 1), 0) % dec_seq
    s_c = jnp.einsum('bqd,bkd->bqk', q, kc, preferred_element_type=F32)
    s_c = jnp.where((c >= t_c)[None], s_c, -jnp.inf).reshape(ns, kvh, nq, nc)
    s_n = [jnp.where((t_n >= n)[None], jnp.sum(qf * kn[:, n:n + 1, :], axis=-1, keepdims=True), -jnp.inf)
           .reshape(ns, kvh, nq, 1) for n in range(dec_seq)]
    sink = sink_ref[...][None]
    m = jnp.maximum(jnp.max(s_c, axis=-1, keepdims=True), sink)
    for s in s_n:
        m = jnp.maximum(m, s)
    e_c = jnp.exp(s_c - m)
    e_n = [jnp.exp(s - m) for s in s_n]
    denom = jnp.sum(e_c, axis=-1, keepdims=True) + jnp.exp(sink - m)
    for e in e_n:
        denom = denom + e
    p_c = (e_c / denom).reshape(b, nq, nc).astype(BF16)
    o = jnp.einsum('bqk,bkd->bqd', p_c, vc, preferred_element_type=F32)
    for n in range(dec_seq):
        p_n = (e_n[n] / denom).reshape(b, nq, 1).astype(BF16).astype(F32)
        o = o + p_n * vn[:, n:n + 1, :]
    o_ref[...] = o.reshape(ns, kvh, nq, hd).astype(BF16)


def _attn_sample(q, cache_k_t, cache_v_t, j, k_new, v_new, sinks, *, dec_seq):
    n, kvh, nq, hd = q.shape
    nc = cache_k_t.shape[3]
    ns = 8 if n % 8 == 0 else n
    sink_rows = jnp.repeat(sinks.reshape(N_KV_HEADS, GROUP), dec_seq, axis=1)[..., None]
    blk4 = lambda i: (i, 0, 0, 0)
    cache_spec = pl.BlockSpec((None, ns, kvh, nc, hd), lambda i: (j, i, 0, 0, 0))
    return pl.pallas_call(
        functools.partial(_attn_sample_kernel, dec_seq=dec_seq),
        grid=(n // ns,),
        in_specs=[pl.BlockSpec((ns, kvh, nq, hd), blk4), cache_spec, cache_spec,
                  pl.BlockSpec((ns, kvh, dec_seq, hd), blk4), pl.BlockSpec((ns, kvh, dec_seq, hd), blk4),
                  pl.BlockSpec((kvh, nq, 1), lambda i: (0, 0, 0))],
        out_specs=pl.BlockSpec((ns, kvh, nq, hd), blk4),
        out_shape=jax.ShapeDtypeStruct((n, kvh, nq, hd), BF16),
        compiler_params=_params(("parallel",)),
        name="attn_sample",
    )(q, cache_k_t, cache_v_t, k_new, v_new, sink_rows)


def _proj_router_kernel(x_ref, zp_ref, zs_ref, w_ref, b_ref, g_ref, wrh_ref, wrl_ref, br_ref,
                        x1_ref, h_ref, meta_ref, cnt_ref, carry_ref, *, prompt_tiles):
    i = pl.program_id(0)
    tm = x_ref.shape[0]

    @pl.when(i == 0)
    def _():
        carry_ref[...] = jnp.zeros_like(carry_ref)

    z = jnp.where(i < prompt_tiles, zp_ref[...], zs_ref[...])
    x1 = x_ref[...] + (jnp.dot(z, w_ref[...], preferred_element_type=F32) + b_ref[...])
    x1_ref[...] = x1
    h = _rms(x1, g_ref[...])
    _to_token_major(h_ref, h)
    h_hi = h.astype(BF16)
    h_lo = (h - h_hi.astype(F32)).astype(BF16)
    logits = (jnp.dot(h_hi, wrh_ref[...], preferred_element_type=F32)
              + (jnp.dot(h_hi, wrl_ref[...], preferred_element_type=F32)
                 + jnp.dot(h_lo, wrh_ref[...], preferred_element_type=F32))) + br_ref[...]

    lane = lax.broadcasted_iota(jnp.int32, (tm, LANES), 1).astype(F32)
    first_lane = lambda hit: jnp.min(jnp.where(hit, lane, float(LANES)), axis=-1, keepdims=True)
    lg = jnp.where(lane < N_GROUPS, logits, -jnp.inf)
    mg = jnp.max(lg, axis=-1, keepdims=True)
    gsel = first_lane(lg == mg)
    gprob = 1.0 / jnp.sum(jnp.exp(lg - mg), axis=-1, keepdims=True)
    lo = ROUTER_LANE0 + gsel * EXPERTS_PER_GROUP
    in_group = (lane >= lo) & (lane < lo + EXPERTS_PER_GROUP)
    le = jnp.where(in_group, logits, -jnp.inf)
    ee = jnp.exp(le - jnp.max(le, axis=-1, keepdims=True))
    pe = jnp.where(in_group, ee / jnp.sum(ee, axis=-1, keepdims=True), -1.0)
    p1 = jnp.max(pe, axis=-1, keepdims=True)
    i1 = first_lane(pe == p1)
    pe2 = jnp.where(lane == i1, -1.0, pe)
    p2 = jnp.max(pe2, axis=-1, keepdims=True)
    i2 = first_lane(pe2 == p2)
    denom = p1 + p2
    g1 = gprob * p1 / denom
    g2 = gprob * p2 / denom
    sel1 = lane == i1
    sel2 = lane == i2
    onehot = jnp.where(sel1 | sel2, 1.0, 0.0)
    r = lax.broadcasted_iota(jnp.int32, (tm, tm), 0)
    c = lax.broadcasted_iota(jnp.int32, (tm, tm), 1)
    before = jnp.where(c < r, 1.0, 0.0).astype(BF16)
    rank = jnp.dot(before, onehot.astype(BF16), preferred_element_type=F32) + carry_ref[...]
    r1 = jnp.sum(jnp.where(sel1, rank, 0.0), axis=-1, keepdims=True)
    r2 = jnp.sum(jnp.where(sel2, rank, 0.0), axis=-1, keepdims=True)
    carry = carry_ref[...] + jnp.sum(onehot, axis=0, keepdims=True)
    carry_ref[...] = carry
    cnt_ref[...] = carry
    e1 = i1 - ROUTER_LANE0
    e2 = i2 - ROUTER_LANE0
    meta = jnp.zeros((tm, LANES), F32)
    for col, val in enumerate((e1, e2, r1, r2, g1, g2)):
        meta = jnp.where(lane == col, val, meta)
    meta_ref[...] = meta


def _router_weights(w_rg, b_rg, w_re, b_re):
    depth = w_rg.shape[0]
    pad = LANES - N_GROUPS - N_EXPERTS
    wr = jnp.concatenate([w_rg, w_re, jnp.zeros((depth, D_MODEL, pad), F32)], axis=2)
    wr_hi = wr.astype(BF16)
    wr_lo = (wr - wr_hi.astype(F32)).astype(BF16)
    br = jnp.concatenate([b_rg, b_re, jnp.zeros((depth, pad), F32)], axis=1).reshape(depth, 1, LANES)
    return wr_hi, wr_lo, br


def _proj_router(x, z_p, z_s, w, b, g, router, layer):
    t = x.shape[0]
    tp, ts = z_p.shape[0], z_s.shape[0]
    tm = _token_tile(tp, ts)
    npt = tp // tm
    w_all, j = w
    wr_hi, wr_lo, br = router
    row = lambda i: (i, 0)
    const = lambda i: (0, 0)
    of_layer = lambda i: (layer, 0, 0)
    return pl.pallas_call(
        functools.partial(_proj_router_kernel, prompt_tiles=npt),
        grid=(t // tm,),
        in_specs=[pl.BlockSpec((tm, D_MODEL), row),
                  pl.BlockSpec((tm, D_MODEL), lambda i: (jnp.minimum(i, npt - 1), 0)),
                  pl.BlockSpec((tm, D_MODEL), lambda i: (jnp.maximum(i - npt, 0), 0)),
                  pl.BlockSpec((None, D_MODEL, D_MODEL), lambda i: (j, 0, 0)), pl.BlockSpec((1, D_MODEL), const),
                  pl.BlockSpec((1, D_MODEL), const), pl.BlockSpec((None, D_MODEL, LANES), of_layer),
                  pl.BlockSpec((None, D_MODEL, LANES), of_layer), pl.BlockSpec((None, 1, LANES), of_layer)],
        out_specs=[pl.BlockSpec((tm, D_MODEL), row), pl.BlockSpec((tm * ROW_TILES, LANES), row),
                   pl.BlockSpec((tm, LANES), row), pl.BlockSpec((1, LANES), const)],
        out_shape=[jax.ShapeDtypeStruct((t, D_MODEL), F32), jax.ShapeDtypeStruct((t * ROW_TILES, LANES), F32),
                   jax.ShapeDtypeStruct((t, LANES), F32), jax.ShapeDtypeStruct((1, LANES), F32)],
        scratch_shapes=[pltpu.VMEM((1, LANES), F32)],
        compiler_params=_params(("arbitrary",)),
        name="proj_router",
    )(x, z_p, z_s, w_all, b.reshape(1, D_MODEL), g.reshape(1, D_MODEL), wr_hi, wr_lo, br)


def _dispatch_plan(meta, counts, t):
    n = t * TOP_K
    n_blocks = -(-(n + N_EXPERTS * (EXPERT_BLOCK - 1)) // EXPERT_BLOCK)
    eid = meta[:, 0:TOP_K].astype(jnp.int32)
    rank = meta[:, TOP_K:2 * TOP_K].astype(jnp.int32)
    cnt = counts[0, ROUTER_LANE0:ROUTER_LANE0 + N_EXPERTS].astype(jnp.int32)
    padded = (cnt + EXPERT_BLOCK - 1) // EXPERT_BLOCK * EXPERT_BLOCK
    pad_end = jnp.cumsum(padded)
    pad_start = pad_end - padded
    experts = jnp.arange(N_EXPERTS, dtype=jnp.int32)
    start_of = jnp.sum(jnp.where(eid[..., None] == experts, pad_start, 0), axis=-1)
    dest = (start_of + rank).reshape(n)
    block_row0 = jnp.arange(n_blocks, dtype=jnp.int32) * EXPERT_BLOCK
    block_e = jnp.minimum(jnp.sum((pad_end[None, :] <= block_row0[:, None]).astype(jnp.int32), axis=1),
                          N_EXPERTS - 1)
    nused = (pad_end[-1] // EXPERT_BLOCK).reshape(1)
    zstart = jnp.where(padded > cnt, pad_end - EXPERT_BLOCK, -1)
    used = cnt > 0
    nxt_e = jnp.min(jnp.where((experts[None, :] > experts[:, None]) & used[None, :], experts[None, :], N_EXPERTS),
                    axis=1)
    slot_e = (jnp.cumsum(used.astype(jnp.int32)) - used.astype(jnp.int32)) % 2
    return dest, block_e, nused, zstart, nxt_e, slot_e, n_blocks


def _dispatch_kernel(dest_ref, zstart_ref, nused_ref, h_ref, xs_hbm, sbuf, zbuf, zsem, dsem, *, n_blocks):
    i = pl.program_id(0)
    tm = h_ref.shape[0] // ROW_TILES

    @pl.when(i == 0)
    def _():
        zbuf[...] = jnp.zeros_like(zbuf)
        nused = nused_ref[0]

        def zero_block(row0):
            rows = pl.ds(pl.multiple_of(row0 * ROW_TILES, EXPERT_BLOCK * ROW_TILES), EXPERT_BLOCK * ROW_TILES)
            return pltpu.make_async_copy(zbuf, xs_hbm.at[rows, :], zsem.at[0])

        def per_partial_block(fn):
            def body(e, carry):
                @pl.when(zstart_ref[e] >= 0)
                def _():
                    fn(pl.multiple_of(zstart_ref[e], EXPERT_BLOCK))
                return carry

            lax.fori_loop(0, N_EXPERTS, body, 0)

        def per_unused_block(fn):
            def body(j, carry):
                fn(pl.multiple_of(j * EXPERT_BLOCK, EXPERT_BLOCK))
                return carry

            lax.fori_loop(nused, n_blocks, body, 0)

        per_partial_block(lambda row0: zero_block(row0).start())
        per_unused_block(lambda row0: zero_block(row0).start())
        per_partial_block(lambda row0: zero_block(row0).wait())
        per_unused_block(lambda row0: zero_block(row0).wait())

    slot = i % 2
    sbuf[slot] = h_ref[...]

    in_flight = min(DISPATCH_LAG, tm * TOP_K // WAIT_CHUNK)

    @pl.when(i > 0)
    def _():
        _wait_row_copies(sbuf.at[1 - slot], xs_hbm, dsem.at[1 - slot], in_flight * WAIT_CHUNK)

    def start_token(j):
        tok = i * tm + j
        for k in range(TOP_K):
            _row_copy(sbuf.at[slot], j, xs_hbm, dest_ref[tok * TOP_K + k], dsem.at[slot]).start()

    _issue_rows_rolling(tm, start_token,
                        lambda: _wait_row_copies(sbuf.at[slot], xs_hbm, dsem.at[slot], WAIT_CHUNK), DISPATCH_LAG)

    @pl.when(i == pl.num_programs(0) - 1)
    def _():
        _wait_row_copies(sbuf.at[slot], xs_hbm, dsem.at[slot], in_flight * WAIT_CHUNK)


def _dispatch(h, dest, zstart, nused, n_blocks):
    t = h.shape[0] // ROW_TILES
    tm = _token_tile(t, largest=COMBINE_TILE)
    return pl.pallas_call(
        functools.partial(_dispatch_kernel, n_blocks=n_blocks),
        grid_spec=pltpu.PrefetchScalarGridSpec(
            num_scalar_prefetch=3, grid=(t // tm,),
            in_specs=[pl.BlockSpec((tm * ROW_TILES, LANES), lambda i, *_: (i, 0))],
            out_specs=pl.BlockSpec(memory_space=pl.ANY),
            scratch_shapes=[pltpu.VMEM((2, tm * ROW_TILES, LANES), F32),
                            pltpu.VMEM((EXPERT_BLOCK * ROW_TILES, LANES), F32),
                            pltpu.SemaphoreType.DMA((1,)), pltpu.SemaphoreType.DMA((2,))]),
        out_shape=jax.ShapeDtypeStruct((n_blocks * EXPERT_BLOCK * ROW_TILES, LANES), F32),
        compiler_params=_params(("arbitrary",)),
        name="dispatch",
    )(dest, zstart, nused, h)


def _experts_kernel(be_ref, nused_ref, nxt_ref, slot_ref, xs_ref, wg_hbm, wu_hbm, wd_hbm, ys_ref,
                    wbuf_g, wbuf_u, wbuf_d, wg_bf, wu_bf, wd_bf, wsem, *, layer):
    i = pl.program_id(0)
    nused = nused_ref[0]

    def weight_copies(e, slot):
        return (pltpu.make_async_copy(wg_hbm.at[layer, e], wbuf_g.at[slot], wsem.at[slot, 0]),
                pltpu.make_async_copy(wu_hbm.at[layer, e], wbuf_u.at[slot], wsem.at[slot, 1]),
                pltpu.make_async_copy(wd_hbm.at[layer, e], wbuf_d.at[slot], wsem.at[slot, 2]))

    @pl.when(i < nused)
    def _():
        e = be_ref[i]
        new_expert = (i == 0) | (e != be_ref[jnp.maximum(i - 1, 0)])

        @pl.when(new_expert)
        def _():
            slot = slot_ref[e]

            @pl.when(i == 0)
            def _():
                for c in weight_copies(e, slot):
                    c.start()

            for c in weight_copies(e, slot):
                c.wait()
            nxt = nxt_ref[e]

            @pl.when(nxt < N_EXPERTS)
            def _():
                for c in weight_copies(nxt, 1 - slot):
                    c.start()

            wg_bf[...] = wbuf_g[slot].astype(BF16)
            wu_bf[...] = wbuf_u[slot].astype(BF16)
            wd_bf[...] = wbuf_d[slot].astype(BF16)

        x = _from_token_major(xs_ref, EXPERT_BLOCK).astype(BF16)
        a = jnp.dot(x, wg_bf[...], preferred_element_type=F32)
        u = jnp.dot(x, wu_bf[...], preferred_element_type=F32)
        act = (jax.nn.silu(a) * u).astype(BF16)
        _to_token_major(ys_ref, jnp.dot(act, wd_bf[...], preferred_element_type=F32))

    @pl.when(i >= nused)
    def _():
        ys_ref[...] = jnp.zeros_like(ys_ref)


def _experts(xs, block_e, nused, nxt_e, slot_e, w_gate, w_up, w_down, layer):
    n_blocks = block_e.shape[0]
    return pl.pallas_call(
        functools.partial(_experts_kernel, layer=layer),
        grid_spec=pltpu.PrefetchScalarGridSpec(
            num_scalar_prefetch=4, grid=(n_blocks,),
            in_specs=[pl.BlockSpec((EXPERT_BLOCK * ROW_TILES, LANES),
                                   lambda i, be, nu, *_: (jnp.minimum(i, nu[0] - 1), 0)),
                      pl.BlockSpec(memory_space=pl.ANY), pl.BlockSpec(memory_space=pl.ANY),
                      pl.BlockSpec(memory_space=pl.ANY)],
            out_specs=pl.BlockSpec((EXPERT_BLOCK * ROW_TILES, LANES), lambda i, *_: (i, 0)),
            scratch_shapes=[pltpu.VMEM((2, D_MODEL, D_EXPERT), F32),
                            pltpu.VMEM((2, D_MODEL, D_EXPERT), F32),
                            pltpu.VMEM((2, D_EXPERT, D_MODEL), F32),
                            pltpu.VMEM((D_MODEL, D_EXPERT), BF16),
                            pltpu.VMEM((D_MODEL, D_EXPERT), BF16),
                            pltpu.VMEM((D_EXPERT, D_MODEL), BF16),
                            pltpu.SemaphoreType.DMA((2, 3))]),
        out_shape=jax.ShapeDtypeStruct((n_blocks * EXPERT_BLOCK * ROW_TILES, LANES), F32),
        compiler_params=_params(("arbitrary",)),
        name="experts",
    )(block_e, nused, nxt_e, slot_e, xs, w_gate, w_up, w_down)


def _final_kernel(dest_ref, x_ref, meta_ref, ys_hbm, g_ref, yp_ref, ys_ref, mbuf, sem, *, prompt_tiles):
    i = pl.program_id(0)
    y = _rms(_moe_combine(dest_ref, x_ref, meta_ref, ys_hbm, mbuf, sem), g_ref[...])

    @pl.when(i < prompt_tiles)
    def _():
        yp_ref[...] = y

    @pl.when(i >= prompt_tiles)
    def _():
        ys_ref[...] = y


def _final_norm(x, moe, g, *, tp, tm):
    t = x.shape[0]
    dest, meta, ys = moe
    npt = tp // tm
    c_specs, c_scratch = _combine_specs(tm)
    return pl.pallas_call(
        functools.partial(_final_kernel, prompt_tiles=npt),
        grid_spec=pltpu.PrefetchScalarGridSpec(
            num_scalar_prefetch=1, grid=(t // tm,),
            in_specs=c_specs + [pl.BlockSpec((1, D_MODEL), lambda i, *_: (0, 0))],
            out_specs=[pl.BlockSpec((tm, D_MODEL), lambda i, *_: (jnp.minimum(i, npt - 1), 0)),
                       pl.BlockSpec((tm, D_MODEL), lambda i, *_: (jnp.maximum(i - npt, 0), 0))],
            scratch_shapes=c_scratch),
        out_shape=[jax.ShapeDtypeStruct((tp, D_MODEL), F32), jax.ShapeDtypeStruct((t - tp, D_MODEL), F32)],
        compiler_params=_params(("arbitrary",)),
        name="final_norm",
    )(dest, x, meta, ys, g.reshape(1, D_MODEL))


def kernel(x_prompt, x_sample, cache_k, cache_v, state_conv, norm_mix, norm_ffn, norm_final, conv_w_pw1, conv_b_pw1, conv_w_dw, conv_b_dw, conv_ln_g, conv_ln_b, conv_w_pw2, conv_b_pw2, attn_w_qkv, attn_b_qkv, attn_sinks, attn_w_o, attn_b_o, moe_w_router_group, moe_b_router_group, moe_w_router_expert, moe_b_router_expert, moe_w_gate, moe_w_up, moe_w_down):
    batch, seq, _ = x_prompt.shape
    n_dec, dec_seq, _ = x_sample.shape
    depth = norm_mix.shape[0]
    tp, ts = batch * seq, n_dec * dec_seq
    t = tp + ts
    tm = _token_tile(tp, ts, largest=COMBINE_TILE)

    assert dec_seq <= CONV_WIDTH - 1 and dec_seq <= WINDOW

    def prompt_tail(a, n_rows):
        return jnp.stack([a[(b + 1) * seq - n_rows:(b + 1) * seq] for b in range(batch)])

    cache_k_t = cache_k.transpose(0, 1, 3, 2, 4)
    cache_v_t = cache_v.transpose(0, 1, 3, 2, 4)
    state_t = state_conv.transpose(0, 2, 1, 3)
    w_pw1, w_pw2 = conv_w_pw1.astype(BF16), conv_w_pw2.astype(BF16)
    w_qkv, w_o = attn_w_qkv.astype(BF16), attn_w_o.astype(BF16)
    router = _router_weights(moe_w_router_group, moe_b_router_group, moe_w_router_expert, moe_b_router_expert)

    x = (x_prompt.reshape(tp, D_MODEL), x_sample.reshape(ts, D_MODEL))
    moe = None
    kp_list, vp_list, cp_list, ks_list, vs_list, cs_list = [], [], [], [], [], []
    for layer in range(depth):
        j = layer // 2
        if layer % 2 == 0:
            x, u = _mixer_in(x, moe, norm_mix[layer], (w_pw1, j), conv_b_pw1[j], is_conv=True, tm=tm)
            u_s = u[tp:].reshape(n_dec, dec_seq, D_MODEL)
            cp_list.append(prompt_tail(u, CONV_WIDTH - 1))
            cs_list.append(u_s)
            z_p = _conv_prompt(u, conv_w_dw[j], conv_b_dw[j], conv_ln_g[j], conv_ln_b[j], batch=batch, seq=seq)
            z_s = _conv_sample(state_t, j, jnp.swapaxes(u_s, 0, 1), conv_w_dw[j], conv_b_dw[j],
                               conv_ln_g[j], conv_ln_b[j])
            z_s = jnp.swapaxes(z_s, 0, 1).reshape(ts, D_MODEL)
            w_out, b_out = (w_pw2, j), conv_b_pw2[j]
        else:
            x, q, k, v = _mixer_in(x, moe, norm_mix[layer], (w_qkv, j), attn_b_qkv[j], is_conv=False, tm=tm)
            kp_list.append(prompt_tail(k, WINDOW).reshape(batch, WINDOW, N_KV_HEADS, HEAD_DIM))
            vp_list.append(prompt_tail(v, WINDOW).reshape(batch, WINDOW, N_KV_HEADS, HEAD_DIM))
            k_s = k[tp:].reshape(n_dec, dec_seq, N_KV_HEADS, HEAD_DIM)
            v_s = v[tp:].reshape(n_dec, dec_seq, N_KV_HEADS, HEAD_DIM)
            ks_list.append(k_s)
            vs_list.append(v_s)
            z_p = _attn_prompt(q, k, v, attn_sinks[j], batch=batch, seq=seq)
            q_s = q[tp:].reshape(n_dec, dec_seq, N_KV_HEADS, GROUP, HEAD_DIM).transpose(0, 2, 3, 1, 4)
            q_s = q_s.reshape(n_dec, N_KV_HEADS, GROUP * dec_seq, HEAD_DIM)
            o_s = _attn_sample(q_s, cache_k_t, cache_v_t, j, k_s.transpose(0, 2, 1, 3), v_s.transpose(0, 2, 1, 3),
                               attn_sinks[j], dec_seq=dec_seq)
            o_s = o_s.reshape(n_dec, N_KV_HEADS, GROUP, dec_seq, HEAD_DIM).transpose(0, 3, 1, 2, 4)
            z_s = o_s.reshape(ts, D_MODEL)
            w_out, b_out = (w_o, j), attn_b_o[j]
        x, h, meta, counts = _proj_router(x, z_p, z_s, w_out, b_out, norm_ffn[layer], router, layer)
        dest, block_e, nused, zstart, nxt_e, slot_e, n_blocks = _dispatch_plan(meta, counts, t)
        xs = _dispatch(h, dest, zstart, nused, n_blocks)
        ys = _experts(xs, block_e, nused, nxt_e, slot_e, moe_w_gate, moe_w_up, moe_w_down, layer)
        moe = (dest, meta, ys)
    y_p, y_s = _final_norm(x, moe, norm_final, tp=tp, tm=tm)
    new_k_s = jnp.concatenate([cache_k[:, :, dec_seq:], jnp.stack(ks_list)], axis=2)
    new_v_s = jnp.concatenate([cache_v[:, :, dec_seq:], jnp.stack(vs_list)], axis=2)
    new_c_s = jnp.concatenate([state_conv[:, :, dec_seq:], jnp.stack(cs_list)], axis=2)
    return (y_p.reshape(batch, seq, D_MODEL), y_s.reshape(n_dec, dec_seq, D_MODEL),
            jnp.stack(kp_list), jnp.stack(vp_list), jnp.stack(cp_list), new_k_s, new_v_s, new_c_s)
```

```python
import functools

import jax
import jax.numpy as jnp
from jax import lax
from jax.experimental import pallas as pl
from jax.experimental.pallas import tpu as pltpu

F32 = jnp.float32
BF16 = jnp.bfloat16

D_MODEL = 1024
CONV_WIDTH = 31
N_HEADS = 16
N_KV_HEADS = 4
HEAD_DIM = D_MODEL // N_HEADS
GROUP = N_HEADS // N_KV_HEADS
KV_DIM = N_KV_HEADS * HEAD_DIM
WINDOW = 128
N_GROUPS = 8
EXPERTS_PER_GROUP = 8
N_EXPERTS = N_GROUPS * EXPERTS_PER_GROUP
TOP_K = 2
D_EXPERT = D_MODEL // 2
EXPERT_BLOCK = 256
RMS_EPS = 1e-6
LN_EPS = 1e-5

LANES = 128
SUBLANES = 8
ROW_TILES = D_MODEL // LANES
assert ROW_TILES == SUBLANES
ROUTER_LANE0 = N_GROUPS
META_GATE_LANE = 2 * TOP_K
CONV_HALO = 32
CONV_ROWS = 32
WAIT_CHUNK = 128
ISSUE_UNROLL = 8
DISPATCH_LAG = 2
COMBINE_LAG = 5
COMBINE_TILE = 512
VMEM_LIMIT =56 * 1024 * 1024


def _params(sem):
    return pltpu.CompilerParams(dimension_semantics=sem, vmem_limit_bytes=VMEM_LIMIT)


def _rms(x, g):
    return x * lax.rsqrt(jnp.mean(x * x, axis=-1, keepdims=True) + RMS_EPS) * g


def _token_tile(*counts, largest=512):
    for tm in (512, 256, 128):
        if tm <= largest and all(c % tm == 0 for c in counts):
            return tm
    raise ValueError("token counts must be multiples of 128")


def _to_token_major(ref, x):
    n = x.shape[0]
    for s in range(ROW_TILES):
        ref[pl.ds(s, n, stride=ROW_TILES), :] = x[:, s * LANES:(s + 1) * LANES]


def _from_token_major(ref, n):
    return jnp.concatenate([ref[pl.ds(s, n, stride=ROW_TILES), :] for s in range(ROW_TILES)], axis=1)


def _row_slice(row):
    start = row * ROW_TILES
    return pl.ds(start if isinstance(start, int) else pl.multiple_of(start, ROW_TILES), ROW_TILES)


def _row_copy(src, src_row, dst, dst_row, sem):
    return pltpu.make_async_copy(src.at[_row_slice(src_row), :], dst.at[_row_slice(dst_row), :], sem)


def _wait_row_copies(src_hbm, dst, sem, n_rows):
    def chunk(c, carry):
        for _ in range(WAIT_CHUNK):
            _row_copy(src_hbm, 0, dst, 0, sem).wait()
        return carry

    lax.fori_loop(0, n_rows // WAIT_CHUNK, chunk, 0)


def _issue_rows_rolling(n_tokens, start_token, wait_chunk, lag):
    chunk_tokens = WAIT_CHUNK // TOP_K
    n_chunks = n_tokens // chunk_tokens

    def chunk(c, carry):
        def body(j, inner):
            start_token(c * chunk_tokens + j)
            return inner

        lax.fori_loop(0, chunk_tokens, body, 0, unroll=ISSUE_UNROLL)

        @pl.when(c >= lag)
        def _():
            wait_chunk()

        return carry

    lax.fori_loop(0, n_chunks, chunk, 0)
    return min(lag, n_chunks)


def _moe_combine(dest_ref, x_ref, meta_ref, ys_hbm, mbuf, sem):
    i = pl.program_id(0)
    tm = x_ref.shape[0]

    def issue(tile, slot):
        def start_token(j):
            tok = tile * tm + j
            for k in range(TOP_K):
                _row_copy(ys_hbm, dest_ref[tok * TOP_K + k], mbuf.at[slot, k], j, sem.at[slot]).start()

        return _issue_rows_rolling(
            tm, start_token, lambda: _wait_row_copies(ys_hbm, mbuf.at[slot, 0], sem.at[slot], WAIT_CHUNK),
            COMBINE_LAG)

    in_flight = min(COMBINE_LAG, tm * TOP_K // WAIT_CHUNK)

    @pl.when(i == 0)
    def _():
        issue(0, 0)

    slot = i % 2
    _wait_row_copies(ys_hbm, mbuf.at[slot, 0], sem.at[slot], in_flight * WAIT_CHUNK)

    @pl.when(i + 1 < pl.num_programs(0))
    def _():
        issue(i + 1, (i + 1) % 2)

    moe = meta_ref[:, META_GATE_LANE:META_GATE_LANE + 1] * _from_token_major(mbuf.at[slot, 0], tm)
    for k in range(1, TOP_K):
        moe = moe + (meta_ref[:, META_GATE_LANE + k:META_GATE_LANE + k + 1]
                     * _from_token_major(mbuf.at[slot, k], tm))
    return x_ref[...] + moe


def _combine_specs(tm):
    in_specs = [pl.BlockSpec((tm, D_MODEL), lambda i, *_: (i, 0)),
                pl.BlockSpec((tm, LANES), lambda i, *_: (i, 0)),
                pl.BlockSpec(memory_space=pl.ANY)]
    scratch = [pltpu.VMEM((2, TOP_K, tm * ROW_TILES, LANES), F32), pltpu.SemaphoreType.DMA((2,))]
    return in_specs, scratch


def _mixer_proj(x, g_ref, w_ref, b_ref, outs, is_conv):
    h = _rms(x, g_ref[...]).astype(BF16)
    y = jnp.dot(h, w_ref[...], preferred_element_type=F32) + b_ref[...]
    if is_conv:
        (u_ref,) = outs
        u_ref[...] = y[:, :D_MODEL] * jax.nn.sigmoid(y[:, D_MODEL:])
    else:
        q_ref, k_ref, v_ref = outs
        q_ref[...] = (y[:, :D_MODEL] * (HEAD_DIM ** -0.5)).astype(BF16)
        k_ref[...] = y[:, D_MODEL:D_MODEL + KV_DIM]
        v_ref[...] = y[:, D_MODEL + KV_DIM:]


def _mixer_in_first_kernel(xp_ref, xs_ref, g_ref, w_ref, b_ref, xo_ref, *outs, is_conv, prompt_tiles):
    x = jnp.where(pl.program_id(0) < prompt_tiles, xp_ref[...], xs_ref[...])
    xo_ref[...] = x
    _mixer_proj(x, g_ref, w_ref, b_ref, outs, is_conv)


def _mixer_in_kernel(dest_ref, x_ref, meta_ref, ys_hbm, g_ref, w_ref, b_ref, xo_ref, *rest, is_conv):
    *outs, mbuf, sem = rest
    x = _moe_combine(dest_ref, x_ref, meta_ref, ys_hbm, mbuf, sem)
    xo_ref[...] = x
    _mixer_proj(x, g_ref, w_ref, b_ref, outs, is_conv)


def _mixer_in(x, moe, g, w, b, *, is_conv, tm):
    t = x.shape[0] if moe is not None else x[0].shape[0] + x[1].shape[0]
    n_out = w.shape[1]
    row = lambda i, *_: (i, 0)
    const = lambda i, *_: (0, 0)
    w_specs = [pl.BlockSpec((1, D_MODEL), const), pl.BlockSpec((D_MODEL, n_out), const),
               pl.BlockSpec((1, n_out), const)]
    w_args = [g.reshape(1, D_MODEL), w.astype(BF16), b.reshape(1, n_out)]
    if is_conv:
        out_shape = [jax.ShapeDtypeStruct((t, D_MODEL), F32)]
        out_specs = [pl.BlockSpec((tm, D_MODEL), row)]
    else:
        out_shape = [jax.ShapeDtypeStruct((t, D_MODEL), BF16), jax.ShapeDtypeStruct((t, KV_DIM), F32),
                     jax.ShapeDtypeStruct((t, KV_DIM), F32)]
        out_specs = [pl.BlockSpec((tm, D_MODEL), row), pl.BlockSpec((tm, KV_DIM), row),
                     pl.BlockSpec((tm, KV_DIM), row)]
    name = "mixer_in_conv" if is_conv else "mixer_in_attn"
    if moe is None:
        x_p, x_s = x
        npt = x_p.shape[0] // tm
        return pl.pallas_call(
            functools.partial(_mixer_in_first_kernel, is_conv=is_conv, prompt_tiles=npt),
            grid=(t // tm,),
            in_specs=[pl.BlockSpec((tm, D_MODEL), lambda i: (jnp.minimum(i, npt - 1), 0)),
                      pl.BlockSpec((tm, D_MODEL), lambda i: (jnp.maximum(i - npt, 0), 0))] + w_specs,
            out_specs=[pl.BlockSpec((tm, D_MODEL), row)] + out_specs,
            out_shape=[jax.ShapeDtypeStruct((t, D_MODEL), F32)] + out_shape,
            compiler_params=_params(("parallel",)), name=name + "_first",
        )(x_p, x_s, *w_args)
    dest, meta, ys = moe
    c_specs, c_scratch = _combine_specs(tm)
    return pl.pallas_call(
        functools.partial(_mixer_in_kernel, is_conv=is_conv),
        grid_spec=pltpu.PrefetchScalarGridSpec(
            num_scalar_prefetch=1, grid=(t // tm,),
            in_specs=c_specs + w_specs,
            out_specs=[pl.BlockSpec((tm, D_MODEL), row)] + out_specs,
            scratch_shapes=c_scratch),
        out_shape=[jax.ShapeDtypeStruct((t, D_MODEL), F32)] + out_shape,
        compiler_params=_params(("arbitrary",)), name=name,
    )(dest, x, meta, ys, *w_args)


def _ln_silu(c, g, b):
    mu = jnp.mean(c, axis=-1, keepdims=True)
    d = c - mu
    var = jnp.mean(d * d, axis=-1, keepdims=True)
    return jax.nn.silu(d * lax.rsqrt(var + LN_EPS) * g + b)


def _conv_prompt_kernel(u_ref, w_ref, bdw_ref, lg_ref, lb_ref, z_ref, ubuf, shifted, *, ts):
    s = pl.program_id(1)
    nbuf = ts + CONV_HALO

    @pl.when(s == 0)
    def _():
        ubuf[0:CONV_HALO, :] = jnp.zeros((CONV_HALO, D_MODEL), F32)

    ubuf[CONV_HALO:nbuf, :] = u_ref[...]
    ubuf[nbuf:nbuf + SUBLANES, :] = jnp.zeros((SUBLANES, D_MODEL), F32)

    def shift_chunk(c, carry):
        r0 = pl.multiple_of(c * CONV_ROWS, CONV_ROWS)
        win = ubuf[pl.ds(r0, CONV_ROWS + SUBLANES), :]
        for b in range(1, SUBLANES):
            rolled = pltpu.roll(win, CONV_ROWS + SUBLANES - b, axis=0)
            shifted[b - 1, pl.ds(r0, CONV_ROWS), :] = rolled[:CONV_ROWS]
        return carry

    lax.fori_loop(0, nbuf // CONV_ROWS, shift_chunk, 0)
    first = CONV_HALO - (CONV_WIDTH - 1)

    def chunk(c, carry):
        base = pl.multiple_of(c * CONV_ROWS, CONV_ROWS)
        groups = CONV_ROWS // SUBLANES
        accs = [jnp.zeros((SUBLANES, D_MODEL), F32) for _ in range(groups)]
        for k in range(CONV_WIDTH):
            a, b = divmod(first + k, SUBLANES)
            wk = w_ref[k]
            for g in range(groups):
                rows = pl.ds(base + (a + g) * SUBLANES, SUBLANES)
                tap = ubuf[rows, :] if b == 0 else shifted[b - 1, rows, :]
                accs[g] = accs[g] + wk * tap
        acc = jnp.concatenate(accs, axis=0)
        z = _ln_silu(acc + bdw_ref[...], lg_ref[...], lb_ref[...])
        z_ref[pl.ds(base, CONV_ROWS), :] = z.astype(BF16)
        return carry

    lax.fori_loop(0, ts // CONV_ROWS, chunk, 0)
    ubuf[0:CONV_HALO, :] = ubuf[ts:nbuf, :]


def _conv_prompt(u, w_dw, b_dw, ln_g, ln_b, *, batch, seq):
    ts = 512 if seq % 512 == 0 else seq
    ns = seq // ts
    const = lambda b, s: (0, 0)
    return pl.pallas_call(
        functools.partial(_conv_prompt_kernel, ts=ts),
        grid=(batch, ns),
        in_specs=[pl.BlockSpec((ts, D_MODEL), lambda b, s: (b * ns + s, 0)),
                  pl.BlockSpec((CONV_WIDTH, SUBLANES, D_MODEL), lambda b, s: (0, 0, 0)),
                  pl.BlockSpec((1, D_MODEL), const), pl.BlockSpec((1, D_MODEL), const),
                  pl.BlockSpec((1, D_MODEL), const)],
        out_specs=pl.BlockSpec((ts, D_MODEL), lambda b, s: (b * ns + s, 0)),
        out_shape=jax.ShapeDtypeStruct((batch * seq, D_MODEL), BF16),
        scratch_shapes=[pltpu.VMEM((ts + CONV_HALO + SUBLANES, D_MODEL), F32),
                        pltpu.VMEM((SUBLANES - 1, ts + CONV_HALO, D_MODEL), F32)],
        compiler_params=_params(("arbitrary", "arbitrary")),
        name="conv_prompt",
    )(u, jnp.broadcast_to(w_dw[:, None, :], (CONV_WIDTH, SUBLANES, D_MODEL)), b_dw.reshape(1, D_MODEL), ln_g.reshape(1, D_MODEL), ln_b.reshape(1, D_MODEL))


def _conv_sample_kernel(st_ref, us_ref, w_ref, bdw_ref, lg_ref, lb_ref, z_ref, *, dec_seq):
    n_past = st_ref.shape[0]
    for t in range(dec_seq):
        acc = jnp.zeros(us_ref.shape[1:], F32)
        for k in range(CONV_WIDTH):
            r = t + k
            acc = acc + w_ref[k:k + 1, :] * (st_ref[r] if r < n_past else us_ref[r - n_past])
        z_ref[t] = _ln_silu(acc + bdw_ref[...], lg_ref[...], lb_ref[...]).astype(BF16)


def _conv_sample(state_t, j, us_t, w_dw, b_dw, ln_g, ln_b):
    _, n_past, n, _ = state_t.shape
    dec_seq = us_t.shape[0]
    nb = 32 if n % 32 == 0 else n
    const = lambda i: (0, 0)
    return pl.pallas_call(
        functools.partial(_conv_sample_kernel, dec_seq=dec_seq),
        grid=(n // nb,),
        in_specs=[pl.BlockSpec((None, n_past, nb, D_MODEL), lambda i: (j, 0, i, 0)),
                  pl.BlockSpec((dec_seq, nb, D_MODEL), lambda i: (0, i, 0)),
                  pl.BlockSpec((CONV_WIDTH, D_MODEL), const),
                  pl.BlockSpec((1, D_MODEL), const), pl.BlockSpec((1, D_MODEL), const),
                  pl.BlockSpec((1, D_MODEL), const)],
        out_specs=pl.BlockSpec((dec_seq, nb, D_MODEL), lambda i: (0, i, 0)),
        out_shape=jax.ShapeDtypeStruct((dec_seq, n, D_MODEL), BF16),
        compiler_params=_params(("parallel",)),
        name="conv_sample",
    )(state_t, us_t, w_dw, b_dw.reshape(1, D_MODEL), ln_g.reshape(1, D_MODEL), ln_b.reshape(1, D_MODEL))


def _softmax_with_sink(s, allowed, sink):
    if allowed is not None:
        s = jnp.where(allowed, s, -jnp.inf)
    m =jnp.maximum(jnp.max(s, axis=-1, keepdims=True), sink)
    e = jnp.exp(s - m)
    return e / (jnp.sum(e, axis=-1, keepdims=True) + jnp.exp(sink - m))


def _attn_prompt_kernel(sink_ref, q_ref, kp_ref, kc_ref, vp_ref, vc_ref, o_ref, bias_ref):
    i = pl.program_id(1)
    blk = q_ref.shape[0]
    qi = lax.broadcasted_iota(jnp.int32, (blk, 2 * blk), 0)
    kc = lax.broadcasted_iota(jnp.int32, (blk, 2 * blk), 1)
    allowed = (kc >= qi) & (kc <= qi + WINDOW) & ((i > 0) | (kc >= blk))
    bias_ref[...] = jnp.where(allowed, 0.0, -jnp.inf)
    for hk in range(N_KV_HEADS):
        cols = slice(hk * HEAD_DIM, (hk + 1) * HEAD_DIM)
        kcat = jnp.concatenate([kp_ref[:, cols], kc_ref[:, cols]], axis=0).astype(BF16)
        vcat = jnp.concatenate([vp_ref[:, cols], vc_ref[:, cols]], axis=0).astype(BF16)
        for pair in range(GROUP // 2):
            outs = []
            for g in (2 * pair, 2 * pair + 1):
                h = hk * GROUP + g
                q = q_ref[:, h * HEAD_DIM:(h + 1) * HEAD_DIM]
                s = lax.dot_general(q, kcat, (((1,), (1,)), ((), ())), preferred_element_type=F32)
                p = _softmax_with_sink(s + bias_ref[...], None, sink_ref[h])
                outs.append(jnp.dot(p.astype(BF16), vcat, preferred_element_type=F32))
            lo = (hk * GROUP + 2 * pair) * HEAD_DIM
            o_ref[:, lo:lo + 2 * HEAD_DIM] = jnp.concatenate(outs, axis=-1).astype(BF16)


def _attn_prompt(q, k, v, sinks, *, batch, seq):
    blk = WINDOW
    nb = seq // blk
    cur = lambda b, i, *_: (b * nb + i, 0)
    prev = lambda b, i, *_: (b * nb + jnp.maximum(i - 1, 0), 0)
    return pl.pallas_call(
        _attn_prompt_kernel,
        grid_spec=pltpu.PrefetchScalarGridSpec(
            num_scalar_prefetch=1, grid=(batch, nb),
            in_specs=[pl.BlockSpec((blk, D_MODEL), cur),
                      pl.BlockSpec((blk, KV_DIM), prev), pl.BlockSpec((blk, KV_DIM), cur),
                      pl.BlockSpec((blk, KV_DIM), prev), pl.BlockSpec((blk, KV_DIM), cur)],
            out_specs=pl.BlockSpec((blk, D_MODEL), cur),
            scratch_shapes=[pltpu.VMEM((blk, 2 * blk), F32)]),
        out_shape=jax.ShapeDtypeStruct((batch * seq, D_MODEL), BF16),
        compiler_params=_params(("parallel", "parallel")),
        name="attn_prompt",
    )(sinks, q, k, k, v, v)


def _attn_sample_kernel(q_ref, kc_ref, vc_ref, kn_ref, vn_ref, sink_ref, o_ref, *, dec_seq):
    ns, kvh, nq, hd = q_ref.shape
    nc = kc_ref.shape[2]
    b = ns * kvh
    q = q_ref[...].reshape(b, nq, hd)
    kc = kc_ref[...].reshape(b, nc, hd).astype(BF16)
    vc = vc_ref[...].reshape(b, nc, hd).astype(BF16)
    qf = q.astype(F32)
    kn = kn_ref[...].reshape(b, dec_seq, hd).astype(BF16).astype(F32)
    vn = vn_ref[...].reshape(b, dec_seq, hd).astype(BF16).astype(F32)
    t_c = lax.broadcasted_iota(jnp.int32, (nq, nc), 0) % dec_seq
    c = lax.broadcasted_iota(jnp.int32, (nq, nc), 1)
    t_n = lax.broadcasted_iota(jnp.int32, (nq, 1), 0) % dec_seq
    s_c = jnp.einsum('bqd,bkd->bqk', q, kc, preferred_element_type=F32)
    s_c = jnp.where((c >= t_c)[None], s_c, -jnp.inf).reshape(ns, kvh, nq, nc)
    s_n = [jnp.where((t_n >= n)[None], jnp.sum(qf * kn[:, n:n + 1, :], axis=-1, keepdims=True), -jnp.inf)
           .reshape(ns, kvh, nq, 1) for n in range(dec_seq)]
    sink = sink_ref[...][None]
    m = jnp.maximum(jnp.max(s_c, axis=-1, keepdims=True), sink)
    for s in s_n:
        m = jnp.maximum(m, s)
    e_c = jnp.exp(s_c - m)
    e_n = [jnp.exp(s - m) for s in s_n]
    denom = jnp.sum(e_c, axis=-1, keepdims=True) + jnp.exp(sink - m)
    for e in e_n:
        denom = denom + e
    p_c = (e_c / denom).reshape(b, nq, nc).astype(BF16)
    o = jnp.einsum('bqk,bkd->bqd', p_c, vc, preferred_element_type=F32)
    for n in range(dec_seq):
        p_n = (e_n[n] / denom).reshape(b, nq, 1).astype(BF16).astype(F32)
        o = o + p_n * vn[:, n:n + 1, :]
    o_ref[...] = o.reshape(ns, kvh, nq, hd).astype(BF16)


def _attn_sample(q, cache_k_t, cache_v_t, j, k_new, v_new, sinks, *, dec_seq):
    n, kvh, nq, hd = q.shape
    nc = cache_k_t.shape[3]
    ns = 8 if n % 8 == 0 else n
    sink_rows = jnp.repeat(sinks.reshape(N_KV_HEADS, GROUP), dec_seq, axis=1)[..., None]
    blk4 = lambda i: (i, 0, 0, 0)
    cache_spec = pl.BlockSpec((None, ns, kvh, nc, hd), lambda i: (j, i, 0, 0, 0))
    return pl.pallas_call(
        functools.partial(_attn_sample_kernel, dec_seq=dec_seq),
        grid=(n // ns,),
        in_specs=[pl.BlockSpec((ns, kvh, nq, hd), blk4), cache_spec, cache_spec,
                  pl.BlockSpec((ns, kvh, dec_seq, hd), blk4), pl.BlockSpec((ns, kvh, dec_seq, hd), blk4),
                  pl.BlockSpec((kvh, nq, 1), lambda i: (0, 0, 0))],
        out_specs=pl.BlockSpec((ns, kvh, nq, hd), blk4),
        out_shape=jax.ShapeDtypeStruct((n, kvh, nq, hd), BF16),
        compiler_params=_params(("parallel",)),
        name="attn_sample",
    )(q, cache_k_t, cache_v_t, k_new, v_new, sink_rows)


def _proj_router_kernel(x_ref, zp_ref, zs_ref, w_ref, b_ref, g_ref, wrh_ref, wrl_ref, br_ref,
                        x1_ref, h_ref, meta_ref, cnt_ref, carry_ref, *, prompt_tiles):
    i = pl.program_id(0)
    tm = x_ref.shape[0]

    @pl.when(i == 0)
    def _():
        carry_ref[...] = jnp.zeros_like(carry_ref)

    z = jnp.where(i < prompt_tiles, zp_ref[...], zs_ref[...])
    x1 = x_ref[...] + (jnp.dot(z, w_ref[...], preferred_element_type=F32) + b_ref[...])
    x1_ref[...] = x1
    h = _rms(x1, g_ref[...])
    _to_token_major(h_ref, h)
    h_hi = h.astype(BF16)
    h_lo = (h - h_hi.astype(F32)).astype(BF16)
    logits = (jnp.dot(h_hi, wrh_ref[...], preferred_element_type=F32)
              + (jnp.dot(h_hi, wrl_ref[...], preferred_element_type=F32)
                 + jnp.dot(h_lo, wrh_ref[...], preferred_element_type=F32))) + br_ref[...]

    lane = lax.broadcasted_iota(jnp.int32, (tm, LANES), 1).astype(F32)
    first_lane = lambda hit: jnp.min(jnp.where(hit, lane, float(LANES)), axis=-1, keepdims=True)
    lg = jnp.where(lane < N_GROUPS, logits, -jnp.inf)
    mg = jnp.max(lg, axis=-1, keepdims=True)
    gsel = first_lane(lg == mg)
    gprob = 1.0 / jnp.sum(jnp.exp(lg - mg), axis=-1, keepdims=True)
    lo = ROUTER_LANE0 + gsel * EXPERTS_PER_GROUP
    in_group = (lane >= lo) & (lane < lo + EXPERTS_PER_GROUP)
    le = jnp.where(in_group, logits, -jnp.inf)
    ee = jnp.exp(le - jnp.max(le, axis=-1, keepdims=True))
    pe = jnp.where(in_group, ee / jnp.sum(ee, axis=-1, keepdims=True), -1.0)
    p1 = jnp.max(pe, axis=-1, keepdims=True)
    i1 = first_lane(pe == p1)
    pe2 = jnp.where(lane == i1, -1.0, pe)
    p2 = jnp.max(pe2, axis=-1, keepdims=True)
    i2 = first_lane(pe2 == p2)
    denom = p1 + p2
    g1 = gprob * p1 / denom
    g2 = gprob * p2 / denom
    sel1 = lane == i1
    sel2 = lane == i2
    onehot = jnp.where(sel1 | sel2, 1.0, 0.0)
    r = lax.broadcasted_iota(jnp.int32, (tm, tm), 0)
    c = lax.broadcasted_iota(jnp.int32, (tm, tm), 1)
    before = jnp.where(c < r, 1.0, 0.0).astype(BF16)
    rank = jnp.dot(before, onehot.astype(BF16), preferred_element_type=F32) + carry_ref[...]
    r1 = jnp.sum(jnp.where(sel1, rank, 0.0), axis=-1, keepdims=True)
    r2 = jnp.sum(jnp.where(sel2, rank, 0.0), axis=-1, keepdims=True)
    carry = carry_ref[...] + jnp.sum(onehot, axis=0, keepdims=True)
    carry_ref[...] = carry
    cnt_ref[...] = carry
    e1 = i1 - ROUTER_LANE0
    e2 = i2 - ROUTER_LANE0
    meta = jnp.zeros((tm, LANES), F32)
    for col, val in enumerate((e1, e2, r1, r2, g1, g2)):
        meta = jnp.where(lane == col, val, meta)
    meta_ref[...] = meta


def _proj_router(x, z_p, z_s, w, b, g, w_rg, b_rg, w_re, b_re):
    t = x.shape[0]
    tp, ts = z_p.shape[0], z_s.shape[0]
    tm = _token_tile(tp, ts)
    npt = tp // tm
    pad = LANES - N_GROUPS - N_EXPERTS
    wr = jnp.concatenate([w_rg, w_re, jnp.zeros((D_MODEL, pad), F32)], axis=1)
    wr_hi = wr.astype(BF16)
    wr_lo = (wr - wr_hi.astype(F32)).astype(BF16)
    br = jnp.concatenate([b_rg, b_re, jnp.zeros((pad,), F32)]).reshape(1, LANES)
    row = lambda i: (i, 0)
    const = lambda i: (0, 0)
    return pl.pallas_call(
        functools.partial(_proj_router_kernel, prompt_tiles=npt),
        grid=(t // tm,),
        in_specs=[pl.BlockSpec((tm, D_MODEL), row),
                  pl.BlockSpec((tm, D_MODEL), lambda i: (jnp.minimum(i, npt - 1), 0)),
                  pl.BlockSpec((tm, D_MODEL), lambda i: (jnp.maximum(i - npt, 0), 0)),
                  pl.BlockSpec((D_MODEL, D_MODEL), const), pl.BlockSpec((1, D_MODEL), const),
                  pl.BlockSpec((1, D_MODEL), const), pl.BlockSpec((D_MODEL, LANES), const),
                  pl.BlockSpec((D_MODEL, LANES), const), pl.BlockSpec((1, LANES), const)],
        out_specs=[pl.BlockSpec((tm, D_MODEL), row), pl.BlockSpec((tm * ROW_TILES, LANES), row),
                   pl.BlockSpec((tm, LANES), row), pl.BlockSpec((1, LANES), const)],
        out_shape=[jax.ShapeDtypeStruct((t, D_MODEL), F32), jax.ShapeDtypeStruct((t * ROW_TILES, LANES), F32),
                   jax.ShapeDtypeStruct((t, LANES), F32), jax.ShapeDtypeStruct((1, LANES), F32)],
        scratch_shapes=[pltpu.VMEM((1, LANES), F32)],
        compiler_params=_params(("arbitrary",)),
        name="proj_router",
    )(x, z_p, z_s, w.astype(BF16), b.reshape(1, D_MODEL), g.reshape(1, D_MODEL), wr_hi, wr_lo, br)


def _dispatch_plan(meta, counts, t):
    n = t * TOP_K
    n_blocks = -(-(n + N_EXPERTS * (EXPERT_BLOCK - 1)) // EXPERT_BLOCK)
    eid = meta[:, 0:TOP_K].astype(jnp.int32)
    rank = meta[:, TOP_K:2 * TOP_K].astype(jnp.int32)
    cnt = counts[0, ROUTER_LANE0:ROUTER_LANE0 + N_EXPERTS].astype(jnp.int32)
    padded = (cnt + EXPERT_BLOCK - 1) // EXPERT_BLOCK * EXPERT_BLOCK
    pad_end = jnp.cumsum(padded)
    pad_start = pad_end - padded
    experts = jnp.arange(N_EXPERTS, dtype=jnp.int32)
    start_of = jnp.sum(jnp.where(eid[..., None] == experts, pad_start, 0), axis=-1)
    dest = (start_of + rank).reshape(n)
    block_row0 = jnp.arange(n_blocks, dtype=jnp.int32) * EXPERT_BLOCK
    block_e = jnp.minimum(jnp.sum((pad_end[None, :] <= block_row0[:, None]).astype(jnp.int32), axis=1),
                          N_EXPERTS - 1)
    nused = (pad_end[-1] // EXPERT_BLOCK).reshape(1)
    zstart = jnp.where(padded > cnt, pad_end - EXPERT_BLOCK, -1)
    used = cnt > 0
    nxt_e = jnp.min(jnp.where((experts[None, :] > experts[:, None]) & used[None, :], experts[None, :], N_EXPERTS),
                    axis=1)
    slot_e = (jnp.cumsum(used.astype(jnp.int32)) - used.astype(jnp.int32)) % 2
    return dest, block_e, nused, zstart, nxt_e, slot_e, n_blocks


def _dispatch_kernel(dest_ref, zstart_ref, nused_ref, h_ref, wg_ref, wu_ref, wd_ref,
                     xs_hbm, wg_out, wu_out, wd_out, sbuf, zbuf, zsem, dsem, *, n_blocks):
    i = pl.program_id(0)
    tm = h_ref.shape[0] // ROW_TILES
    wg_out[...] = wg_ref[...].astype(BF16)
    wu_out[...] = wu_ref[...].astype(BF16)
    wd_out[...] = wd_ref[...].astype(BF16)

    @pl.when(i == 0)
    def _():
        zbuf[...] = jnp.zeros_like(zbuf)
        nused = nused_ref[0]

        def zero_block(row0):
            rows = pl.ds(pl.multiple_of(row0 * ROW_TILES, EXPERT_BLOCK * ROW_TILES), EXPERT_BLOCK * ROW_TILES)
            return pltpu.make_async_copy(zbuf, xs_hbm.at[rows, :], zsem.at[0])

        def per_partial_block(fn):
            def body(e, carry):
                @pl.when(zstart_ref[e] >= 0)
                def _():
                    fn(pl.multiple_of(zstart_ref[e], EXPERT_BLOCK))
                return carry

            lax.fori_loop(0, N_EXPERTS, body, 0)

        def per_unused_block(fn):
            def body(j, carry):
                fn(pl.multiple_of(j * EXPERT_BLOCK, EXPERT_BLOCK))
                return carry

            lax.fori_loop(nused, n_blocks, body, 0)

        per_partial_block(lambda row0: zero_block(row0).start())
        per_unused_block(lambda row0: zero_block(row0).start())
        per_partial_block(lambda row0: zero_block(row0).wait())
        per_unused_block(lambda row0: zero_block(row0).wait())

    slot = i % 2
    sbuf[slot] = h_ref[...]

    in_flight = min(DISPATCH_LAG, tm * TOP_K // WAIT_CHUNK)

    @pl.when(i > 0)
    def _():
        _wait_row_copies(sbuf.at[1 - slot], xs_hbm, dsem.at[1 - slot], in_flight * WAIT_CHUNK)

    def start_token(j):
        tok = i * tm + j
        for k in range(TOP_K):
            _row_copy(sbuf.at[slot], j, xs_hbm, dest_ref[tok * TOP_K + k], dsem.at[slot]).start()

    _issue_rows_rolling(tm, start_token,
                        lambda: _wait_row_copies(sbuf.at[slot], xs_hbm, dsem.at[slot], WAIT_CHUNK), DISPATCH_LAG)

    @pl.when(i == pl.num_programs(0) - 1)
    def _():
        _wait_row_copies(sbuf.at[slot], xs_hbm, dsem.at[slot], in_flight * WAIT_CHUNK)


def _dispatch(h, dest, zstart, nused, n_blocks, w_gate, w_up, w_down, layer):
    t = h.shape[0] // ROW_TILES
    tm = _token_tile(t, largest=COMBINE_TILE)
    steps = t // tm
    per_step = next(c for c in (1, 2, 4, 8, 16, 32, 64) if N_EXPERTS // c <= steps)
    w_blocks = N_EXPERTS // per_step
    w_in = lambda i, *_: (layer, jnp.minimum(i, w_blocks - 1), 0, 0)
    w_out = lambda i, *_: (jnp.minimum(i, w_blocks - 1), 0, 0)
    return pl.pallas_call(
        functools.partial(_dispatch_kernel, n_blocks=n_blocks),
        grid_spec=pltpu.PrefetchScalarGridSpec(
            num_scalar_prefetch=3, grid=(steps,),
            in_specs=[pl.BlockSpec((tm * ROW_TILES, LANES), lambda i, *_: (i, 0)),
                      pl.BlockSpec((None, per_step, D_MODEL, D_EXPERT), w_in),
                      pl.BlockSpec((None, per_step, D_MODEL, D_EXPERT), w_in),
                      pl.BlockSpec((None, per_step, D_EXPERT, D_MODEL), w_in)],
            out_specs=[pl.BlockSpec(memory_space=pl.ANY),
                       pl.BlockSpec((per_step, D_MODEL, D_EXPERT), w_out),
                       pl.BlockSpec((per_step, D_MODEL, D_EXPERT), w_out),
                       pl.BlockSpec((per_step, D_EXPERT, D_MODEL), w_out)],
            scratch_shapes=[pltpu.VMEM((2, tm * ROW_TILES, LANES), F32),
                            pltpu.VMEM((EXPERT_BLOCK * ROW_TILES, LANES), F32),
                            pltpu.SemaphoreType.DMA((1,)), pltpu.SemaphoreType.DMA((2,))]),
        out_shape=[jax.ShapeDtypeStruct((n_blocks * EXPERT_BLOCK * ROW_TILES, LANES), F32),
                   jax.ShapeDtypeStruct((N_EXPERTS, D_MODEL, D_EXPERT), BF16),
                   jax.ShapeDtypeStruct((N_EXPERTS, D_MODEL, D_EXPERT), BF16),
                   jax.ShapeDtypeStruct((N_EXPERTS, D_EXPERT, D_MODEL), BF16)],
        compiler_params=_params(("arbitrary",)),
        name="dispatch",
    )(dest, zstart, nused, h, w_gate, w_up, w_down)


def _experts_kernel(be_ref, nused_ref, nxt_ref, slot_ref, xs_ref, wg_hbm, wu_hbm, wd_hbm, ys_ref,
                    wbuf_g, wbuf_u, wbuf_d, wsem):
    i = pl.program_id(0)
    nused = nused_ref[0]

    def weight_copies(e, slot):
        return (pltpu.make_async_copy(wg_hbm.at[e], wbuf_g.at[slot], wsem.at[slot, 0]),
                pltpu.make_async_copy(wu_hbm.at[e], wbuf_u.at[slot], wsem.at[slot, 1]),
                pltpu.make_async_copy(wd_hbm.at[e], wbuf_d.at[slot], wsem.at[slot, 2]))

    @pl.when(i < nused)
    def _():
        e = be_ref[i]
        slot = slot_ref[e]
        new_expert = (i == 0) | (e != be_ref[jnp.maximum(i - 1, 0)])

        @pl.when(new_expert)
        def _():

            @pl.when(i == 0)
            def _():
                for c in weight_copies(e, slot):
                    c.start()

            for c in weight_copies(e, slot):
                c.wait()
            nxt = nxt_ref[e]

            @pl.when(nxt < N_EXPERTS)
            def _():
                for c in weight_copies(nxt, 1 - slot):
                    c.start()

        x = _from_token_major(xs_ref, EXPERT_BLOCK).astype(BF16)
        a = jnp.dot(x, wbuf_g[slot], preferred_element_type=F32)
        u = jnp.dot(x, wbuf_u[slot], preferred_element_type=F32)
        act = (jax.nn.silu(a) * u).astype(BF16)
        _to_token_major(ys_ref, jnp.dot(act, wbuf_d[slot], preferred_element_type=F32))

    @pl.when(i >= nused)
    def _():
        ys_ref[...] = jnp.zeros_like(ys_ref)


def _experts(xs, block_e, nused, nxt_e, slot_e, w_gate, w_up, w_down):
    n_blocks = block_e.shape[0]
    return pl.pallas_call(
        _experts_kernel,
        grid_spec=pltpu.PrefetchScalarGridSpec(
            num_scalar_prefetch=4, grid=(n_blocks,),
            in_specs=[pl.BlockSpec((EXPERT_BLOCK * ROW_TILES, LANES),
                                   lambda i, be, nu, *_: (jnp.minimum(i, nu[0] - 1), 0)),
                      pl.BlockSpec(memory_space=pl.ANY), pl.BlockSpec(memory_space=pl.ANY),
                      pl.BlockSpec(memory_space=pl.ANY)],
            out_specs=pl.BlockSpec((EXPERT_BLOCK * ROW_TILES, LANES), lambda i, *_: (i, 0)),
            scratch_shapes=[pltpu.VMEM((2, D_MODEL, D_EXPERT), BF16),
                            pltpu.VMEM((2, D_MODEL, D_EXPERT), BF16),
                            pltpu.VMEM((2, D_EXPERT, D_MODEL), BF16),
                            pltpu.SemaphoreType.DMA((2, 3))]),
        out_shape=jax.ShapeDtypeStruct((n_blocks * EXPERT_BLOCK * ROW_TILES, LANES), F32),
        compiler_params=_params(("arbitrary",)),
        name="experts",
    )(block_e, nused, nxt_e, slot_e, xs, w_gate, w_up, w_down)


def _final_kernel(dest_ref, x_ref, meta_ref, ys_hbm, g_ref, yp_ref, ys_ref, mbuf, sem, *, prompt_tiles):
    i = pl.program_id(0)
    y = _rms(_moe_combine(dest_ref, x_ref, meta_ref, ys_hbm, mbuf, sem), g_ref[...])

    @pl.when(i < prompt_tiles)
    def _():
        yp_ref[...] = y

    @pl.when(i >= prompt_tiles)
    def _():
        ys_ref[...] = y


def _final_norm(x, moe, g, *, tp, tm):
    t = x.shape[0]
    dest, meta, ys = moe
    npt = tp // tm
    c_specs, c_scratch = _combine_specs(tm)
    return pl.pallas_call(
        functools.partial(_final_kernel, prompt_tiles=npt),
        grid_spec=pltpu.PrefetchScalarGridSpec(
            num_scalar_prefetch=1, grid=(t // tm,),
            in_specs=c_specs + [pl.BlockSpec((1, D_MODEL), lambda i, *_: (0, 0))],
            out_specs=[pl.BlockSpec((tm, D_MODEL), lambda i, *_: (jnp.minimum(i, npt - 1), 0)),
                       pl.BlockSpec((tm, D_MODEL), lambda i, *_: (jnp.maximum(i - npt, 0), 0))],
            scratch_shapes=c_scratch),
        out_shape=[jax.ShapeDtypeStruct((tp, D_MODEL), F32), jax.ShapeDtypeStruct((t - tp, D_MODEL), F32)],
        compiler_params=_params(("arbitrary",)),
        name="final_norm",
    )(dest, x, meta, ys, g.reshape(1, D_MODEL))


def kernel(x_prompt, x_sample, cache_k, cache_v, state_conv, norm_mix, norm_ffn, norm_final, conv_w_pw1, conv_b_pw1, conv_w_dw, conv_b_dw, conv_ln_g, conv_ln_b, conv_w_pw2, conv_b_pw2, attn_w_qkv, attn_b_qkv, attn_sinks, attn_w_o, attn_b_o, moe_w_router_group, moe_b_router_group, moe_w_router_expert, moe_b_router_expert, moe_w_gate, moe_w_up, moe_w_down):
    batch, seq, _ = x_prompt.shape
    n_dec, dec_seq, _ = x_sample.shape
    depth = norm_mix.shape[0]
    tp, ts = batch * seq, n_dec * dec_seq
    t = tp + ts
    tm = _token_tile(tp, ts, largest=COMBINE_TILE)

    assert dec_seq <= CONV_WIDTH - 1 and dec_seq <= WINDOW

    def prompt_tail(a, n_rows):
        return jnp.stack([a[(b + 1) * seq - n_rows:(b + 1) * seq] for b in range(batch)])

    cache_k_t = cache_k.transpose(0, 1, 3, 2, 4)
    cache_v_t = cache_v.transpose(0, 1, 3, 2, 4)
    state_t = state_conv.transpose(0, 2, 1, 3)

    x = (x_prompt.reshape(tp, D_MODEL), x_sample.reshape(ts, D_MODEL))
    moe = None
    kp_list, vp_list, cp_list, ks_list, vs_list, cs_list = [], [], [], [], [], []
    for layer in range(depth):
        j = layer // 2
        if layer % 2 == 0:
            x, u = _mixer_in(x, moe, norm_mix[layer], conv_w_pw1[j], conv_b_pw1[j], is_conv=True, tm=tm)
            u_s = u[tp:].reshape(n_dec, dec_seq, D_MODEL)
            cp_list.append(prompt_tail(u, CONV_WIDTH - 1))
            cs_list.append(u_s)
            z_p = _conv_prompt(u, conv_w_dw[j], conv_b_dw[j], conv_ln_g[j], conv_ln_b[j], batch=batch, seq=seq)
            z_s = _conv_sample(state_t, j, jnp.swapaxes(u_s, 0, 1), conv_w_dw[j], conv_b_dw[j],
                               conv_ln_g[j], conv_ln_b[j])
            z_s = jnp.swapaxes(z_s, 0, 1).reshape(ts, D_MODEL)
            w_out, b_out = conv_w_pw2[j], conv_b_pw2[j]
        else:
            x, q, k, v = _mixer_in(x, moe, norm_mix[layer], attn_w_qkv[j], attn_b_qkv[j], is_conv=False, tm=tm)
            kp_list.append(prompt_tail(k, WINDOW).reshape(batch, WINDOW, N_KV_HEADS, HEAD_DIM))
            vp_list.append(prompt_tail(v, WINDOW).reshape(batch, WINDOW, N_KV_HEADS, HEAD_DIM))
            k_s = k[tp:].reshape(n_dec, dec_seq, N_KV_HEADS, HEAD_DIM)
            v_s = v[tp:].reshape(n_dec, dec_seq, N_KV_HEADS, HEAD_DIM)
            ks_list.append(k_s)
            vs_list.append(v_s)
            z_p = _attn_prompt(q, k, v, attn_sinks[j], batch=batch, seq=seq)
            q_s = q[tp:].reshape(n_dec, dec_seq, N_KV_HEADS, GROUP, HEAD_DIM).transpose(0, 2, 3, 1, 4)
            q_s = q_s.reshape(n_dec, N_KV_HEADS, GROUP * dec_seq, HEAD_DIM)
            o_s = _attn_sample(q_s, cache_k_t, cache_v_t, j, k_s.transpose(0, 2, 1, 3), v_s.transpose(0, 2, 1, 3),
                               attn_sinks[j], dec_seq=dec_seq)
            o_s = o_s.reshape(n_dec, N_KV_HEADS, GROUP, dec_seq, HEAD_DIM).transpose(0, 3, 1, 2, 4)
            z_s = o_s.reshape(ts, D_MODEL)
            w_out, b_out = attn_w_o[j], attn_b_o[j]
        x, h, meta, counts = _proj_router(x, z_p, z_s, w_out, b_out, norm_ffn[layer],
                                          moe_w_router_group[layer], moe_b_router_group[layer],
                                          moe_w_router_expert[layer], moe_b_router_expert[layer])
        dest, block_e, nused, zstart, nxt_e, slot_e, n_blocks = _dispatch_plan(meta, counts, t)
        xs, wg_bf, wu_bf, wd_bf = _dispatch(h, dest, zstart, nused, n_blocks,
                                            moe_w_gate, moe_w_up, moe_w_down, layer)
        ys = _experts(xs, block_e, nused, nxt_e, slot_e, wg_bf, wu_bf, wd_bf)
        moe = (dest, meta, ys)
    y_p, y_s = _final_norm(x, moe, norm_final, tp=tp, tm=tm)
    new_k_s = jnp.concatenate([cache_k[:, :, dec_seq:], jnp.stack(ks_list)], axis=2)
    new_v_s = jnp.concatenate([cache_v[:, :, dec_seq:], jnp.stack(vs_list)], axis=2)
    new_c_s = jnp.concatenate([state_conv[:, :, dec_seq:], jnp.stack(cs_list)], axis=2)
    return (y_p.reshape(batch, seq, D_MODEL), y_s.reshape(n_dec, dec_seq, D_MODEL),
            jnp.stack(kp_list), jnp.stack(vp_list), jnp.stack(cp_list), new_k_s, new_v_s, new_c_s)
```

```python
import functools

import jax
import jax.numpy as jnp
from jax import lax
from jax.experimental import pallas as pl
from jax.experimental.pallas import tpu as pltpu

F32 = jnp.float32
BF16 = jnp.bfloat16

D_MODEL = 1024
CONV_WIDTH = 31
N_HEADS = 16
N_KV_HEADS = 4
HEAD_DIM = D_MODEL // N_HEADS
GROUP = N_HEADS // N_KV_HEADS
KV_DIM = N_KV_HEADS * HEAD_DIM
WINDOW = 128
N_GROUPS = 8
EXPERTS_PER_GROUP = 8
N_EXPERTS = N_GROUPS * EXPERTS_PER_GROUP
TOP_K = 2
D_EXPERT = D_MODEL // 2
EXPERT_BLOCK = 256
RMS_EPS = 1e-6
LN_EPS = 1e-5

LANES = 128
SUBLANES = 8
ROW_TILES = D_MODEL // LANES
assert ROW_TILES == SUBLANES
ROUTER_LANE0 = N_GROUPS
META_GATE_LANE = 2 * TOP_K
CONV_HALO = 32
CONV_ROWS = 32
WAIT_CHUNK = 128
ISSUE_UNROLL = 8
DISPATCH_LAG = 2
COMBINE_LAG = 5
COMBINE_TILE = 512
VMEM_LIMIT =56 * 1024 * 1024


def _params(sem):
    return pltpu.CompilerParams(dimension_semantics=sem, vmem_limit_bytes=VMEM_LIMIT)


def _rms(x, g):
    return x * lax.rsqrt(jnp.mean(x * x, axis=-1, keepdims=True) + RMS_EPS) * g


def _token_tile(*counts, largest=512):
    for tm in (512, 256, 128):
        if tm <= largest and all(c % tm == 0 for c in counts):
            return tm
    raise ValueError("token counts must be multiples of 128")


def _to_token_major(ref, x):
    n = x.shape[0]
    for s in range(ROW_TILES):
        ref[pl.ds(s, n, stride=ROW_TILES), :] = x[:, s * LANES:(s + 1) * LANES]


def _from_token_major(ref, n):
    return jnp.concatenate([ref[pl.ds(s, n, stride=ROW_TILES), :] for s in range(ROW_TILES)], axis=1)


def _row_slice(row):
    start = row * ROW_TILES
    return pl.ds(start if isinstance(start, int) else pl.multiple_of(start, ROW_TILES), ROW_TILES)


def _row_copy(src, src_row, dst, dst_row, sem):
    return pltpu.make_async_copy(src.at[_row_slice(src_row), :], dst.at[_row_slice(dst_row), :], sem)


def _wait_row_copies(src_hbm, dst, sem, n_rows):
    def chunk(c, carry):
        for _ in range(WAIT_CHUNK):
            _row_copy(src_hbm, 0, dst, 0, sem).wait()
        return carry

    lax.fori_loop(0, n_rows // WAIT_CHUNK, chunk, 0)


def _issue_rows_rolling(n_tokens, start_token, wait_chunk, lag):
    chunk_tokens = WAIT_CHUNK // TOP_K
    n_chunks = n_tokens // chunk_tokens

    def chunk(c, carry):
        def body(j, inner):
            start_token(c * chunk_tokens + j)
            return inner

        lax.fori_loop(0, chunk_tokens, body, 0, unroll=ISSUE_UNROLL)

        @pl.when(c >= lag)
        def _():
            wait_chunk()

        return carry

    lax.fori_loop(0, n_chunks, chunk, 0)
    return min(lag, n_chunks)


def _moe_combine(dest_ref, x_ref, meta_ref, ys_hbm, mbuf, sem):
    i = pl.program_id(0)
    tm = x_ref.shape[0]

    def issue(tile, slot):
        def start_token(j):
            tok = tile * tm + j
            for k in range(TOP_K):
                _row_copy(ys_hbm, dest_ref[tok * TOP_K + k], mbuf.at[slot, k], j, sem.at[slot]).start(priority=k)

        return _issue_rows_rolling(
            tm, start_token, lambda: _wait_row_copies(ys_hbm, mbuf.at[slot, 0], sem.at[slot], WAIT_CHUNK),
            COMBINE_LAG)

    in_flight = min(COMBINE_LAG, tm * TOP_K // WAIT_CHUNK)

    @pl.when(i == 0)
    def _():
        issue(0, 0)

    slot = i % 2
    _wait_row_copies(ys_hbm, mbuf.at[slot, 0], sem.at[slot], in_flight * WAIT_CHUNK)

    @pl.when(i + 1 < pl.num_programs(0))
    def _():
        issue(i + 1, (i + 1) % 2)

    moe = meta_ref[:, META_GATE_LANE:META_GATE_LANE + 1] * _from_token_major(mbuf.at[slot, 0], tm)
    for k in range(1, TOP_K):
        moe = moe + (meta_ref[:, META_GATE_LANE + k:META_GATE_LANE + k + 1]
                     * _from_token_major(mbuf.at[slot, k], tm))
    return x_ref[...] + moe


def _combine_specs(tm):
    in_specs = [pl.BlockSpec((tm, D_MODEL), lambda i, *_: (i, 0)),
                pl.BlockSpec((tm, LANES), lambda i, *_: (i, 0)),
                pl.BlockSpec(memory_space=pl.ANY)]
    scratch = [pltpu.VMEM((2, TOP_K, tm * ROW_TILES, LANES), F32), pltpu.SemaphoreType.DMA((2,))]
    return in_specs, scratch


def _mixer_proj(x, g_ref, w_ref, b_ref, outs, is_conv):
    h = _rms(x, g_ref[...]).astype(BF16)
    y = jnp.dot(h, w_ref[...], preferred_element_type=F32) + b_ref[...]
    if is_conv:
        (u_ref,) = outs
        u_ref[...] = y[:, :D_MODEL] * jax.nn.sigmoid(y[:, D_MODEL:])
    else:
        q_ref, k_ref, v_ref = outs
        q_ref[...] = (y[:, :D_MODEL] * (HEAD_DIM ** -0.5)).astype(BF16)
        k_ref[...] = y[:, D_MODEL:D_MODEL + KV_DIM]
        v_ref[...] = y[:, D_MODEL + KV_DIM:]


def _mixer_in_first_kernel(xp_ref, xs_ref, g_ref, w_ref, b_ref, xo_ref, *outs, is_conv, prompt_tiles):
    x = jnp.where(pl.program_id(0) < prompt_tiles, xp_ref[...], xs_ref[...])
    xo_ref[...] = x
    _mixer_proj(x, g_ref, w_ref, b_ref, outs, is_conv)


def _mixer_in_kernel(dest_ref, x_ref, meta_ref, ys_hbm, g_ref, w_ref, b_ref, xo_ref, *rest, is_conv):
    *outs, mbuf, sem = rest
    x = _moe_combine(dest_ref, x_ref, meta_ref, ys_hbm, mbuf, sem)
    xo_ref[...] = x
    _mixer_proj(x, g_ref, w_ref, b_ref, outs, is_conv)


def _mixer_in(x, moe, g, w, b, *, is_conv, tm):
    t = x.shape[0] if moe is not None else x[0].shape[0] + x[1].shape[0]
    n_out = w.shape[1]
    row = lambda i, *_: (i, 0)
    const = lambda i, *_: (0, 0)
    w_specs = [pl.BlockSpec((1, D_MODEL), const), pl.BlockSpec((D_MODEL, n_out), const),
               pl.BlockSpec((1, n_out), const)]
    w_args = [g.reshape(1, D_MODEL), w.astype(BF16), b.reshape(1, n_out)]
    if is_conv:
        out_shape = [jax.ShapeDtypeStruct((t, D_MODEL), F32)]
        out_specs = [pl.BlockSpec((tm, D_MODEL), row)]
    else:
        out_shape = [jax.ShapeDtypeStruct((t, D_MODEL), BF16), jax.ShapeDtypeStruct((t, KV_DIM), F32),
                     jax.ShapeDtypeStruct((t, KV_DIM), F32)]
        out_specs = [pl.BlockSpec((tm, D_MODEL), row), pl.BlockSpec((tm, KV_DIM), row),
                     pl.BlockSpec((tm, KV_DIM), row)]
    name = "mixer_in_conv" if is_conv else "mixer_in_attn"
    if moe is None:
        x_p, x_s = x
        npt = x_p.shape[0] // tm
        return pl.pallas_call(
            functools.partial(_mixer_in_first_kernel, is_conv=is_conv, prompt_tiles=npt),
            grid=(t // tm,),
            in_specs=[pl.BlockSpec((tm, D_MODEL), lambda i: (jnp.minimum(i, npt - 1), 0)),
                      pl.BlockSpec((tm, D_MODEL), lambda i: (jnp.maximum(i - npt, 0), 0))] + w_specs,
            out_specs=[pl.BlockSpec((tm, D_MODEL), row)] + out_specs,
            out_shape=[jax.ShapeDtypeStruct((t, D_MODEL), F32)] + out_shape,
            compiler_params=_params(("parallel",)), name=name + "_first",
        )(x_p, x_s, *w_args)
    dest, meta, ys = moe
    c_specs, c_scratch = _combine_specs(tm)
    return pl.pallas_call(
        functools.partial(_mixer_in_kernel, is_conv=is_conv),
        grid_spec=pltpu.PrefetchScalarGridSpec(
            num_scalar_prefetch=1, grid=(t // tm,),
            in_specs=c_specs + w_specs,
            out_specs=[pl.BlockSpec((tm, D_MODEL), row)] + out_specs,
            scratch_shapes=c_scratch),
        out_shape=[jax.ShapeDtypeStruct((t, D_MODEL), F32)] + out_shape,
        compiler_params=_params(("arbitrary",)), name=name,
    )(dest, x, meta, ys, *w_args)


def _ln_silu(c, g, b):
    mu = jnp.mean(c, axis=-1, keepdims=True)
    d = c - mu
    var = jnp.mean(d * d, axis=-1, keepdims=True)
    return jax.nn.silu(d * lax.rsqrt(var + LN_EPS) * g + b)


def _conv_prompt_kernel(u_ref, w_ref, bdw_ref, lg_ref, lb_ref, z_ref, ubuf, shifted, *, ts):
    s = pl.program_id(1)
    nbuf = ts + CONV_HALO

    @pl.when(s == 0)
    def _():
        ubuf[0:CONV_HALO, :] = jnp.zeros((CONV_HALO, D_MODEL), F32)

    ubuf[CONV_HALO:nbuf, :] = u_ref[...]
    ubuf[nbuf:nbuf + SUBLANES, :] = jnp.zeros((SUBLANES, D_MODEL), F32)

    def shift_chunk(c, carry):
        r0 = pl.multiple_of(c * CONV_ROWS, CONV_ROWS)
        win = ubuf[pl.ds(r0, CONV_ROWS + SUBLANES), :]
        for b in range(1, SUBLANES):
            rolled = pltpu.roll(win, CONV_ROWS + SUBLANES - b, axis=0)
            shifted[b - 1, pl.ds(r0, CONV_ROWS), :] = rolled[:CONV_ROWS]
        return carry

    lax.fori_loop(0, nbuf // CONV_ROWS, shift_chunk, 0)
    first = CONV_HALO - (CONV_WIDTH - 1)

    def chunk(c, carry):
        base = pl.multiple_of(c * CONV_ROWS, CONV_ROWS)
        groups = CONV_ROWS // SUBLANES
        accs = [jnp.zeros((SUBLANES, D_MODEL), F32) for _ in range(groups)]
        for k in range(CONV_WIDTH):
            a, b = divmod(first + k, SUBLANES)
            wk = w_ref[k]
            for g in range(groups):
                rows = pl.ds(base + (a + g) * SUBLANES, SUBLANES)
                tap = ubuf[rows, :] if b == 0 else shifted[b - 1, rows, :]
                accs[g] = accs[g] + wk * tap
        acc = jnp.concatenate(accs, axis=0)
        z = _ln_silu(acc + bdw_ref[...], lg_ref[...], lb_ref[...])
        z_ref[pl.ds(base, CONV_ROWS), :] = z.astype(BF16)
        return carry

    lax.fori_loop(0, ts // CONV_ROWS, chunk, 0)
    ubuf[0:CONV_HALO, :] = ubuf[ts:nbuf, :]


def _conv_prompt(u, w_dw, b_dw, ln_g, ln_b, *, batch, seq):
    ts = 512 if seq % 512 == 0 else seq
    ns = seq // ts
    const = lambda b, s: (0, 0)
    return pl.pallas_call(
        functools.partial(_conv_prompt_kernel, ts=ts),
        grid=(batch, ns),
        in_specs=[pl.BlockSpec((ts, D_MODEL), lambda b, s: (b * ns + s, 0)),
                  pl.BlockSpec((CONV_WIDTH, SUBLANES, D_MODEL), lambda b, s: (0, 0, 0)),
                  pl.BlockSpec((1, D_MODEL), const), pl.BlockSpec((1, D_MODEL), const),
                  pl.BlockSpec((1, D_MODEL), const)],
        out_specs=pl.BlockSpec((ts, D_MODEL), lambda b, s: (b * ns + s, 0)),
        out_shape=jax.ShapeDtypeStruct((batch * seq, D_MODEL), BF16),
        scratch_shapes=[pltpu.VMEM((ts + CONV_HALO + SUBLANES, D_MODEL), F32),
                        pltpu.VMEM((SUBLANES - 1, ts + CONV_HALO, D_MODEL), F32)],
        compiler_params=_params(("arbitrary", "arbitrary")),
        name="conv_prompt",
    )(u, jnp.broadcast_to(w_dw[:, None, :], (CONV_WIDTH, SUBLANES, D_MODEL)), b_dw.reshape(1, D_MODEL), ln_g.reshape(1, D_MODEL), ln_b.reshape(1, D_MODEL))


def _conv_sample_kernel(st_ref, us_ref, w_ref, bdw_ref, lg_ref, lb_ref, z_ref, *, dec_seq):
    n_past = st_ref.shape[0]
    for t in range(dec_seq):
        acc = jnp.zeros(us_ref.shape[1:], F32)
        for k in range(CONV_WIDTH):
            r = t + k
            acc = acc + w_ref[k:k + 1, :] * (st_ref[r] if r < n_past else us_ref[r - n_past])
        z_ref[t] = _ln_silu(acc + bdw_ref[...], lg_ref[...], lb_ref[...]).astype(BF16)


def _conv_sample(state_t, j, us_t, w_dw, b_dw, ln_g, ln_b):
    _, n_past, n, _ = state_t.shape
    dec_seq = us_t.shape[0]
    nb = 32 if n % 32 == 0 else n
    const = lambda i: (0, 0)
    return pl.pallas_call(
        functools.partial(_conv_sample_kernel, dec_seq=dec_seq),
        grid=(n // nb,),
        in_specs=[pl.BlockSpec((None, n_past, nb, D_MODEL), lambda i: (j, 0, i, 0)),
                  pl.BlockSpec((dec_seq, nb, D_MODEL), lambda i: (0, i, 0)),
                  pl.BlockSpec((CONV_WIDTH, D_MODEL), const),
                  pl.BlockSpec((1, D_MODEL), const), pl.BlockSpec((1, D_MODEL), const),
                  pl.BlockSpec((1, D_MODEL), const)],
        out_specs=pl.BlockSpec((dec_seq, nb, D_MODEL), lambda i: (0, i, 0)),
        out_shape=jax.ShapeDtypeStruct((dec_seq, n, D_MODEL), BF16),
        compiler_params=_params(("parallel",)),
        name="conv_sample",
    )(state_t, us_t, w_dw, b_dw.reshape(1, D_MODEL), ln_g.reshape(1, D_MODEL), ln_b.reshape(1, D_MODEL))


def _softmax_with_sink(s, allowed, sink):
    if allowed is not None:
        s = jnp.where(allowed, s, -jnp.inf)
    m =jnp.maximum(jnp.max(s, axis=-1, keepdims=True), sink)
    e = jnp.exp(s - m)
    return e / (jnp.sum(e, axis=-1, keepdims=True) + jnp.exp(sink - m))


def _attn_prompt_kernel(sink_ref, q_ref, kp_ref, kc_ref, vp_ref, vc_ref, o_ref, bias_ref):
    i = pl.program_id(1)
    blk = q_ref.shape[0]
    qi = lax.broadcasted_iota(jnp.int32, (blk, 2 * blk), 0)
    kc = lax.broadcasted_iota(jnp.int32, (blk, 2 * blk), 1)
    allowed = (kc >= qi) & (kc <= qi + WINDOW) & ((i > 0) | (kc >= blk))
    bias_ref[...] = jnp.where(allowed, 0.0, -jnp.inf)
    for hk in range(N_KV_HEADS):
        cols = slice(hk * HEAD_DIM, (hk + 1) * HEAD_DIM)
        kcat = jnp.concatenate([kp_ref[:, cols], kc_ref[:, cols]], axis=0).astype(BF16)
        vcat = jnp.concatenate([vp_ref[:, cols], vc_ref[:, cols]], axis=0).astype(BF16)
        for pair in range(GROUP // 2):
            outs = []
            for g in (2 * pair, 2 * pair + 1):
                h = hk * GROUP + g
                q = q_ref[:, h * HEAD_DIM:(h + 1) * HEAD_DIM]
                s = lax.dot_general(q, kcat, (((1,), (1,)), ((), ())), preferred_element_type=F32)
                p = _softmax_with_sink(s + bias_ref[...], None, sink_ref[h])
                outs.append(jnp.dot(p.astype(BF16), vcat, preferred_element_type=F32))
            lo = (hk * GROUP + 2 * pair) * HEAD_DIM
            o_ref[:, lo:lo + 2 * HEAD_DIM] = jnp.concatenate(outs, axis=-1).astype(BF16)


def _attn_prompt(q, k, v, sinks, *, batch, seq):
    blk = WINDOW
    nb = seq // blk
    cur = lambda b, i, *_: (b * nb + i, 0)
    prev = lambda b, i, *_: (b * nb + jnp.maximum(i - 1, 0), 0)
    return pl.pallas_call(
        _attn_prompt_kernel,
        grid_spec=pltpu.PrefetchScalarGridSpec(
            num_scalar_prefetch=1, grid=(batch, nb),
            in_specs=[pl.BlockSpec((blk, D_MODEL), cur),
                      pl.BlockSpec((blk, KV_DIM), prev), pl.BlockSpec((blk, KV_DIM), cur),
                      pl.BlockSpec((blk, KV_DIM), prev), pl.BlockSpec((blk, KV_DIM), cur)],
            out_specs=pl.BlockSpec((blk, D_MODEL), cur),
            scratch_shapes=[pltpu.VMEM((blk, 2 * blk), F32)]),
        out_shape=jax.ShapeDtypeStruct((batch * seq, D_MODEL), BF16),
        compiler_params=_params(("parallel", "parallel")),
        name="attn_prompt",
    )(sinks, q, k, k, v, v)


def _attn_sample_kernel(q_ref, kc_ref, vc_ref, kn_ref, vn_ref, sink_ref, o_ref, *, dec_seq):
    ns, kvh, nq, hd = q_ref.shape
    nc = kc_ref.shape[2]
    b = ns * kvh
    q = q_ref[...].reshape(b, nq, hd)
    kc = kc_ref[...].reshape(b, nc, hd).astype(BF16)
    vc = vc_ref[...].reshape(b, nc, hd).astype(BF16)
    qf = q.astype(F32)
    kn = kn_ref[...].reshape(b, dec_seq, hd).astype(BF16).astype(F32)
    vn = vn_ref[...].reshape(b, dec_seq, hd).astype(BF16).astype(F32)
    t_c = lax.broadcasted_iota(jnp.int32, (nq, nc), 0) % dec_seq
    c = lax.broadcasted_iota(jnp.int32, (nq, nc), 1)
    t_n = lax.broadcasted_iota(jnp.int32, (nq, 1), 0) % dec_seq
    s_c = jnp.einsum('bqd,bkd->bqk', q, kc, preferred_element_type=F32)
    s_c = jnp.where((c >= t_c)[None], s_c, -jnp.inf).reshape(ns, kvh, nq, nc)
    s_n = [jnp.where((t_n >= n)[None], jnp.sum(qf * kn[:, n:n + 1, :], axis=-1, keepdims=True), -jnp.inf)
           .reshape(ns, kvh, nq, 1) for n in range(dec_seq)]
    sink = sink_ref[...][None]
    m = jnp.maximum(jnp.max(s_c, axis=-1, keepdims=True), sink)
    for s in s_n:
        m = jnp.maximum(m, s)
    e_c = jnp.exp(s_c - m)
    e_n = [jnp.exp(s - m) for s in s_n]
    denom = jnp.sum(e_c, axis=-1, keepdims=True) + jnp.exp(sink - m)
    for e in e_n:
        denom = denom + e
    p_c = (e_c / denom).reshape(b, nq, nc).astype(BF16)
    o = jnp.einsum('bqk,bkd->bqd', p_c, vc, preferred_element_type=F32)
    for n in range(dec_seq):
        p_n = (e_n[n] / denom).reshape(b, nq, 1).astype(BF16).astype(F32)
        o = o + p_n * vn[:, n:n + 1, :]
    o_ref[...] = o.reshape(ns, kvh, nq, hd).astype(BF16)


def _attn_sample(q, cache_k_t, cache_v_t, j, k_new, v_new, sinks, *, dec_seq):
    n, kvh, nq, hd = q.shape
    nc = cache_k_t.shape[3]
    ns = 8 if n % 8 == 0 else n
    sink_rows = jnp.repeat(sinks.reshape(N_KV_HEADS, GROUP), dec_seq, axis=1)[..., None]
    blk4 = lambda i: (i, 0, 0, 0)
    cache_spec = pl.BlockSpec((None, ns, kvh, nc, hd), lambda i: (j, i, 0, 0, 0))
    return pl.pallas_call(
        functools.partial(_attn_sample_kernel, dec_seq=dec_seq),
        grid=(n // ns,),
        in_specs=[pl.BlockSpec((ns, kvh, nq, hd), blk4), cache_spec, cache_spec,
                  pl.BlockSpec((ns, kvh, dec_seq, hd), blk4), pl.BlockSpec((ns, kvh, dec_seq, hd), blk4),
                  pl.BlockSpec((kvh, nq, 1), lambda i: (0, 0, 0))],
        out_specs=pl.BlockSpec((ns, kvh, nq, hd), blk4),
        out_shape=jax.ShapeDtypeStruct((n, kvh, nq, hd), BF16),
        compiler_params=_params(("parallel",)),
        name="attn_sample",
    )(q, cache_k_t, cache_v_t, k_new, v_new, sink_rows)


def _proj_router_kernel(x_ref, zp_ref, zs_ref, w_ref, b_ref, g_ref, wrh_ref, wrl_ref, br_ref,
                        x1_ref, h_ref, meta_ref, cnt_ref, carry_ref, *, prompt_tiles):
    i = pl.program_id(0)
    tm = x_ref.shape[0]

    @pl.when(i == 0)
    def _():
        carry_ref[...] = jnp.zeros_like(carry_ref)

    z = jnp.where(i < prompt_tiles, zp_ref[...], zs_ref[...])
    x1 = x_ref[...] + (jnp.dot(z, w_ref[...], preferred_element_type=F32) + b_ref[...])
    x1_ref[...] = x1
    h = _rms(x1, g_ref[...])
    _to_token_major(h_ref, h)
    h_hi = h.astype(BF16)
    h_lo = (h - h_hi.astype(F32)).astype(BF16)
    logits = (jnp.dot(h_hi, wrh_ref[...], preferred_element_type=F32)
              + (jnp.dot(h_hi, wrl_ref[...], preferred_element_type=F32)
                 + jnp.dot(h_lo, wrh_ref[...], preferred_element_type=F32))) + br_ref[...]

    lane = lax.broadcasted_iota(jnp.int32, (tm, LANES), 1).astype(F32)
    first_lane = lambda hit: jnp.min(jnp.where(hit, lane, float(LANES)), axis=-1, keepdims=True)
    lg = jnp.where(lane < N_GROUPS, logits, -jnp.inf)
    mg = jnp.max(lg, axis=-1, keepdims=True)
    gsel = first_lane(lg == mg)
    gprob = 1.0 / jnp.sum(jnp.exp(lg - mg), axis=-1, keepdims=True)
    lo = ROUTER_LANE0 + gsel * EXPERTS_PER_GROUP
    in_group = (lane >= lo) & (lane < lo + EXPERTS_PER_GROUP)
    le = jnp.where(in_group, logits, -jnp.inf)
    ee = jnp.exp(le - jnp.max(le, axis=-1, keepdims=True))
    pe = jnp.where(in_group, ee / jnp.sum(ee, axis=-1, keepdims=True), -1.0)
    p1 = jnp.max(pe, axis=-1, keepdims=True)
    i1 = first_lane(pe == p1)
    pe2 = jnp.where(lane == i1, -1.0, pe)
    p2 = jnp.max(pe2, axis=-1, keepdims=True)
    i2 = first_lane(pe2 == p2)
    denom = p1 + p2
    g1 = gprob * p1 / denom
    g2 = gprob * p2 / denom
    sel1 = lane == i1
    sel2 = lane == i2
    onehot = jnp.where(sel1 | sel2, 1.0, 0.0)
    r = lax.broadcasted_iota(jnp.int32, (tm, tm), 0)
    c = lax.broadcasted_iota(jnp.int32, (tm, tm), 1)
    before = jnp.where(c < r, 1.0, 0.0).astype(BF16)
    rank = jnp.dot(before, onehot.astype(BF16), preferred_element_type=F32) + carry_ref[...]
    r1 = jnp.sum(jnp.where(sel1, rank, 0.0), axis=-1, keepdims=True)
    r2 = jnp.sum(jnp.where(sel2, rank, 0.0), axis=-1, keepdims=True)
    carry = carry_ref[...] + jnp.sum(onehot, axis=0, keepdims=True)
    carry_ref[...] = carry
    cnt_ref[...] = carry
    e1 = i1 - ROUTER_LANE0
    e2 = i2 - ROUTER_LANE0
    meta = jnp.zeros((tm, LANES), F32)
    for col, val in enumerate((e1, e2, r1, r2, g1, g2)):
        meta = jnp.where(lane == col, val, meta)
    meta_ref[...] = meta


def _proj_router(x, z_p, z_s, w, b, g, w_rg, b_rg, w_re, b_re):
    t = x.shape[0]
    tp, ts = z_p.shape[0], z_s.shape[0]
    tm = _token_tile(tp, ts)
    npt = tp // tm
    pad = LANES - N_GROUPS - N_EXPERTS
    wr = jnp.concatenate([w_rg, w_re, jnp.zeros((D_MODEL, pad), F32)], axis=1)
    wr_hi = wr.astype(BF16)
    wr_lo = (wr - wr_hi.astype(F32)).astype(BF16)
    br = jnp.concatenate([b_rg, b_re, jnp.zeros((pad,), F32)]).reshape(1, LANES)
    row = lambda i: (i, 0)
    const = lambda i: (0, 0)
    return pl.pallas_call(
        functools.partial(_proj_router_kernel, prompt_tiles=npt),
        grid=(t // tm,),
        in_specs=[pl.BlockSpec((tm, D_MODEL), row),
                  pl.BlockSpec((tm, D_MODEL), lambda i: (jnp.minimum(i, npt - 1), 0)),
                  pl.BlockSpec((tm, D_MODEL), lambda i: (jnp.maximum(i - npt, 0), 0)),
                  pl.BlockSpec((D_MODEL, D_MODEL), const), pl.BlockSpec((1, D_MODEL), const),
                  pl.BlockSpec((1, D_MODEL), const), pl.BlockSpec((D_MODEL, LANES), const),
                  pl.BlockSpec((D_MODEL, LANES), const), pl.BlockSpec((1, LANES), const)],
        out_specs=[pl.BlockSpec((tm, D_MODEL), row), pl.BlockSpec((tm * ROW_TILES, LANES), row),
                   pl.BlockSpec((tm, LANES), row), pl.BlockSpec((1, LANES), const)],
        out_shape=[jax.ShapeDtypeStruct((t, D_MODEL), F32), jax.ShapeDtypeStruct((t * ROW_TILES, LANES), F32),
                   jax.ShapeDtypeStruct((t, LANES), F32), jax.ShapeDtypeStruct((1, LANES), F32)],
        scratch_shapes=[pltpu.VMEM((1, LANES), F32)],
        compiler_params=_params(("arbitrary",)),
        name="proj_router",
    )(x, z_p, z_s, w.astype(BF16), b.reshape(1, D_MODEL), g.reshape(1, D_MODEL), wr_hi, wr_lo, br)


def _dispatch_plan(meta, counts, t):
    n = t * TOP_K
    n_blocks = -(-(n + N_EXPERTS * (EXPERT_BLOCK - 1)) // EXPERT_BLOCK)
    eid = meta[:, 0:TOP_K].astype(jnp.int32)
    rank = meta[:, TOP_K:2 * TOP_K].astype(jnp.int32)
    cnt = counts[0, ROUTER_LANE0:ROUTER_LANE0 + N_EXPERTS].astype(jnp.int32)
    padded = (cnt + EXPERT_BLOCK - 1) // EXPERT_BLOCK * EXPERT_BLOCK
    pad_end = jnp.cumsum(padded)
    pad_start = pad_end - padded
    experts = jnp.arange(N_EXPERTS, dtype=jnp.int32)
    start_of = jnp.sum(jnp.where(eid[..., None] == experts, pad_start, 0), axis=-1)
    dest = (start_of + rank).reshape(n)
    block_row0 = jnp.arange(n_blocks, dtype=jnp.int32) * EXPERT_BLOCK
    block_e = jnp.minimum(jnp.sum((pad_end[None, :] <= block_row0[:, None]).astype(jnp.int32), axis=1),
                          N_EXPERTS - 1)
    nused = (pad_end[-1] // EXPERT_BLOCK).reshape(1)
    zstart = jnp.where(padded > cnt, pad_end - EXPERT_BLOCK, -1)
    used = cnt > 0
    nxt_e = jnp.min(jnp.where((experts[None, :] > experts[:, None]) & used[None, :], experts[None, :], N_EXPERTS),
                    axis=1)
    slot_e = (jnp.cumsum(used.astype(jnp.int32)) - used.astype(jnp.int32)) % 2
    return dest, block_e, nused, zstart, nxt_e, slot_e, n_blocks


def _dispatch_kernel(dest_ref, zstart_ref, nused_ref, h_ref, xs_hbm, sbuf, zbuf, zsem, dsem, *, n_blocks):
    i = pl.program_id(0)
    tm = h_ref.shape[0] // ROW_TILES

    @pl.when(i == 0)
    def _():
        zbuf[...] = jnp.zeros_like(zbuf)
        nused = nused_ref[0]

        def zero_block(row0):
            rows = pl.ds(pl.multiple_of(row0 * ROW_TILES, EXPERT_BLOCK * ROW_TILES), EXPERT_BLOCK * ROW_TILES)
            return pltpu.make_async_copy(zbuf, xs_hbm.at[rows, :], zsem.at[0])

        def per_partial_block(fn):
            def body(e, carry):
                @pl.when(zstart_ref[e] >= 0)
                def _():
                    fn(pl.multiple_of(zstart_ref[e], EXPERT_BLOCK))
                return carry

            lax.fori_loop(0, N_EXPERTS, body, 0)

        def per_unused_block(fn):
            def body(j, carry):
                fn(pl.multiple_of(j * EXPERT_BLOCK, EXPERT_BLOCK))
                return carry

            lax.fori_loop(nused, n_blocks, body, 0)

        per_partial_block(lambda row0: zero_block(row0).start())
        per_unused_block(lambda row0: zero_block(row0).start())
        per_partial_block(lambda row0: zero_block(row0).wait())
        per_unused_block(lambda row0: zero_block(row0).wait())

    slot = i % 2
    sbuf[slot] = h_ref[...]

    in_flight = min(DISPATCH_LAG, tm * TOP_K // WAIT_CHUNK)

    @pl.when(i > 0)
    def _():
        _wait_row_copies(sbuf.at[1 - slot], xs_hbm, dsem.at[1 - slot], in_flight * WAIT_CHUNK)

    def start_token(j):
        tok = i * tm + j
        for k in range(TOP_K):
            _row_copy(sbuf.at[slot], j, xs_hbm, dest_ref[tok * TOP_K + k], dsem.at[slot]).start(priority=k)

    _issue_rows_rolling(tm, start_token,
                        lambda: _wait_row_copies(sbuf.at[slot], xs_hbm, dsem.at[slot], WAIT_CHUNK), DISPATCH_LAG)

    @pl.when(i == pl.num_programs(0) - 1)
    def _():
        _wait_row_copies(sbuf.at[slot], xs_hbm, dsem.at[slot], in_flight * WAIT_CHUNK)


def _dispatch(h, dest, zstart, nused, n_blocks):
    t = h.shape[0] // ROW_TILES
    tm = _token_tile(t, largest=COMBINE_TILE)
    return pl.pallas_call(
        functools.partial(_dispatch_kernel, n_blocks=n_blocks),
        grid_spec=pltpu.PrefetchScalarGridSpec(
            num_scalar_prefetch=3, grid=(t // tm,),
            in_specs=[pl.BlockSpec((tm * ROW_TILES, LANES), lambda i, *_: (i, 0))],
            out_specs=pl.BlockSpec(memory_space=pl.ANY),
            scratch_shapes=[pltpu.VMEM((2, tm * ROW_TILES, LANES), F32),
                            pltpu.VMEM((EXPERT_BLOCK * ROW_TILES, LANES), F32),
                            pltpu.SemaphoreType.DMA((1,)), pltpu.SemaphoreType.DMA((2,))]),
        out_shape=jax.ShapeDtypeStruct((n_blocks * EXPERT_BLOCK * ROW_TILES, LANES), F32),
        compiler_params=_params(("arbitrary",)),
        name="dispatch",
    )(dest, zstart, nused, h)


def _experts_kernel(be_ref, nused_ref, nxt_ref, slot_ref, xs_ref, wg_hbm, wu_hbm, wd_hbm, ys_ref,
                    wbuf_g, wbuf_u, wbuf_d, wg_bf, wu_bf, wd_bf, wsem, *, layer):
    i = pl.program_id(0)
    nused = nused_ref[0]

    def weight_copies(e, slot):
        return (pltpu.make_async_copy(wg_hbm.at[layer, e], wbuf_g.at[slot], wsem.at[slot, 0]),
                pltpu.make_async_copy(wu_hbm.at[layer, e], wbuf_u.at[slot], wsem.at[slot, 1]),
                pltpu.make_async_copy(wd_hbm.at[layer, e], wbuf_d.at[slot], wsem.at[slot, 2]))

    @pl.when(i < nused)
    def _():
        e = be_ref[i]
        new_expert = (i == 0) | (e != be_ref[jnp.maximum(i - 1, 0)])

        @pl.when(new_expert)
        def _():
            slot = slot_ref[e]

            @pl.when(i == 0)
            def _():
                for c in weight_copies(e, slot):
                    c.start()

            for c in weight_copies(e, slot):
                c.wait()
            nxt = nxt_ref[e]

            @pl.when(nxt < N_EXPERTS)
            def _():
                for c in weight_copies(nxt, 1 - slot):
                    c.start()

            wg_bf[...] = wbuf_g[slot].astype(BF16)
            wu_bf[...] = wbuf_u[slot].astype(BF16)
            wd_bf[...] = wbuf_d[slot].astype(BF16)

        x = _from_token_major(xs_ref, EXPERT_BLOCK).astype(BF16)
        a = jnp.dot(x, wg_bf[...], preferred_element_type=F32)
        u = jnp.dot(x, wu_bf[...], preferred_element_type=F32)
        act = (jax.nn.silu(a) * u).astype(BF16)
        _to_token_major(ys_ref, jnp.dot(act, wd_bf[...], preferred_element_type=F32))

    @pl.when(i >= nused)
    def _():
        ys_ref[...] = jnp.zeros_like(ys_ref)


def _experts(xs, block_e, nused, nxt_e, slot_e, w_gate, w_up, w_down, layer):
    n_blocks = block_e.shape[0]
    return pl.pallas_call(
        functools.partial(_experts_kernel, layer=layer),
        grid_spec=pltpu.PrefetchScalarGridSpec(
            num_scalar_prefetch=4, grid=(n_blocks,),
            in_specs=[pl.BlockSpec((EXPERT_BLOCK * ROW_TILES, LANES),
                                   lambda i, be, nu, *_: (jnp.minimum(i, nu[0] - 1), 0)),
                      pl.BlockSpec(memory_space=pl.ANY), pl.BlockSpec(memory_space=pl.ANY),
                      pl.BlockSpec(memory_space=pl.ANY)],
            out_specs=pl.BlockSpec((EXPERT_BLOCK * ROW_TILES, LANES), lambda i, *_: (i, 0)),
            scratch_shapes=[pltpu.VMEM((2, D_MODEL, D_EXPERT), F32),
                            pltpu.VMEM((2, D_MODEL, D_EXPERT), F32),
                            pltpu.VMEM((2, D_EXPERT, D_MODEL), F32),
                            pltpu.VMEM((D_MODEL, D_EXPERT), BF16),
                            pltpu.VMEM((D_MODEL, D_EXPERT), BF16),
                            pltpu.VMEM((D_EXPERT, D_MODEL), BF16),
                            pltpu.SemaphoreType.DMA((2, 3))]),
        out_shape=jax.ShapeDtypeStruct((n_blocks * EXPERT_BLOCK * ROW_TILES, LANES), F32),
        compiler_params=_params(("arbitrary",)),
        name="experts",
    )(block_e, nused, nxt_e, slot_e, xs, w_gate, w_up, w_down)


def _final_kernel(dest_ref, x_ref, meta_ref, ys_hbm, g_ref, yp_ref, ys_ref, mbuf, sem, *, prompt_tiles):
    i = pl.program_id(0)
    y = _rms(_moe_combine(dest_ref, x_ref, meta_ref, ys_hbm, mbuf, sem), g_ref[...])

    @pl.when(i < prompt_tiles)
    def _():
        yp_ref[...] = y

    @pl.when(i >= prompt_tiles)
    def _():
        ys_ref[...] = y


def _final_norm(x, moe, g, *, tp, tm):
    t = x.shape[0]
    dest, meta, ys = moe
    npt = tp // tm
    c_specs, c_scratch = _combine_specs(tm)
    return pl.pallas_call(
        functools.partial(_final_kernel, prompt_tiles=npt),
        grid_spec=pltpu.PrefetchScalarGridSpec(
            num_scalar_prefetch=1, grid=(t // tm,),
            in_specs=c_specs + [pl.BlockSpec((1, D_MODEL), lambda i, *_: (0, 0))],
            out_specs=[pl.BlockSpec((tm, D_MODEL), lambda i, *_: (jnp.minimum(i, npt - 1), 0)),
                       pl.BlockSpec((tm, D_MODEL), lambda i, *_: (jnp.maximum(i - npt, 0), 0))],
            scratch_shapes=c_scratch),
        out_shape=[jax.ShapeDtypeStruct((tp, D_MODEL), F32), jax.ShapeDtypeStruct((t - tp, D_MODEL), F32)],
        compiler_params=_params(("arbitrary",)),
        name="final_norm",
    )(dest, x, meta, ys, g.reshape(1, D_MODEL))


def kernel(x_prompt, x_sample, cache_k, cache_v, state_conv, norm_mix, norm_ffn, norm_final, conv_w_pw1, conv_b_pw1, conv_w_dw, conv_b_dw, conv_ln_g, conv_ln_b, conv_w_pw2, conv_b_pw2, attn_w_qkv, attn_b_qkv, attn_sinks, attn_w_o, attn_b_o, moe_w_router_group, moe_b_router_group, moe_w_router_expert, moe_b_router_expert, moe_w_gate, moe_w_up, moe_w_down):
    batch, seq, _ = x_prompt.shape
    n_dec, dec_seq, _ = x_sample.shape
    depth = norm_mix.shape[0]
    tp, ts = batch * seq, n_dec * dec_seq
    t = tp + ts
    tm = _token_tile(tp, ts, largest=COMBINE_TILE)

    assert dec_seq <= CONV_WIDTH - 1 and dec_seq <= WINDOW

    def prompt_tail(a, n_rows):
        return jnp.stack([a[(b + 1) * seq - n_rows:(b + 1) * seq] for b in range(batch)])

    cache_k_t = cache_k.transpose(0, 1, 3, 2, 4)
    cache_v_t = cache_v.transpose(0, 1, 3, 2, 4)
    state_t = state_conv.transpose(0, 2, 1, 3)

    x = (x_prompt.reshape(tp, D_MODEL), x_sample.reshape(ts, D_MODEL))
    moe = None
    kp_list, vp_list, cp_list, ks_list, vs_list, cs_list = [], [], [], [], [], []
    for layer in range(depth):
        j = layer // 2
        if layer % 2 == 0:
            x, u = _mixer_in(x, moe, norm_mix[layer], conv_w_pw1[j], conv_b_pw1[j], is_conv=True, tm=tm)
            u_s = u[tp:].reshape(n_dec, dec_seq, D_MODEL)
            cp_list.append(prompt_tail(u, CONV_WIDTH - 1))
            cs_list.append(u_s)
            z_p = _conv_prompt(u, conv_w_dw[j], conv_b_dw[j], conv_ln_g[j], conv_ln_b[j], batch=batch, seq=seq)
            z_s = _conv_sample(state_t, j, jnp.swapaxes(u_s, 0, 1), conv_w_dw[j], conv_b_dw[j],
                               conv_ln_g[j], conv_ln_b[j])
            z_s = jnp.swapaxes(z_s, 0, 1).reshape(ts, D_MODEL)
            w_out, b_out = conv_w_pw2[j], conv_b_pw2[j]
        else:
            x, q, k, v = _mixer_in(x, moe, norm_mix[layer], attn_w_qkv[j], attn_b_qkv[j], is_conv=False, tm=tm)
            kp_list.append(prompt_tail(k, WINDOW).reshape(batch, WINDOW, N_KV_HEADS, HEAD_DIM))
            vp_list.append(prompt_tail(v, WINDOW).reshape(batch, WINDOW, N_KV_HEADS, HEAD_DIM))
            k_s = k[tp:].reshape(n_dec, dec_seq, N_KV_HEADS, HEAD_DIM)
            v_s = v[tp:].reshape(n_dec, dec_seq, N_KV_HEADS, HEAD_DIM)
            ks_list.append(k_s)
            vs_list.append(v_s)
            z_p = _attn_prompt(q, k, v, attn_sinks[j], batch=batch, seq=seq)
            q_s = q[tp:].reshape(n_dec, dec_seq, N_KV_HEADS, GROUP, HEAD_DIM).transpose(0, 2, 3, 1, 4)
            q_s = q_s.reshape(n_dec, N_KV_HEADS, GROUP * dec_seq, HEAD_DIM)
            o_s = _attn_sample(q_s, cache_k_t, cache_v_t, j, k_s.transpose(0, 2, 1, 3), v_s.transpose(0, 2, 1, 3),
                               attn_sinks[j], dec_seq=dec_seq)
            o_s = o_s.reshape(n_dec, N_KV_HEADS, GROUP, dec_seq, HEAD_DIM).transpose(0, 3, 1, 2, 4)
            z_s = o_s.reshape(ts, D_MODEL)
            w_out, b_out = attn_w_o[j], attn_b_o[j]
        x, h, meta, counts = _proj_router(x, z_p, z_s, w_out, b_out, norm_ffn[layer],
                                          moe_w_router_group[layer], moe_b_router_group[layer],
                                          moe_w_router_expert[layer], moe_b_router_expert[layer])
        dest, block_e, nused, zstart, nxt_e, slot_e, n_blocks = _dispatch_plan(meta, counts, t)
        xs = _dispatch(h, dest, zstart, nused, n_blocks)
        ys = _experts(xs, block_e, nused, nxt_e, slot_e, moe_w_gate, moe_w_up, moe_w_down, layer)
        moe = (dest, meta, ys)
    y_p, y_s = _final_norm(x, moe, norm_final, tp=tp, tm=tm)
    new_k_s = jnp.concatenate([cache_k[:, :, dec_seq:], jnp.stack(ks_list)], axis=2)
    new_v_s = jnp.concatenate([cache_v[:, :, dec_seq:], jnp.stack(vs_list)], axis=2)
    new_c_s = jnp.concatenate([state_conv[:, :, dec_seq:], jnp.stack(cs_list)], axis=2)
    return (y_p.reshape(batch, seq, D_MODEL), y_s.reshape(n_dec, dec_seq, D_MODEL),
            jnp.stack(kp_list), jnp.stack(vp_list), jnp.stack(cp_list), new_k_s, new_v_s, new_c_s)
```
